```python
import jax
import jax.numpy as jnp
from jax import lax
import numpy as np

D_MODEL = 2048
BATCH = 4
SEQ = 4096
DEPTH = 2

HEAD_DIM = 64
A_Q_HEADS = 16
A_KV_HEADS = 4
A_Q_WIDTH = A_Q_HEADS * HEAD_DIM
A_KV_WIDTH = A_KV_HEADS * HEAD_DIM
WINDOW = 128
ROPE_THETA = 10000.0
B_HEADS = 8
B_HEAD_DIM = 128
B_WIDTH = B_HEADS * B_HEAD_DIM
EVEN_IN_WIDTH = A_Q_WIDTH + 2 * A_KV_WIDTH + 4 * B_WIDTH
EVEN_MIX_WIDTH = A_Q_WIDTH + B_WIDTH
C_HEADS = 4
C_KEY_DIM = D_MODEL // 2
C_VALUE_DIM = D_MODEL
C_DK = C_KEY_DIM // C_HEADS
C_DV = C_VALUE_DIM // C_HEADS
GATE_RANK = 16
GATE_NORMALIZER = 16.0
ODD_IN_WIDTH = 2 * C_KEY_DIM + 2 * C_VALUE_DIM
CHUNK = 64
D_FF = 5632
CONV_WIDTH = 3
LN_EPS = 1e-5
RMS_EPS = 1e-6
ALPHA = (2.0 * DEPTH) ** 0.25
BETA = (8.0 * DEPTH) ** -0.25
NEG_INF = -1e30

kernel_name = 'hybrid_swa_hgrn2_gla_deepnorm_adaln'


def split_cols(t, widths):
    out, start = [], 0
    for w in widths:
        out.append(t[..., start:start + w])
        start += w
    return out


def layer_norm(x, g, b):
    xf = x.astype(jnp.float32)
    mu = jnp.mean(xf, axis=-1, keepdims=True)
    var = jnp.mean(jnp.square(xf - mu), axis=-1, keepdims=True)
    return ((xf - mu) * lax.rsqrt(var + LN_EPS)).astype(x.dtype) * g + b


def rms_norm_gated(o, g, w):
    of = o.astype(jnp.float32)
    of = of * lax.rsqrt(jnp.mean(of * of, axis=-1, keepdims=True) + RMS_EPS)
    return (of.astype(o.dtype) * w) * jax.nn.silu(g)


def rope(x, positions):
    half = x.shape[-1] // 2
    inv_freq = ROPE_THETA ** (-jnp.arange(half, dtype=jnp.float32) / half)
    ang = positions.astype(jnp.float32)[..., None] * inv_freq
    cos = jnp.cos(ang)[:, :, None, :]
    sin = jnp.sin(ang)[:, :, None, :]
    x1 = x[..., :half].astype(jnp.float32)
    x2 = x[..., half:].astype(jnp.float32)
    return jnp.concatenate([x1 * cos - x2 * sin, x2 * cos + x1 * sin], axis=-1).astype(x.dtype)


def swa_sink_attention(q, k, v, sinks):
    bsz, t, hq, hd = q.shape
    hkv = k.shape[2]
    grp = hq // hkv
    nb = t // WINDOW
    qb = q.reshape(bsz, nb, WINDOW, hkv, grp, hd)
    pad = ((0, 0), (WINDOW, 0), (0, 0), (0, 0))
    kp = jnp.pad(k, pad).reshape(bsz, nb + 1, WINDOW, hkv, hd)
    vp = jnp.pad(v, pad).reshape(bsz, nb + 1, WINDOW, hkv, hd)
    kb = jnp.concatenate([kp[:, :-1], kp[:, 1:]], axis=2)
    vb = jnp.concatenate([vp[:, :-1], vp[:, 1:]], axis=2)
    s = jnp.einsum('bnqhgd,bnkhd->bnhgqk', qb, kb).astype(jnp.float32) * (hd ** -0.5)
    qi = jnp.arange(WINDOW)[:, None] + WINDOW
    kj = jnp.arange(2 * WINDOW)[None, :]
    rel = qi - kj
    band = (rel >= 0) & (rel < WINDOW)
    blk = jnp.arange(nb)[:, None, None]
    valid = band[None] & ((blk > 0) | (kj[None] >= WINDOW))
    s = jnp.where(valid[None, :, None, None], s, NEG_INF)
    sink = sinks.astype(jnp.float32).reshape(1, 1, hkv, grp, 1, 1)
    m = jnp.maximum(jnp.max(s, axis=-1, keepdims=True), sink)
    p = jnp.exp(s - m)
    denom = jnp.sum(p, axis=-1, keepdims=True) + jnp.exp(sink - m)
    p = (p / denom).astype(v.dtype)
    o = jnp.einsum('bnhgqk,bnkhd->bnqhgd', p, vb)
    return o.reshape(bsz, t, hq * hd)


def chunk_gla(q, k, v, log_f):
    bsz, t, h, dk = q.shape
    dv = v.shape[-1]
    n = t // CHUNK

    def to_chunks(a):
        return a.astype(jnp.float32).reshape(bsz, n, CHUNK, h, a.shape[-1]).transpose(1, 0, 3, 2, 4)

    qc, kc, vc, gc = to_chunks(q), to_chunks(k), to_chunks(v), to_chunks(log_f)
    b = jnp.cumsum(gc, axis=3)
    b_last = b[:, :, :, -1:, :]
    q_e = qc * jnp.exp(b)
    k_e = kc * jnp.exp(-b)
    k_s = kc * jnp.exp(b_last - b)
    causal = jnp.tril(jnp.ones((CHUNK, CHUNK), dtype=bool))
    a = jnp.where(causal, jnp.einsum('nbhid,nbhjd->nbhij', q_e, k_e), 0.0)
    o_intra = jnp.einsum('nbhij,nbhjv->nbhiv', a, vc)
    decay = jnp.exp(b_last[:, :, :, 0, :])

    def step(state, inp):
        q_c, k_c, v_c, d_c = inp
        o = jnp.einsum('bhid,bhdv->bhiv', q_c, state)
        state = d_c[..., None] * state + jnp.einsum('bhjd,bhjv->bhdv', k_c, v_c)
        return state, o

    s0 = jnp.zeros((bsz, h, dk, dv), jnp.float32)
    _, o_inter = lax.scan(step, s0, (q_e, k_s, vc, decay))
    o = o_intra + o_inter
    return o.transpose(1, 0, 3, 2, 4).reshape(bsz, t, h, dv).astype(v.dtype)


def even_mixer(h, positions, w_in, w_out, sinks, lb, norm_w):
    bsz, t, _ = h.shape
    proj = h @ w_in
    q_a, k_a, v_a, q_b, f_b, i_b, g_b = split_cols(
        proj, [A_Q_WIDTH, A_KV_WIDTH, A_KV_WIDTH, B_WIDTH, B_WIDTH, B_WIDTH, B_WIDTH])
    q_a = rope(q_a.reshape(bsz, t, A_Q_HEADS, HEAD_DIM), positions)
    k_a = rope(k_a.reshape(bsz, t, A_KV_HEADS, HEAD_DIM), positions)
    v_a = v_a.reshape(bsz, t, A_KV_HEADS, HEAD_DIM)
    o_a = swa_sink_attention(q_a, k_a, v_a, sinks)
    lb = lb.astype(jnp.float32).reshape(B_HEADS, B_HEAD_DIM)
    fg = lb + (1.0 - lb) * jax.nn.sigmoid(f_b.reshape(bsz, t, B_HEADS, B_HEAD_DIM).astype(jnp.float32))
    k_b = (1.0 - fg).astype(h.dtype)
    o_b = chunk_gla(q_b.reshape(bsz, t, B_HEADS, B_HEAD_DIM), k_b,
                    i_b.reshape(bsz, t, B_HEADS, B_HEAD_DIM), jnp.log(fg))
    o_b = rms_norm_gated(o_b, g_b.reshape(bsz, t, B_HEADS, B_HEAD_DIM), norm_w).reshape(bsz, t, B_WIDTH)
    return jnp.concatenate([o_a, o_b], axis=-1) @ w_out


def odd_mixer(h, w_in, w_gk_a, w_gk_b, b_gk, norm_w, w_out):
    bsz, t, _ = h.shape
    q, k, v, g = split_cols(h @ w_in, [C_KEY_DIM, C_KEY_DIM, C_VALUE_DIM, C_VALUE_DIM])
    gk = (h @ w_gk_a) @ w_gk_b + b_gk
    log_f = jax.nn.log_sigmoid(gk.astype(jnp.float32)) / GATE_NORMALIZER
    o = chunk_gla((q * (C_DK ** -0.5)).reshape(bsz, t, C_HEADS, C_DK),
                  k.reshape(bsz, t, C_HEADS, C_DK),
                  v.reshape(bsz, t, C_HEADS, C_DV),
                  log_f.reshape(bsz, t, C_HEADS, C_DK))
    o = rms_norm_gated(o, g.reshape(bsz, t, C_HEADS, C_DV), norm_w).reshape(bsz, t, C_VALUE_DIM)
    return o @ w_out


def conv_glu_ffn(h, w_up, conv_w, conv_b, w_down):
    u, v = split_cols(h @ w_up, [D_FF, D_FF])
    u = lax.conv_general_dilated(u, conv_w[:, None, :].astype(u.dtype), window_strides=(1,),
                                 padding=[(CONV_WIDTH - 1, 0)],
                                 dimension_numbers=('NWC', 'WIO', 'NWC'),
                                 feature_group_count=D_FF) + conv_b
    return (jax.nn.silu(u) * v) @ w_down


def setup_inputs(seed: int = 0) -> dict:
    key = jax.random.key(seed)
    ks = iter(jax.random.split(key, 48))

    def nrm(shape, scale):
        return jax.random.normal(next(ks), shape, jnp.float32) * scale

    d = D_MODEL
    inp = {}
    inp['x'] = nrm((BATCH, SEQ, d), 1.0)
    inp['c'] = nrm((BATCH, d), 1.0)
    inp['positions'] = jnp.broadcast_to(jnp.arange(SEQ, dtype=jnp.int32), (BATCH, SEQ))
    inp['ada_w0'] = nrm((d, 6 * d), d ** -0.5)
    inp['ada_b0'] = nrm((6 * d,), 0.02)
    inp['mix_w_in0'] = nrm((d, EVEN_IN_WIDTH), d ** -0.5)
    inp['mix_w_out0'] = nrm((EVEN_MIX_WIDTH, d), BETA * EVEN_MIX_WIDTH ** -0.5)
    inp['attn_sinks0'] = nrm((A_Q_HEADS,), 0.5)
    inp['hgrn_lb_logits'] = nrm((DEPTH + 1, B_WIDTH), 0.1)
    inp['hgrn_norm_w0'] = 1.0 + nrm((B_HEAD_DIM,), 0.02)
    inp['ln_mix_g0'] = 1.0 + nrm((d,), 0.02)
    inp['ln_mix_b0'] = nrm((d,), 0.02)
    inp['ffn_w_up0'] = nrm((d, 2 * D_FF), d ** -0.5)
    inp['ffn_conv_w0'] = nrm((CONV_WIDTH, D_FF), CONV_WIDTH ** -0.5)
    inp['ffn_conv_b0'] = nrm((D_FF,), 0.02)
    inp['ffn_w_down0'] = nrm((D_FF, d), BETA * D_FF ** -0.5)
    inp['ln_ffn_g0'] = 1.0 + nrm((d,), 0.02)
    inp['ln_ffn_b0'] = nrm((d,), 0.02)
    inp['ada_w1'] = nrm((d, 6 * d), d ** -0.5)
    inp['ada_b1'] = nrm((6 * d,), 0.02)
    inp['mix_w_in1'] = nrm((d, ODD_IN_WIDTH), d ** -0.5)
    inp['gla_w_gk_a1'] = nrm((d, GATE_RANK), d ** -0.5)
    inp['gla_w_gk_b1'] = nrm((GATE_RANK, C_KEY_DIM), GATE_RANK ** -0.5)
    inp['gla_b_gk1'] = nrm((C_KEY_DIM,), 0.1)
    inp['gla_norm_w1'] = 1.0 + nrm((C_DV,), 0.02)
    inp['mix_w_out1'] = nrm((C_VALUE_DIM, d), BETA * C_VALUE_DIM ** -0.5)
    inp['ln_mix_g1'] = 1.0 + nrm((d,), 0.02)
    inp['ln_mix_b1'] = nrm((d,), 0.02)
    inp['ffn_w_up1'] = nrm((d, 2 * D_FF), d ** -0.5)
    inp['ffn_conv_w1'] = nrm((CONV_WIDTH, D_FF), CONV_WIDTH ** -0.5)
    inp['ffn_conv_b1'] = nrm((D_FF,), 0.02)
    inp['ffn_w_down1'] = nrm((D_FF, d), BETA * D_FF ** -0.5)
    inp['ln_ffn_g1'] = 1.0 + nrm((d,), 0.02)
    inp['ln_ffn_b1'] = nrm((d,), 0.02)
    return inp


def reference(x, c, positions,
              ada_w0, ada_b0, mix_w_in0, mix_w_out0, attn_sinks0, hgrn_lb_logits, hgrn_norm_w0,
              ln_mix_g0, ln_mix_b0, ffn_w_up0, ffn_conv_w0, ffn_conv_b0, ffn_w_down0, ln_ffn_g0, ln_ffn_b0,
              ada_w1, ada_b1, mix_w_in1, gla_w_gk_a1, gla_w_gk_b1, gla_b_gk1, gla_norm_w1, mix_w_out1,
              ln_mix_g1, ln_mix_b1, ffn_w_up1, ffn_conv_w1, ffn_conv_b1, ffn_w_down1, ln_ffn_g1, ln_ffn_b1):
    lb_table = jnp.cumsum(jax.nn.softmax(hgrn_lb_logits.astype(jnp.float32), axis=0), axis=0)
    c_act = jax.nn.silu(c)
    layers = [
        dict(ada_w=ada_w0, ada_b=ada_b0, w_in=mix_w_in0, w_out=mix_w_out0, sinks=attn_sinks0,
             norm_w=hgrn_norm_w0, ln_mix_g=ln_mix_g0, ln_mix_b=ln_mix_b0, ffn_w_up=ffn_w_up0,
             ffn_conv_w=ffn_conv_w0, ffn_conv_b=ffn_conv_b0, ffn_w_down=ffn_w_down0,
             ln_ffn_g=ln_ffn_g0, ln_ffn_b=ln_ffn_b0),
        dict(ada_w=ada_w1, ada_b=ada_b1, w_in=mix_w_in1, w_gk_a=gla_w_gk_a1, w_gk_b=gla_w_gk_b1,
             b_gk=gla_b_gk1, norm_w=gla_norm_w1, w_out=mix_w_out1, ln_mix_g=ln_mix_g1, ln_mix_b=ln_mix_b1,
             ffn_w_up=ffn_w_up1, ffn_conv_w=ffn_conv_w1, ffn_conv_b=ffn_conv_b1, ffn_w_down=ffn_w_down1,
             ln_ffn_g=ln_ffn_g1, ln_ffn_b=ln_ffn_b1),
    ]
    for layer in range(DEPTH):
        p = layers[layer]
        mod = c_act @ p['ada_w'] + p['ada_b']
        shift_m, scale_m, gate_m, shift_f, scale_f, gate_f = [m[:, None, :] for m in jnp.split(mod, 6, axis=-1)]
        h = x * (1.0 + scale_m) + shift_m
        if layer % 2 == 0:
            y = even_mixer(h, positions, p['w_in'], p['w_out'], p['sinks'], lb_table[layer], p['norm_w'])
        else:
            y = odd_mixer(h, p['w_in'], p['w_gk_a'], p['w_gk_b'], p['b_gk'], p['norm_w'], p['w_out'])
        x = layer_norm(ALPHA * x + gate_m * y, p['ln_mix_g'], p['ln_mix_b'])
        h = x * (1.0 + scale_f) + shift_f
        y = conv_glu_ffn(h, p['ffn_w_up'], p['ffn_conv_w'], p['ffn_conv_b'], p['ffn_w_down'])
        x = layer_norm(ALPHA * x + gate_f * y, p['ln_ffn_g'], p['ln_ffn_b'])
    return x
```

```python
import functools
import math

import jax
import jax.numpy as jnp
from jax import lax
from jax.experimental import pallas as pl
from jax.experimental.pallas import tpu as pltpu

F32 = jnp.float32
BF16 = jnp.bfloat16

D_MODEL = 2048
DEPTH = 2
HEAD_DIM = 64
A_Q_HEADS = 16
A_KV_HEADS = 4
A_Q_WIDTH = A_Q_HEADS * HEAD_DIM
A_KV_WIDTH = A_KV_HEADS * HEAD_DIM
WINDOW = 128
ROPE_THETA = 10000.0
B_HEADS = 8
B_HEAD_DIM = 128
B_WIDTH = B_HEADS * B_HEAD_DIM
EVEN_IN_WIDTH = A_Q_WIDTH + 2 * A_KV_WIDTH + 4 * B_WIDTH
C_HEADS = 4
C_KEY_DIM = D_MODEL // 2
C_VALUE_DIM = D_MODEL
C_DK = C_KEY_DIM // C_HEADS
C_DV = C_VALUE_DIM // C_HEADS
GATE_RANK = 16
GATE_NORMALIZER = 16.0
CHUNK = 64
D_FF = 5632
CONV_WIDTH = 3
LN_EPS = 1e-5
RMS_EPS = 1e-6
ALPHA = (2.0 * DEPTH) ** 0.25
NEG_INF = -1e30

V7X_LANES = 128
V7X_VMEM_BYTES = 64 * 1024 * 1024
V7X_VMEM_CEILING = 56 * 1024 * 1024
BF16_SUBLANE_PACK = 16

RANK_PAD = V7X_LANES


def _params(semantics, vmem_estimate):
    limit = min(V7X_VMEM_CEILING, max(16 * 1024 * 1024, int(vmem_estimate * 1.25)))
    return pltpu.CompilerParams(dimension_semantics=semantics, vmem_limit_bytes=limit)


def _resident(shape):
    return pl.BlockSpec(shape, lambda *_: (0,) * len(shape), pipeline_mode=pl.Buffered(1))


def _silu(x):
    return x * jax.nn.sigmoid(x)


def _layer_norm(z, g, b):
    mu = jnp.mean(z, axis=-1, keepdims=True)
    zc = z - mu
    var = jnp.mean(zc * zc, axis=-1, keepdims=True)
    return zc * lax.rsqrt(var + LN_EPS) * g + b


def _ada_kernel(c_ref, w_ref, b_ref, o_ref):
    a = _silu(c_ref[...]).astype(BF16)
    o_ref[...] = jnp.dot(a, w_ref[...].astype(BF16), preferred_element_type=F32) + b_ref[...]


def _ada_modulation(c_pad, ada_w, ada_b):
    rows, d = c_pad.shape
    n = ada_w.shape[1]
    tn = 1024
    est = 2 * d * tn * 4 + d * tn * 2 + 4 * rows * n
    return pl.pallas_call(
        _ada_kernel,
        out_shape=jax.ShapeDtypeStruct((rows, n), F32),
        grid=(n // tn,),
        in_specs=[pl.BlockSpec((rows, d), lambda j: (0, 0)),
                  pl.BlockSpec((d, tn), lambda j: (0, j)),
                  pl.BlockSpec((1, tn), lambda j: (0, j))],
        out_specs=pl.BlockSpec((rows, tn), lambda j: (0, j)),
        compiler_params=_params(("arbitrary",), est),
        name="ada_modulation",
    )(c_pad, ada_w, ada_b.reshape(1, n))


def _rope_table_kernel(pos_ref, invf_ref, sign_ref, cos_ref, sin_ref):
    ang = pos_ref[...].astype(F32) * invf_ref[...]
    cos_ref[...] = jnp.cos(ang)
    sin_ref[...] = jnp.sin(ang) * sign_ref[...]


def _rope_tables(pos_col):
    n = pos_col.shape[0]
    half = HEAD_DIM // 2
    lane = jnp.arange(V7X_LANES)
    inv_freq = ROPE_THETA ** (-jnp.arange(half, dtype=F32) / half)
    invf = inv_freq[lane % half].reshape(1, V7X_LANES)
    sign = jnp.where((lane % HEAD_DIM) < half, -1.0, 1.0).astype(F32).reshape(1, V7X_LANES)
    tm = 2048
    est = 2 * (tm * V7X_LANES * 4) * 3
    return pl.pallas_call(
        _rope_table_kernel,
        out_shape=(jax.ShapeDtypeStruct((n, V7X_LANES), F32),) * 2,
        grid=(n // tm,),
        in_specs=[pl.BlockSpec((tm, 1), lambda i: (i, 0)),
                  pl.BlockSpec((1, V7X_LANES), lambda i: (0, 0)),
                  pl.BlockSpec((1, V7X_LANES), lambda i: (0, 0))],
        out_specs=(pl.BlockSpec((tm, V7X_LANES), lambda i: (i, 0)),) * 2,
        compiler_params=_params(("arbitrary",), est),
        name="rope_tables",
    )(pos_col, invf, sign)


def _rope(a, cos, sin_signed, first_half):
    outs = []
    for s in range(a.shape[1] // V7X_LANES):
        blk = a[:, s * V7X_LANES:(s + 1) * V7X_LANES]
        partner = jnp.where(first_half,
                            pltpu.roll(blk, V7X_LANES - HEAD_DIM // 2, 1),
                            pltpu.roll(blk, HEAD_DIM // 2, 1))
        outs.append(blk * cos + partner * sin_signed)
    return jnp.concatenate(outs, axis=1) if len(outs) > 1 else outs[0]


PROJ_COLS = 512


def _inproj_even_kernel(x_ref, shift_ref, scale_ref, w_ref, cos_ref, sin_ref, lbl_ref,
                        qa_ref, ka_ref, va_ref, qb_ref, kb_ref, lf_ref, ib_ref, gb_ref, h_scr, *, layer):
    tm = x_ref.shape[0]
    h_scr[...] = (x_ref[...] * (1.0 + scale_ref[...]) + shift_ref[...]).astype(BF16)

    def proj(c0, width=PROJ_COLS):
        return jnp.dot(h_scr[...], w_ref[:, c0:c0 + width], preferred_element_type=F32)

    cos = cos_ref[...]
    sin_signed = sin_ref[...]
    lane = lax.broadcasted_iota(jnp.int32, (tm, V7X_LANES), 1)
    first_half = (lane % HEAD_DIM) < (HEAD_DIM // 2)

    for c0 in range(0, A_Q_WIDTH, PROJ_COLS):
        qa_ref[:, c0:c0 + PROJ_COLS] = (_rope(proj(c0), cos, sin_signed, first_half)
                                        * (HEAD_DIM ** -0.5)).astype(BF16)
    kv = proj(A_Q_WIDTH, 2 * A_KV_WIDTH)
    ka_ref[...] = _rope(kv[:, :A_KV_WIDTH], cos, sin_signed, first_half).astype(BF16)
    va_ref[...] = kv[:, A_KV_WIDTH:].astype(BF16)

    base = A_Q_WIDTH + 2 * A_KV_WIDTH
    lg = lbl_ref[...]
    e = jnp.exp(lg - jnp.max(lg, axis=0, keepdims=True))
    sm = e / jnp.sum(e, axis=0, keepdims=True)
    lb = jnp.sum(sm[0:layer + 1], axis=0, keepdims=True)
    for c0 in range(0, B_WIDTH, PROJ_COLS):
        sl = slice(c0, c0 + PROJ_COLS)
        qb_ref[:, sl] = proj(base + c0)
        lbc = lb[:, sl]
        fg = lbc + (1.0 - lbc) * jax.nn.sigmoid(proj(base + B_WIDTH + c0))
        kb_ref[:, sl] = 1.0 - fg
        lf_ref[:, sl] = jnp.log(fg)
        ib_ref[:, sl] = proj(base + 2 * B_WIDTH + c0).astype(BF16)
        gb_ref[:, sl] = proj(base + 3 * B_WIDTH + c0)


def _inproj_even(x2, mod3, w_bf, cos, sin_signed, lb_logits, *, seq, layer):
    n, d = x2.shape
    tm = 256
    per_seq = seq // tm
    width = w_bf.shape[1]
    row = lambda i: (i, 0)
    out_shapes = (
        jax.ShapeDtypeStruct((n, A_Q_WIDTH), BF16), jax.ShapeDtypeStruct((n, A_KV_WIDTH), BF16),
        jax.ShapeDtypeStruct((n, A_KV_WIDTH), BF16), jax.ShapeDtypeStruct((n, B_WIDTH), F32),
        jax.ShapeDtypeStruct((n, B_WIDTH), F32), jax.ShapeDtypeStruct((n, B_WIDTH), F32),
        jax.ShapeDtypeStruct((n, B_WIDTH), BF16), jax.ShapeDtypeStruct((n, B_WIDTH), F32))
    out_bytes = sum(tm * s.shape[1] * s.dtype.itemsize for s in out_shapes)
    est = d * width * 2 + 2 * tm * d * 4 + 2 * out_bytes + tm * d * 2 + 6 * tm * PROJ_COLS * 4
    return pl.pallas_call(
        functools.partial(_inproj_even_kernel, layer=layer),
        out_shape=out_shapes,
        grid=(n // tm,),
        in_specs=[pl.BlockSpec((tm, d), row),
                  pl.BlockSpec((None, 1, d), lambda i: ((i // per_seq) * 6 + 0, 0, 0)),
                  pl.BlockSpec((None, 1, d), lambda i: ((i // per_seq) * 6 + 1, 0, 0)),
                  _resident((d, width)),
                  pl.BlockSpec((tm, V7X_LANES), row),
                  pl.BlockSpec((tm, V7X_LANES), row),
                  pl.BlockSpec(lb_logits.shape, lambda i: (0, 0))],
        out_specs=tuple(pl.BlockSpec((tm, s.shape[1]), row) for s in out_shapes),
        scratch_shapes=[pltpu.VMEM((tm, d), BF16)],
        compiler_params=_params(("arbitrary",), est),
        name="inproj_even",
    )(x2, mod3, mod3, w_bf, cos, sin_signed, lb_logits)


def _swa_kernel(sink_ref, q_ref, kp_ref, kc_ref, vp_ref, vc_ref, o_ref):
    n = pl.program_id(1)
    grp = A_Q_HEADS // A_KV_HEADS
    r = lax.broadcasted_iota(jnp.int32, (WINDOW, 2 * WINDOW), 0)
    c = lax.broadcasted_iota(jnp.int32, (WINDOW, 2 * WINDOW), 1)
    rel = r + WINDOW - c
    valid = (rel >= 0) & (rel < WINDOW) & ((c >= WINDOW) | (n > 0))
    q = q_ref[...]
    kcat = jnp.concatenate([kp_ref[...], kc_ref[...]], axis=0)
    vcat = jnp.concatenate([vp_ref[...], vc_ref[...]], axis=0)
    outs = []
    for j in range(A_Q_HEADS):
        g = j // grp
        qh = q[:, j * HEAD_DIM:(j + 1) * HEAD_DIM]
        kh = kcat[:, g * HEAD_DIM:(g + 1) * HEAD_DIM]
        vh = vcat[:, g * HEAD_DIM:(g + 1) * HEAD_DIM]
        s = lax.dot_general(qh, kh, (((1,), (1,)), ((), ())), preferred_element_type=F32)
        s = jnp.where(valid, s, NEG_INF)
        sink = sink_ref[j]
        m = jnp.maximum(jnp.max(s, axis=-1, keepdims=True), sink)
        p = jnp.exp(s - m)
        denom = jnp.sum(p, axis=-1, keepdims=True) + jnp.exp(sink - m)
        p = (p / denom).astype(BF16)
        outs.append(jnp.dot(p, vh, preferred_element_type=F32))
    o_ref[...] = jnp.concatenate(outs, axis=1).astype(BF16)


def _swa_attention(q_a, k_a, v_a, sinks, *, batch, seq):
    n = q_a.shape[0]
    nb = seq // WINDOW
    cur = lambda b, i: (b * nb + i, 0)
    prev = lambda b, i: (b * nb + jnp.maximum(i - 1, 0), 0)
    est = 2 * (2 * WINDOW * A_Q_WIDTH * 2 + 4 * WINDOW * A_KV_WIDTH * 2) + 16 * WINDOW * 2 * WINDOW * 4
    return pl.pallas_call(
        _swa_kernel,
        out_shape=jax.ShapeDtypeStruct((n, A_Q_WIDTH), BF16),
        grid=(batch, nb),
        in_specs=[pl.BlockSpec(memory_space=pltpu.SMEM),
                  pl.BlockSpec((WINDOW, A_Q_WIDTH), cur),
                  pl.BlockSpec((WINDOW, A_KV_WIDTH), prev),
                  pl.BlockSpec((WINDOW, A_KV_WIDTH), cur),
                  pl.BlockSpec((WINDOW, A_KV_WIDTH), prev),
                  pl.BlockSpec((WINDOW, A_KV_WIDTH), cur)],
        out_specs=pl.BlockSpec((WINDOW, A_Q_WIDTH), cur),
        compiler_params=_params(("arbitrary", "arbitrary"), est),
        name="swa_attention",
    )(sinks, q_a, k_a, k_a, v_a, v_a)


def _split3(x):
    hi = x.astype(BF16)
    r1 = x - hi.astype(F32)
    mid = r1.astype(BF16)
    lo = (r1 - mid.astype(F32)).astype(BF16)
    return hi, mid, lo


def _gla_kernel(q_ref, k_ref, v_ref, lf_ref, g_ref, nw_ref, o_ref, s_scr, *, dk, dv):
    @pl.when(pl.program_id(2) == 0)
    def _():
        s_scr[...] = jnp.zeros_like(s_scr)

    tb = q_ref.shape[0]
    row = lax.broadcasted_iota(jnp.int32, (CHUNK, CHUNK), 0)
    col = lax.broadcasted_iota(jnp.int32, (CHUNK, CHUNK), 1)
    causal = row >= col
    tril = jnp.where(causal, 1.0, 0.0).astype(BF16)
    nw = nw_ref[...]
    state = s_scr[...]
    for ci in range(tb // CHUNK):
        rows = slice(ci * CHUNK, (ci + 1) * CHUNK)
        hi, mid, lo = _split3(lf_ref[rows, :])
        bb = jnp.dot(tril, jnp.concatenate([hi, mid, lo], axis=1), preferred_element_type=F32)
        b = bb[:, :dk] + bb[:, dk:2 * dk] + bb[:, 2 * dk:]
        b_last = b[CHUNK - 1:CHUNK, :]
        b_mid = b[CHUNK // 2 - 1:CHUNK // 2, :]
        q = q_ref[rows, :]
        k = k_ref[rows, :]
        v = v_ref[rows, :]
        q_e = (q * jnp.exp(b)).astype(BF16)
        q_i = (q * jnp.exp(b - b_mid)).astype(BF16)
        k_i = (k * jnp.exp(b_mid - b)).astype(BF16)
        k_s = (k * jnp.exp(b_last - b)).astype(BF16)
        a = lax.dot_general(q_i, k_i, (((1,), (1,)), ((), ())), preferred_element_type=F32)
        a = jnp.where(causal, a, 0.0).astype(BF16)
        o = (jnp.dot(a, v, preferred_element_type=F32)
             + jnp.dot(q_e, state.astype(BF16), preferred_element_type=F32))
        kv = lax.dot_general(k_s, v, (((0,), (0,)), ((), ())), preferred_element_type=F32)
        decay = jnp.exp(jnp.transpose(jnp.broadcast_to(b_last, (V7X_LANES, dk))))
        state = state * jnp.tile(decay, (1, dv // V7X_LANES)) + kv
        o = o * lax.rsqrt(jnp.mean(o * o, axis=-1, keepdims=True) + RMS_EPS)
        o_ref[rows, :] = ((o * nw) * _silu(g_ref[rows, :])).astype(BF16)
    s_scr[...] = state


def _gla(q, k, v, log_f, g, norm_w, *, batch, seq, heads, dk, dv, name):
    n = q.shape[0]
    tb = 512
    nt = seq // tb
    idx = lambda b, h, t: (b * nt + t, h)
    est = 2 * tb * (3 * dk * 4 + dv * 2 + dv * 4 + dv * 2) + dk * dv * 4 * 4 + 32 * CHUNK * max(3 * dk, dv) * 4
    return pl.pallas_call(
        functools.partial(_gla_kernel, dk=dk, dv=dv),
        out_shape=jax.ShapeDtypeStruct((n, heads * dv), BF16),
        grid=(batch, heads, nt),
        in_specs=[pl.BlockSpec((tb, dk), idx), pl.BlockSpec((tb, dk), idx), pl.BlockSpec((tb, dv), idx),
                  pl.BlockSpec((tb, dk), idx), pl.BlockSpec((tb, dv), idx),
                  pl.BlockSpec((1, dv), lambda b, h, t: (0, 0))],
        out_specs=pl.BlockSpec((tb, dv), idx),
        scratch_shapes=[pltpu.VMEM((dk, dv), F32)],
        compiler_params=_params(("arbitrary", "arbitrary", "arbitrary"), est),
        name=name,
    )(q, k, v, log_f, g, norm_w.reshape(1, dv))


def _outproj_ln_kernel(*refs, k_sizes):
    lhs_refs = refs[:len(k_sizes)]
    w_ref, x_ref, gate_ref, g_ref, b_ref, o_ref = refs[len(k_sizes):]
    y = None
    off = 0
    for r, ks in zip(lhs_refs, k_sizes):
        part = jnp.dot(r[...], w_ref[off:off + ks, :], preferred_element_type=F32)
        y = part if y is None else y + part
        off += ks
    o_ref[...] = _layer_norm(ALPHA * x_ref[...] + gate_ref[...] * y, g_ref[...], b_ref[...])


def _outproj_ln(lhs_list, w_bf, x2, mod3, gate_slot, ln_g, ln_b, *, seq):
    n, d = x2.shape
    tm = 512
    per_seq = seq // tm
    row = lambda i: (i, 0)
    k_sizes = tuple(a.shape[1] for a in lhs_list)
    k_total = sum(k_sizes)
    vec = pl.BlockSpec((1, d), lambda i: (0, 0))
    est = k_total * d * 2 + 2 * tm * k_total * 2 + 4 * tm * d * 4 + 4 * tm * d * 4
    return pl.pallas_call(
        functools.partial(_outproj_ln_kernel, k_sizes=k_sizes),
        out_shape=jax.ShapeDtypeStruct((n, d), F32),
        grid=(n // tm,),
        in_specs=[pl.BlockSpec((tm, ks), row) for ks in k_sizes] + [
            _resident((k_total, d)),
            pl.BlockSpec((tm, d), row),
            pl.BlockSpec((None, 1, d), lambda i: ((i // per_seq) * 6 + gate_slot, 0, 0)),
            vec, vec],
        out_specs=pl.BlockSpec((tm, d), row),
        compiler_params=_params(("arbitrary",), est),
        name="outproj_ln",
    )(*lhs_list, w_bf, x2, mod3, ln_g.reshape(1, d), ln_b.reshape(1, d))


FFN_HALO = BF16_SUBLANE_PACK


def _ffn_kernel(x_ref, halo_ref, shift_ref, scale_ref, gate_ref, wu_ref, wv_ref, cw_ref, cb_ref, wd_ref,
                lng_ref, lnb_ref, o_ref, h_scr, u_scr, *, per_seq):
    i = pl.program_id(0)
    f = pl.program_id(1)
    tm = x_ref.shape[0]

    @pl.when(f == 0)
    def _():
        sc = 1.0 + scale_ref[...]
        sh = shift_ref[...]
        h_scr[0:FFN_HALO, :] = (halo_ref[...] * sc + sh).astype(BF16)
        h_scr[FFN_HALO:, :] = (x_ref[...] * sc + sh).astype(BF16)

    u_full = jnp.dot(h_scr[...], wu_ref[...], preferred_element_type=F32)
    keep = jnp.where(i % per_seq == 0, 0.0, 1.0)
    u_scr[0:FFN_HALO, :] = u_full[0:FFN_HALO, :] * keep
    u_scr[FFN_HALO:, :] = u_full[FFN_HALO:, :]
    cw = cw_ref[...]
    u = (cw[2:3, :] * u_scr[FFN_HALO:FFN_HALO + tm, :]
         + cw[1:2, :] * u_scr[FFN_HALO - 1:FFN_HALO - 1 + tm, :]
         + cw[0:1, :] * u_scr[FFN_HALO - 2:FFN_HALO - 2 + tm, :]) + cb_ref[...]
    v = jnp.dot(h_scr[FFN_HALO:, :], wv_ref[...], preferred_element_type=F32)
    act = (_silu(u) * v).astype(BF16)
    part = jnp.dot(act, wd_ref[...], preferred_element_type=F32)

    @pl.when(f == 0)
    def _():
        o_ref[...] = part

    @pl.when(f > 0)
    def _():
        o_ref[...] += part

    @pl.when(f == pl.num_programs(1) - 1)
    def _():
        o_ref[...] = _layer_norm(ALPHA * x_ref[...] + gate_ref[...] * o_ref[...], lng_ref[...], lnb_ref[...])


def _ffn(x2, mod3, w_up_bf, conv_w, conv_b, w_down_bf, ln_g, ln_b, *, seq):
    n, d = x2.shape
    tm = 512
    tf = 512
    per_seq = seq // tm
    nf = D_FF // tf
    halo_blocks = tm // FFN_HALO
    slot = lambda s: (lambda i, f: ((i // per_seq) * 6 + s, 0, 0))
    vec = pl.BlockSpec((1, d), lambda i, f: (0, 0))
    est = (4 * tm * d * 4 + 2 * FFN_HALO * d * 4 + 2 * 2 * d * tf * 2 + 2 * tf * d * 2
           + (tm + FFN_HALO) * d * 2 + (tm + FFN_HALO) * tf * 4 + 8 * tm * tf * 4 + 2 * tm * d * 4)
    return pl.pallas_call(
        functools.partial(_ffn_kernel, per_seq=per_seq),
        out_shape=jax.ShapeDtypeStruct((n, d), F32),
        grid=(n // tm, nf),
        in_specs=[pl.BlockSpec((tm, d), lambda i, f: (i, 0)),
                  pl.BlockSpec((FFN_HALO, d), lambda i, f: (jnp.maximum(i * halo_blocks - 1, 0), 0)),
                  pl.BlockSpec((None, 1, d), slot(3)),
                  pl.BlockSpec((None, 1, d), slot(4)),
                  pl.BlockSpec((None, 1, d), slot(5)),
                  pl.BlockSpec((d, tf), lambda i, f: (0, f)),
                  pl.BlockSpec((d, tf), lambda i, f: (0, nf + f)),
                  pl.BlockSpec((CONV_WIDTH, tf), lambda i, f: (0, f)),
                  pl.BlockSpec((1, tf), lambda i, f: (0, f)),
                  pl.BlockSpec((tf, d), lambda i, f: (f, 0)),
                  vec, vec],
        out_specs=pl.BlockSpec((tm, d), lambda i, f: (i, 0)),
        scratch_shapes=[pltpu.VMEM((tm + FFN_HALO, d), BF16), pltpu.VMEM((tm + FFN_HALO, tf), F32)],
        compiler_params=_params(("arbitrary", "arbitrary"), est),
        name="ffn",
    )(x2, x2, mod3, mod3, mod3, w_up_bf, w_up_bf, conv_w, conv_b.reshape(1, D_FF), w_down_bf,
      ln_g.reshape(1, d), ln_b.reshape(1, d))


def _inproj_odd_kernel(x_ref, shift_ref, scale_ref, w_ref, wgb_ref, bg_ref,
                       q_ref, k_ref, v_ref, g_ref, lf_ref, h_scr):
    h_scr[...] = (x_ref[...] * (1.0 + scale_ref[...]) + shift_ref[...]).astype(BF16)

    def proj(c0, width=PROJ_COLS):
        return jnp.dot(h_scr[...], w_ref[:, c0:c0 + width], preferred_element_type=F32)

    for c0 in range(0, C_KEY_DIM, PROJ_COLS):
        sl = slice(c0, c0 + PROJ_COLS)
        q_ref[:, sl] = proj(c0) * (C_DK ** -0.5)
        k_ref[:, sl] = proj(C_KEY_DIM + c0)
    for c0 in range(0, C_VALUE_DIM, PROJ_COLS):
        sl = slice(c0, c0 + PROJ_COLS)
        v_ref[:, sl] = proj(2 * C_KEY_DIM + c0).astype(BF16)
        g_ref[:, sl] = proj(2 * C_KEY_DIM + C_VALUE_DIM + c0)
    low = proj(2 * C_KEY_DIM + 2 * C_VALUE_DIM, RANK_PAD).astype(BF16)
    for c0 in range(0, C_KEY_DIM, PROJ_COLS):
        sl = slice(c0, c0 + PROJ_COLS)
        gk = jnp.dot(low, wgb_ref[:, sl], preferred_element_type=F32) + bg_ref[:, sl]
        lf_ref[:, sl] = jax.nn.log_sigmoid(gk) * (1.0 / GATE_NORMALIZER)


def _inproj_odd(x2, mod3, w_bf, w_gk_b_bf, b_gk, *, seq):
    n, d = x2.shape
    tm = 256
    per_seq = seq // tm
    width = w_bf.shape[1]
    row = lambda i: (i, 0)
    out_shapes = (
        jax.ShapeDtypeStruct((n, C_KEY_DIM), F32), jax.ShapeDtypeStruct((n, C_KEY_DIM), F32),
        jax.ShapeDtypeStruct((n, C_VALUE_DIM), BF16), jax.ShapeDtypeStruct((n, C_VALUE_DIM), F32),
        jax.ShapeDtypeStruct((n, C_KEY_DIM), F32))
    out_bytes = sum(tm * s.shape[1] * s.dtype.itemsize for s in out_shapes)
    est = d * width * 2 + 2 * tm * d * 4 + 2 * out_bytes + tm * d * 2 + 6 * tm * PROJ_COLS * 4
    return pl.pallas_call(
        _inproj_odd_kernel,
        out_shape=out_shapes,
        grid=(n // tm,),
        in_specs=[pl.BlockSpec((tm, d), row),
                  pl.BlockSpec((None, 1, d), lambda i: ((i // per_seq) * 6 + 0, 0, 0)),
                  pl.BlockSpec((None, 1, d), lambda i: ((i // per_seq) * 6 + 1, 0, 0)),
                  _resident((d, width)),
                  _resident((RANK_PAD, C_KEY_DIM)),
                  pl.BlockSpec((1, C_KEY_DIM), lambda i: (0, 0))],
        out_specs=tuple(pl.BlockSpec((tm, s.shape[1]), row) for s in out_shapes),
        scratch_shapes=[pltpu.VMEM((tm, d), BF16)],
        compiler_params=_params(("arbitrary",), est),
        name="inproj_odd",
    )(x2, mod3, mod3, w_bf, w_gk_b_bf, b_gk.reshape(1, C_KEY_DIM))


def kernel(x, c, positions,
           ada_w0, ada_b0, mix_w_in0, mix_w_out0, attn_sinks0, hgrn_lb_logits, hgrn_norm_w0,
           ln_mix_g0, ln_mix_b0, ffn_w_up0, ffn_conv_w0, ffn_conv_b0, ffn_w_down0, ln_ffn_g0, ln_ffn_b0,
           ada_w1, ada_b1, mix_w_in1, gla_w_gk_a1, gla_w_gk_b1, gla_b_gk1, gla_norm_w1, mix_w_out1,
           ln_mix_g1, ln_mix_b1, ffn_w_up1, ffn_conv_w1, ffn_conv_b1, ffn_w_down1, ln_ffn_g1, ln_ffn_b1):
    batch, seq, d = x.shape
    n = batch * seq
    x2 = x.reshape(n, d)
    c_pad = jnp.pad(c, ((0, 8 - batch), (0, 0)))

    def modulation(ada_w, ada_b):
        mod = _ada_modulation(c_pad, ada_w, ada_b)[:batch]
        return mod.reshape(batch * 6, 1, d)

    mod3 = modulation(ada_w0, ada_b0)
    cos, sin_signed = _rope_tables(positions.reshape(n, 1))
    q_a, k_a, v_a, q_b, k_b, lf_b, i_b, g_b = _inproj_even(
        x2, mod3, mix_w_in0.astype(BF16), cos, sin_signed, hgrn_lb_logits, seq=seq, layer=0)
    o_a = _swa_attention(q_a, k_a, v_a, attn_sinks0, batch=batch, seq=seq)
    o_b = _gla(q_b, k_b, i_b, lf_b, g_b, hgrn_norm_w0, batch=batch, seq=seq,
               heads=B_HEADS, dk=B_HEAD_DIM, dv=B_HEAD_DIM, name="hgrn2")
    x2 = _outproj_ln([o_a, o_b], mix_w_out0.astype(BF16), x2, mod3, 2, ln_mix_g0, ln_mix_b0, seq=seq)
    x2 = _ffn(x2, mod3, ffn_w_up0.astype(BF16), ffn_conv_w0, ffn_conv_b0, ffn_w_down0.astype(BF16),
              ln_ffn_g0, ln_ffn_b0, seq=seq)

    mod3 = modulation(ada_w1, ada_b1)
    w_in1 = jnp.concatenate(
        [mix_w_in1, jnp.pad(gla_w_gk_a1, ((0, 0), (0, RANK_PAD - GATE_RANK)))], axis=1).astype(BF16)
    w_gk_b = jnp.pad(gla_w_gk_b1, ((0, RANK_PAD - GATE_RANK), (0, 0))).astype(BF16)
    q_c, k_c, v_c, g_c, lf_c = _inproj_odd(x2, mod3, w_in1, w_gk_b, gla_b_gk1, seq=seq)
    o_c = _gla(q_c, k_c, v_c, lf_c, g_c, gla_norm_w1, batch=batch, seq=seq,
               heads=C_HEADS, dk=C_DK, dv=C_DV, name="gla")
    x2 = _outproj_ln([o_c], mix_w_out1.astype(BF16), x2, mod3, 2, ln_mix_g1, ln_mix_b1, seq=seq)
    x2 = _ffn(x2, mod3, ffn_w_up1.astype(BF16), ffn_conv_w1, ffn_conv_b1, ffn_w_down1.astype(BF16),
              ln_ffn_g1, ln_ffn_b1, seq=seq)
    return x2.reshape(batch, seq, d)
```

```python
import functools
import math

import jax
import jax.numpy as jnp
from jax import lax
from jax.experimental import pallas as pl
from jax.experimental.pallas import tpu as pltpu

F32 = jnp.float32
BF16 = jnp.bfloat16

D_MODEL = 2048
DEPTH = 2
HEAD_DIM = 64
A_Q_HEADS = 16
A_KV_HEADS = 4
A_Q_WIDTH = A_Q_HEADS * HEAD_DIM
A_KV_WIDTH = A_KV_HEADS * HEAD_DIM
WINDOW = 128
ROPE_THETA = 10000.0
B_HEADS = 8
B_HEAD_DIM = 128
B_WIDTH = B_HEADS * B_HEAD_DIM
EVEN_IN_WIDTH = A_Q_WIDTH + 2 * A_KV_WIDTH + 4 * B_WIDTH
C_HEADS = 4
C_KEY_DIM = D_MODEL // 2
C_VALUE_DIM = D_MODEL
C_DK = C_KEY_DIM // C_HEADS
C_DV = C_VALUE_DIM // C_HEADS
GATE_RANK = 16
GATE_NORMALIZER = 16.0
CHUNK = 64
D_FF = 5632
CONV_WIDTH = 3
LN_EPS = 1e-5
RMS_EPS = 1e-6
ALPHA = (2.0 * DEPTH) ** 0.25
NEG_INF = -1e30

V7X_LANES = 128
V7X_VMEM_BYTES = 64 * 1024 * 1024
V7X_VMEM_CEILING = 56 * 1024 * 1024
BF16_SUBLANE_PACK = 16

RANK_PAD = V7X_LANES


def _params(semantics, vmem_estimate):
    limit = min(V7X_VMEM_CEILING, max(16 * 1024 * 1024, int(vmem_estimate * 1.25)))
    return pltpu.CompilerParams(dimension_semantics=semantics, vmem_limit_bytes=limit)


def _resident(shape):
    return pl.BlockSpec(shape, lambda *_: (0,) * len(shape), pipeline_mode=pl.Buffered(1))


def _silu(x):
    return x * jax.nn.sigmoid(x)


def _layer_norm(z, g, b):
    mu = jnp.mean(z, axis=-1, keepdims=True)
    zc = z - mu
    var = jnp.mean(zc * zc, axis=-1, keepdims=True)
    return zc * lax.rsqrt(var + LN_EPS) * g + b


def _ada_kernel(c_ref, w_ref, b_ref, o_ref):
    a = _silu(c_ref[...]).astype(BF16)
    o_ref[...] = jnp.dot(a, w_ref[...].astype(BF16), preferred_element_type=F32) + b_ref[...]


def _ada_modulation(c_pad, ada_w, ada_b):
    rows, d = c_pad.shape
    n = ada_w.shape[1]
    tn = 1024
    est = 2 * d * tn * 4 + d * tn * 2 + 4 * rows * n
    return pl.pallas_call(
        _ada_kernel,
        out_shape=jax.ShapeDtypeStruct((rows, n), F32),
        grid=(n // tn,),
        in_specs=[pl.BlockSpec((rows, d), lambda j: (0, 0)),
                  pl.BlockSpec((d, tn), lambda j: (0, j)),
                  pl.BlockSpec((1, tn), lambda j: (0, j))],
        out_specs=pl.BlockSpec((rows, tn), lambda j: (0, j)),
        compiler_params=_params(("arbitrary",), est),
        name="ada_modulation",
    )(c_pad, ada_w, ada_b.reshape(1, n))


def _rope_table_kernel(pos_ref, invf_ref, sign_ref, cos_ref, sin_ref):
    ang = pos_ref[...].astype(F32) * invf_ref[...]
    cos_ref[...] = jnp.cos(ang)
    sin_ref[...] = jnp.sin(ang) * sign_ref[...]


def _rope_tables(pos_col):
    n = pos_col.shape[0]
    half = HEAD_DIM // 2
    lane = jnp.arange(V7X_LANES)
    inv_freq = ROPE_THETA ** (-jnp.arange(half, dtype=F32) / half)
    invf = inv_freq[lane % half].reshape(1, V7X_LANES)
    sign = jnp.where((lane % HEAD_DIM) < half, -1.0, 1.0).astype(F32).reshape(1, V7X_LANES)
    tm = 2048
    est = 2 * (tm * V7X_LANES * 4) * 3
    return pl.pallas_call(
        _rope_table_kernel,
        out_shape=(jax.ShapeDtypeStruct((n, V7X_LANES), F32),) * 2,
        grid=(n // tm,),
        in_specs=[pl.BlockSpec((tm, 1), lambda i: (i, 0)),
                  pl.BlockSpec((1, V7X_LANES), lambda i: (0, 0)),
                  pl.BlockSpec((1, V7X_LANES), lambda i: (0, 0))],
        out_specs=(pl.BlockSpec((tm, V7X_LANES), lambda i: (i, 0)),) * 2,
        compiler_params=_params(("arbitrary",), est),
        name="rope_tables",
    )(pos_col, invf, sign)


def _rope(a, cos, sin_signed, first_half):
    outs = []
    for s in range(a.shape[1] // V7X_LANES):
        blk = a[:, s * V7X_LANES:(s + 1) * V7X_LANES]
        partner = jnp.where(first_half,
                            pltpu.roll(blk, V7X_LANES - HEAD_DIM // 2, 1),
                            pltpu.roll(blk, HEAD_DIM // 2, 1))
        outs.append(blk * cos + partner * sin_signed)
    return jnp.concatenate(outs, axis=1) if len(outs) > 1 else outs[0]


PROJ_COLS = 512


def _inproj_even_kernel(x_ref, shift_ref, scale_ref, w_ref, cos_ref, sin_ref, lbl_ref,
                        qa_ref, ka_ref, va_ref, qb_ref, kb_ref, lf_ref, ib_ref, gb_ref, h_scr, *, layer):
    tm = x_ref.shape[0]
    h_scr[...] = (x_ref[...] * (1.0 + scale_ref[...]) + shift_ref[...]).astype(BF16)

    def proj(c0, width=PROJ_COLS):
        return jnp.dot(h_scr[...], w_ref[:, c0:c0 + width], preferred_element_type=F32)

    cos = cos_ref[...]
    sin_signed = sin_ref[...]
    lane = lax.broadcasted_iota(jnp.int32, (tm, V7X_LANES), 1)
    first_half = (lane % HEAD_DIM) < (HEAD_DIM // 2)

    for c0 in range(0, A_Q_WIDTH, PROJ_COLS):
        qa_ref[:, c0:c0 + PROJ_COLS] = (_rope(proj(c0), cos, sin_signed, first_half)
                                        * (HEAD_DIM ** -0.5)).astype(BF16)
    kv = proj(A_Q_WIDTH, 2 * A_KV_WIDTH)
    ka_ref[...] = _rope(kv[:, :A_KV_WIDTH], cos, sin_signed, first_half).astype(BF16)
    va_ref[...] = kv[:, A_KV_WIDTH:].astype(BF16)

    base = A_Q_WIDTH + 2 * A_KV_WIDTH
    lg = lbl_ref[...]
    e = jnp.exp(lg - jnp.max(lg, axis=0, keepdims=True))
    sm = e / jnp.sum(e, axis=0, keepdims=True)
    lb = jnp.sum(sm[0:layer + 1], axis=0, keepdims=True)
    for c0 in range(0, B_WIDTH, PROJ_COLS):
        sl = slice(c0, c0 + PROJ_COLS)
        qb_ref[:, sl] = proj(base + c0)
        lbc = lb[:, sl]
        fg = lbc + (1.0 - lbc) * jax.nn.sigmoid(proj(base + B_WIDTH + c0))
        kb_ref[:, sl] = 1.0 - fg
        lf_ref[:, sl] = jnp.log(fg)
        ib_ref[:, sl] = proj(base + 2 * B_WIDTH + c0).astype(BF16)
        gb_ref[:, sl] = proj(base + 3 * B_WIDTH + c0)


def _inproj_even(x2, mod3, w_bf, cos, sin_signed, lb_logits, *, seq, layer):
    n, d = x2.shape
    tm = 256
    per_seq = seq // tm
    width = w_bf.shape[1]
    row = lambda i: (i, 0)
    out_shapes = (
        jax.ShapeDtypeStruct((n, A_Q_WIDTH), BF16), jax.ShapeDtypeStruct((n, A_KV_WIDTH), BF16),
        jax.ShapeDtypeStruct((n, A_KV_WIDTH), BF16), jax.ShapeDtypeStruct((n, B_WIDTH), F32),
        jax.ShapeDtypeStruct((n, B_WIDTH), F32), jax.ShapeDtypeStruct((n, B_WIDTH), F32),
        jax.ShapeDtypeStruct((n, B_WIDTH), BF16), jax.ShapeDtypeStruct((n, B_WIDTH), F32))
    out_bytes = sum(tm * s.shape[1] * s.dtype.itemsize for s in out_shapes)
    est = d * width * 2 + 2 * tm * d * 4 + 2 * out_bytes + tm * d * 2 + 6 * tm * PROJ_COLS * 4
    return pl.pallas_call(
        functools.partial(_inproj_even_kernel, layer=layer),
        out_shape=out_shapes,
        grid=(n // tm,),
        in_specs=[pl.BlockSpec((tm, d), row),
                  pl.BlockSpec((None, 1, d), lambda i: ((i // per_seq) * 6 + 0, 0, 0)),
                  pl.BlockSpec((None, 1, d), lambda i: ((i // per_seq) * 6 + 1, 0, 0)),
                  _resident((d, width)),
                  pl.BlockSpec((tm, V7X_LANES), row),
                  pl.BlockSpec((tm, V7X_LANES), row),
                  pl.BlockSpec(lb_logits.shape, lambda i: (0, 0))],
        out_specs=tuple(pl.BlockSpec((tm, s.shape[1]), row) for s in out_shapes),
        scratch_shapes=[pltpu.VMEM((tm, d), BF16)],
        compiler_params=_params(("arbitrary",), est),
        name="inproj_even",
    )(x2, mod3, mod3, w_bf, cos, sin_signed, lb_logits)


def _swa_kernel(sink_ref, q_ref, kp_ref, kc_ref, vp_ref, vc_ref, o_ref):
    n = pl.program_id(1)
    grp = A_Q_HEADS // A_KV_HEADS
    r = lax.broadcasted_iota(jnp.int32, (WINDOW, 2 * WINDOW), 0)
    c = lax.broadcasted_iota(jnp.int32, (WINDOW, 2 * WINDOW), 1)
    rel = r + WINDOW - c
    valid = (rel >= 0) & (rel < WINDOW) & ((c >= WINDOW) | (n > 0))
    q = q_ref[...]
    kcat = jnp.concatenate([kp_ref[...], kc_ref[...]], axis=0)
    vcat = jnp.concatenate([vp_ref[...], vc_ref[...]], axis=0)
    outs = []
    for j in range(A_Q_HEADS):
        g = j // grp
        qh = q[:, j * HEAD_DIM:(j + 1) * HEAD_DIM]
        kh = kcat[:, g * HEAD_DIM:(g + 1) * HEAD_DIM]
        vh = vcat[:, g * HEAD_DIM:(g + 1) * HEAD_DIM]
        s = lax.dot_general(qh, kh, (((1,), (1,)), ((), ())), preferred_element_type=F32)
        s = jnp.where(valid, s, NEG_INF)
        sink = sink_ref[j]
        m = jnp.maximum(jnp.max(s, axis=-1, keepdims=True), sink)
        p = jnp.exp(s - m)
        denom = jnp.sum(p, axis=-1, keepdims=True) + jnp.exp(sink - m)
        p = (p / denom).astype(BF16)
        outs.append(jnp.dot(p, vh, preferred_element_type=F32))
    o_ref[...] = jnp.concatenate(outs, axis=1).astype(BF16)


def _swa_attention(q_a, k_a, v_a, sinks, *, batch, seq):
    n = q_a.shape[0]
    nb = seq // WINDOW
    cur = lambda b, i: (b * nb + i, 0)
    prev = lambda b, i: (b * nb + jnp.maximum(i - 1, 0), 0)
    est = 2 * (2 * WINDOW * A_Q_WIDTH * 2 + 4 * WINDOW * A_KV_WIDTH * 2) + 16 * WINDOW * 2 * WINDOW * 4
    return pl.pallas_call(
        _swa_kernel,
        out_shape=jax.ShapeDtypeStruct((n, A_Q_WIDTH), BF16),
        grid=(batch, nb),
        in_specs=[pl.BlockSpec(memory_space=pltpu.SMEM),
                  pl.BlockSpec((WINDOW, A_Q_WIDTH), cur),
                  pl.BlockSpec((WINDOW, A_KV_WIDTH), prev),
                  pl.BlockSpec((WINDOW, A_KV_WIDTH), cur),
                  pl.BlockSpec((WINDOW, A_KV_WIDTH), prev),
                  pl.BlockSpec((WINDOW, A_KV_WIDTH), cur)],
        out_specs=pl.BlockSpec((WINDOW, A_Q_WIDTH), cur),
        compiler_params=_params(("arbitrary", "arbitrary"), est),
        name="swa_attention",
    )(sinks, q_a, k_a, k_a, v_a, v_a)


def _split3(x):
    hi = x.astype(BF16)
    r1 = x - hi.astype(F32)
    mid = r1.astype(BF16)
    lo = (r1 - mid.astype(F32)).astype(BF16)
    return hi, mid, lo


GLA_GROUP = 256


def _gla_kernel(q_ref, k_ref, v_ref, lf_ref, g_ref, nw_ref, o_ref, s_scr, *, dk, dv, heads_per_step):
    @pl.when(pl.program_id(2) == 0)
    def _():
        s_scr[...] = jnp.zeros_like(s_scr)

    tb = q_ref.shape[0]
    per_group = GLA_GROUP // CHUNK
    row = lax.broadcasted_iota(jnp.int32, (GLA_GROUP, GLA_GROUP), 0)
    col = lax.broadcasted_iota(jnp.int32, (GLA_GROUP, GLA_GROUP), 1)
    causal = (row // CHUNK == col // CHUNK) & (row >= col)
    tril = jnp.where(causal, 1.0, 0.0).astype(BF16)
    nw = nw_ref[...]

    def chunk_rows_to_group(b, r):
        return jnp.concatenate(
            [jnp.broadcast_to(b[c * CHUNK + r:c * CHUNK + r + 1, :], (CHUNK, dk)) for c in range(per_group)], axis=0)

    for hh in range(heads_per_step):
        kcols = slice(hh * dk, (hh + 1) * dk)
        vcols = slice(hh * dv, (hh + 1) * dv)
        state = s_scr[hh]
        for gi in range(tb // GLA_GROUP):
            rows = slice(gi * GLA_GROUP, (gi + 1) * GLA_GROUP)
            hi, mid, lo = _split3(lf_ref[rows, kcols])
            bb = jnp.dot(tril, jnp.concatenate([hi, mid, lo], axis=1), preferred_element_type=F32)
            b = bb[:, :dk] + bb[:, dk:2 * dk] + bb[:, 2 * dk:]
            b_last = chunk_rows_to_group(b, CHUNK - 1)
            b_mid = chunk_rows_to_group(b, CHUNK // 2 - 1)
            q = q_ref[rows, kcols]
            k = k_ref[rows, kcols]
            v = v_ref[rows, vcols]
            q_e = (q * jnp.exp(b)).astype(BF16)
            q_i = (q * jnp.exp(b - b_mid)).astype(BF16)
            k_i = (k * jnp.exp(b_mid - b)).astype(BF16)
            k_s = (k * jnp.exp(b_last - b)).astype(BF16)
            a = lax.dot_general(q_i, k_i, (((1,), (1,)), ((), ())), preferred_element_type=F32)
            a = jnp.where(causal, a, 0.0).astype(BF16)
            o_intra = jnp.dot(a, v, preferred_element_type=F32)
            outs = []
            for c in range(per_group):
                cr = slice(c * CHUNK, (c + 1) * CHUNK)
                outs.append(o_intra[cr] + jnp.dot(q_e[cr], state.astype(BF16), preferred_element_type=F32))
                kv = lax.dot_general(k_s[cr], v[cr], (((0,), (0,)), ((), ())), preferred_element_type=F32)
                decay = jnp.exp(jnp.transpose(
                    jnp.broadcast_to(b[(c + 1) * CHUNK - 1:(c + 1) * CHUNK, :], (V7X_LANES, dk))))
                state = state * jnp.tile(decay, (1, dv // V7X_LANES)) + kv
            o = jnp.concatenate(outs, axis=0)
            o = o * lax.rsqrt(jnp.mean(o * o, axis=-1, keepdims=True) + RMS_EPS)
            o_ref[rows, vcols] = ((o * nw) * _silu(g_ref[rows, vcols])).astype(BF16)
        s_scr[hh] = state


def _gla(q, k, v, log_f, g, norm_w, *, batch, seq, heads, dk, dv, heads_per_step, name):
    n = q.shape[0]
    tb = 512
    nt = seq // tb
    hp = heads_per_step
    idx = lambda b, h, t: (b * nt + t, h)
    est = (2 * tb * hp * (3 * dk * 4 + dv * 2 + dv * 4 + dv * 2) + hp * dk * dv * 4 + 4 * dk * dv * 4
           + 16 * GLA_GROUP * max(3 * dk, dv) * 4)
    return pl.pallas_call(
        functools.partial(_gla_kernel, dk=dk, dv=dv, heads_per_step=hp),
        out_shape=jax.ShapeDtypeStruct((n, heads * dv), BF16),
        grid=(batch, heads // hp, nt),
        in_specs=[pl.BlockSpec((tb, hp * dk), idx), pl.BlockSpec((tb, hp * dk), idx),
                  pl.BlockSpec((tb, hp * dv), idx), pl.BlockSpec((tb, hp * dk), idx),
                  pl.BlockSpec((tb, hp * dv), idx),
                  pl.BlockSpec((1, dv), lambda b, h, t: (0, 0))],
        out_specs=pl.BlockSpec((tb, hp * dv), idx),
        scratch_shapes=[pltpu.VMEM((hp, dk, dv), F32)],
        compiler_params=_params(("arbitrary", "arbitrary", "arbitrary"), est),
        name=name,
    )(q, k, v, log_f, g, norm_w.reshape(1, dv))


def _outproj_ln_kernel(*refs, k_sizes):
    lhs_refs = refs[:len(k_sizes)]
    w_ref, x_ref, gate_ref, g_ref, b_ref, o_ref = refs[len(k_sizes):]
    y = None
    off = 0
    for r, ks in zip(lhs_refs, k_sizes):
        part = jnp.dot(r[...], w_ref[off:off + ks, :], preferred_element_type=F32)
        y = part if y is None else y + part
        off += ks
    o_ref[...] = _layer_norm(ALPHA * x_ref[...] + gate_ref[...] * y, g_ref[...], b_ref[...])


def _outproj_ln(lhs_list, w_bf, x2, mod3, gate_slot, ln_g, ln_b, *, seq):
    n, d = x2.shape
    tm = 512
    per_seq = seq // tm
    row = lambda i: (i, 0)
    k_sizes = tuple(a.shape[1] for a in lhs_list)
    k_total = sum(k_sizes)
    vec = pl.BlockSpec((1, d), lambda i: (0, 0))
    est = k_total * d * 2 + 2 * tm * k_total * 2 + 4 * tm * d * 4 + 4 * tm * d * 4
    return pl.pallas_call(
        functools.partial(_outproj_ln_kernel, k_sizes=k_sizes),
        out_shape=jax.ShapeDtypeStruct((n, d), F32),
        grid=(n // tm,),
        in_specs=[pl.BlockSpec((tm, ks), row) for ks in k_sizes] + [
            _resident((k_total, d)),
            pl.BlockSpec((tm, d), row),
            pl.BlockSpec((None, 1, d), lambda i: ((i // per_seq) * 6 + gate_slot, 0, 0)),
            vec, vec],
        out_specs=pl.BlockSpec((tm, d), row),
        compiler_params=_params(("arbitrary",), est),
        name="outproj_ln",
    )(*lhs_list, w_bf, x2, mod3, ln_g.reshape(1, d), ln_b.reshape(1, d))


FFN_HALO = BF16_SUBLANE_PACK


def _ffn_kernel(x_ref, halo_ref, shift_ref, scale_ref, gate_ref, wu_ref, wv_ref, cw_ref, cb_ref, wd_ref,
                lng_ref, lnb_ref, o_ref, h_scr, u_scr, *, per_seq):
    i = pl.program_id(0)
    f = pl.program_id(1)
    tm = x_ref.shape[0]

    @pl.when(f == 0)
    def _():
        sc = 1.0 + scale_ref[...]
        sh = shift_ref[...]
        h_scr[0:FFN_HALO, :] = (halo_ref[...] * sc + sh).astype(BF16)
        h_scr[FFN_HALO:, :] = (x_ref[...] * sc + sh).astype(BF16)
        o_ref[...] = jnp.zeros_like(o_ref)

    u_full = jnp.dot(h_scr[...], wu_ref[...], preferred_element_type=F32)
    keep = jnp.where(i % per_seq == 0, 0.0, 1.0)
    u_scr[0:FFN_HALO, :] = u_full[0:FFN_HALO, :] * keep
    u_scr[FFN_HALO:, :] = u_full[FFN_HALO:, :]
    cw = cw_ref[...]
    u = (cw[2:3, :] * u_scr[FFN_HALO:FFN_HALO + tm, :]
         + cw[1:2, :] * u_scr[FFN_HALO - 1:FFN_HALO - 1 + tm, :]
         + cw[0:1, :] * u_scr[FFN_HALO - 2:FFN_HALO - 2 + tm, :]) + cb_ref[...]
    v = jnp.dot(h_scr[FFN_HALO:, :], wv_ref[...], preferred_element_type=F32)
    act = (_silu(u) * v).astype(BF16)
    o_ref[...] += jnp.dot(act, wd_ref[...], preferred_element_type=F32)

    @pl.when(f == pl.num_programs(1) - 1)
    def _():
        o_ref[...] = _layer_norm(ALPHA * x_ref[...] + gate_ref[...] * o_ref[...], lng_ref[...], lnb_ref[...])


def _ffn(x2, mod3, w_up_bf, conv_w, conv_b, w_down_bf, ln_g, ln_b, *, seq):
    n, d = x2.shape
    tm = 512
    tf = 512
    per_seq = seq // tm
    nf = D_FF // tf
    halo_blocks = tm // FFN_HALO
    slot = lambda s: (lambda i, f: ((i // per_seq) * 6 + s, 0, 0))
    vec = pl.BlockSpec((1, d), lambda i, f: (0, 0))
    est = (4 * tm * d * 4 + 2 * FFN_HALO * d * 4 + 2 * 2 * d * tf * 2 + 2 * tf * d * 2
           + (tm + FFN_HALO) * d * 2 + (tm + FFN_HALO) * tf * 4 + 8 * tm * tf * 4 + 2 * tm * d * 4)
    return pl.pallas_call(
        functools.partial(_ffn_kernel, per_seq=per_seq),
        out_shape=jax.ShapeDtypeStruct((n, d), F32),
        grid=(n // tm, nf),
        in_specs=[pl.BlockSpec((tm, d), lambda i, f: (i, 0)),
                  pl.BlockSpec((FFN_HALO, d), lambda i, f: (jnp.maximum(i * halo_blocks - 1, 0), 0)),
                  pl.BlockSpec((None, 1, d), slot(3)),
                  pl.BlockSpec((None, 1, d), slot(4)),
                  pl.BlockSpec((None, 1, d), slot(5)),
                  pl.BlockSpec((d, tf), lambda i, f: (0, f)),
                  pl.BlockSpec((d, tf), lambda i, f: (0, nf + f)),
                  pl.BlockSpec((CONV_WIDTH, tf), lambda i, f: (0, f)),
                  pl.BlockSpec((1, tf), lambda i, f: (0, f)),
                  pl.BlockSpec((tf, d), lambda i, f: (f, 0)),
                  vec, vec],
        out_specs=pl.BlockSpec((tm, d), lambda i, f: (i, 0)),
        scratch_shapes=[pltpu.VMEM((tm + FFN_HALO, d), BF16), pltpu.VMEM((tm + FFN_HALO, tf), F32)],
        compiler_params=_params(("arbitrary", "arbitrary"), est),
        name="ffn",
    )(x2, x2, mod3, mod3, mod3, w_up_bf, w_up_bf, conv_w, conv_b.reshape(1, D_FF), w_down_bf,
      ln_g.reshape(1, d), ln_b.reshape(1, d))


def _inproj_odd_kernel(x_ref, shift_ref, scale_ref, w_ref, wgb_ref, bg_ref,
                       q_ref, k_ref, v_ref, g_ref, lf_ref, h_scr):
    h_scr[...] = (x_ref[...] * (1.0 + scale_ref[...]) + shift_ref[...]).astype(BF16)

    def proj(c0, width=PROJ_COLS):
        return jnp.dot(h_scr[...], w_ref[:, c0:c0 + width], preferred_element_type=F32)

    for c0 in range(0, C_KEY_DIM, PROJ_COLS):
        sl = slice(c0, c0 + PROJ_COLS)
        q_ref[:, sl] = proj(c0) * (C_DK ** -0.5)
        k_ref[:, sl] = proj(C_KEY_DIM + c0)
    for c0 in range(0, C_VALUE_DIM, PROJ_COLS):
        sl = slice(c0, c0 + PROJ_COLS)
        v_ref[:, sl] = proj(2 * C_KEY_DIM + c0).astype(BF16)
        g_ref[:, sl] = proj(2 * C_KEY_DIM + C_VALUE_DIM + c0)
    low = proj(2 * C_KEY_DIM + 2 * C_VALUE_DIM, RANK_PAD).astype(BF16)
    for c0 in range(0, C_KEY_DIM, PROJ_COLS):
        sl = slice(c0, c0 + PROJ_COLS)
        gk = jnp.dot(low, wgb_ref[:, sl], preferred_element_type=F32) + bg_ref[:, sl]
        lf_ref[:, sl] = jax.nn.log_sigmoid(gk) * (1.0 / GATE_NORMALIZER)


def _inproj_odd(x2, mod3, w_bf, w_gk_b_bf, b_gk, *, seq):
    n, d = x2.shape
    tm = 256
    per_seq = seq // tm
    width = w_bf.shape[1]
    row = lambda i: (i, 0)
    out_shapes = (
        jax.ShapeDtypeStruct((n, C_KEY_DIM), F32), jax.ShapeDtypeStruct((n, C_KEY_DIM), F32),
        jax.ShapeDtypeStruct((n, C_VALUE_DIM), BF16), jax.ShapeDtypeStruct((n, C_VALUE_DIM), F32),
        jax.ShapeDtypeStruct((n, C_KEY_DIM), F32))
    out_bytes = sum(tm * s.shape[1] * s.dtype.itemsize for s in out_shapes)
    est = d * width * 2 + 2 * tm * d * 4 + 2 * out_bytes + tm * d * 2 + 6 * tm * PROJ_COLS * 4
    return pl.pallas_call(
        _inproj_odd_kernel,
        out_shape=out_shapes,
        grid=(n // tm,),
        in_specs=[pl.BlockSpec((tm, d), row),
                  pl.BlockSpec((None, 1, d), lambda i: ((i // per_seq) * 6 + 0, 0, 0)),
                  pl.BlockSpec((None, 1, d), lambda i: ((i // per_seq) * 6 + 1, 0, 0)),
                  _resident((d, width)),
                  _resident((RANK_PAD, C_KEY_DIM)),
                  pl.BlockSpec((1, C_KEY_DIM), lambda i: (0, 0))],
        out_specs=tuple(pl.BlockSpec((tm, s.shape[1]), row) for s in out_shapes),
        scratch_shapes=[pltpu.VMEM((tm, d), BF16)],
        compiler_params=_params(("arbitrary",), est),
        name="inproj_odd",
    )(x2, mod3, mod3, w_bf, w_gk_b_bf, b_gk.reshape(1, C_KEY_DIM))


def kernel(x, c, positions,
           ada_w0, ada_b0, mix_w_in0, mix_w_out0, attn_sinks0, hgrn_lb_logits, hgrn_norm_w0,
           ln_mix_g0, ln_mix_b0, ffn_w_up0, ffn_conv_w0, ffn_conv_b0, ffn_w_down0, ln_ffn_g0, ln_ffn_b0,
           ada_w1, ada_b1, mix_w_in1, gla_w_gk_a1, gla_w_gk_b1, gla_b_gk1, gla_norm_w1, mix_w_out1,
           ln_mix_g1, ln_mix_b1, ffn_w_up1, ffn_conv_w1, ffn_conv_b1, ffn_w_down1, ln_ffn_g1, ln_ffn_b1):
    batch, seq, d = x.shape
    n = batch * seq
    x2 = x.reshape(n, d)
    c_pad = jnp.pad(c, ((0, 8 - batch), (0, 0)))

    def modulation(ada_w, ada_b):
        mod = _ada_modulation(c_pad, ada_w, ada_b)[:batch]
        return mod.reshape(batch * 6, 1, d)

    mod3 = modulation(ada_w0, ada_b0)
    cos, sin_signed = _rope_tables(positions.reshape(n, 1))
    q_a, k_a, v_a, q_b, k_b, lf_b, i_b, g_b = _inproj_even(
        x2, mod3, mix_w_in0.astype(BF16), cos, sin_signed, hgrn_lb_logits, seq=seq, layer=0)
    o_a = _swa_attention(q_a, k_a, v_a, attn_sinks0, batch=batch, seq=seq)
    o_b = _gla(q_b, k_b, i_b, lf_b, g_b, hgrn_norm_w0, batch=batch, seq=seq,
               heads=B_HEADS, dk=B_HEAD_DIM, dv=B_HEAD_DIM, heads_per_step=2, name="hgrn2")
    x2 = _outproj_ln([o_a, o_b], mix_w_out0.astype(BF16), x2, mod3, 2, ln_mix_g0, ln_mix_b0, seq=seq)
    x2 = _ffn(x2, mod3, ffn_w_up0.astype(BF16), ffn_conv_w0, ffn_conv_b0, ffn_w_down0.astype(BF16),
              ln_ffn_g0, ln_ffn_b0, seq=seq)

    mod3 = modulation(ada_w1, ada_b1)
    w_in1 = jnp.concatenate(
        [mix_w_in1, jnp.pad(gla_w_gk_a1, ((0, 0), (0, RANK_PAD - GATE_RANK)))], axis=1).astype(BF16)
    w_gk_b = jnp.pad(gla_w_gk_b1, ((0, RANK_PAD - GATE_RANK), (0, 0))).astype(BF16)
    q_c, k_c, v_c, g_c, lf_c = _inproj_odd(x2, mod3, w_in1, w_gk_b, gla_b_gk1, seq=seq)
    o_c = _gla(q_c, k_c, v_c, lf_c, g_c, gla_norm_w1, batch=batch, seq=seq,
               heads=C_HEADS, dk=C_DK, dv=C_DV, heads_per_step=1, name="gla")
    x2 = _outproj_ln([o_c], mix_w_out1.astype(BF16), x2, mod3, 2, ln_mix_g1, ln_mix_b1, seq=seq)
    x2 = _ffn(x2, mod3, ffn_w_up1.astype(BF16), ffn_conv_w1, ffn_conv_b1, ffn_w_down1.astype(BF16),
              ln_ffn_g1, ln_ffn_b1, seq=seq)
    return x2.reshape(batch, seq, d)
```

```python
import functools
import math

import jax
import jax.numpy as jnp
from jax import lax
from jax.experimental import pallas as pl
from jax.experimental.pallas import tpu as pltpu

F32 = jnp.float32
BF16 = jnp.bfloat16

D_MODEL = 2048
DEPTH = 2
HEAD_DIM = 64
A_Q_HEADS = 16
A_KV_HEADS = 4
A_Q_WIDTH = A_Q_HEADS * HEAD_DIM
A_KV_WIDTH = A_KV_HEADS * HEAD_DIM
WINDOW = 128
ROPE_THETA = 10000.0
B_HEADS = 8
B_HEAD_DIM = 128
B_WIDTH = B_HEADS * B_HEAD_DIM
EVEN_IN_WIDTH = A_Q_WIDTH + 2 * A_KV_WIDTH + 4 * B_WIDTH
C_HEADS = 4
C_KEY_DIM = D_MODEL // 2
C_VALUE_DIM = D_MODEL
C_DK = C_KEY_DIM // C_HEADS
C_DV = C_VALUE_DIM // C_HEADS
GATE_RANK = 16
GATE_NORMALIZER = 16.0
CHUNK = 64
D_FF = 5632
CONV_WIDTH = 3
LN_EPS = 1e-5
RMS_EPS = 1e-6
ALPHA = (2.0 * DEPTH) ** 0.25
NEG_INF = -1e30

V7X_LANES = 128
V7X_VMEM_BYTES = 64 * 1024 * 1024
V7X_VMEM_CEILING = 56 * 1024 * 1024
BF16_SUBLANE_PACK = 16

RANK_PAD = V7X_LANES


def _params(semantics, vmem_estimate):
    limit = min(V7X_VMEM_CEILING, max(16 * 1024 * 1024, int(vmem_estimate * 1.25)))
    return pltpu.CompilerParams(dimension_semantics=semantics, vmem_limit_bytes=limit)


def _resident(shape):
    return pl.BlockSpec(shape, lambda *_: (0,) * len(shape), pipeline_mode=pl.Buffered(1))


def _silu(x):
    return x * jax.nn.sigmoid(x)


def _layer_norm(z, g, b):
    mu = jnp.mean(z, axis=-1, keepdims=True)
    zc = z - mu
    var = jnp.mean(zc * zc, axis=-1, keepdims=True)
    return zc * lax.rsqrt(var + LN_EPS) * g + b


def _ada_kernel(c_ref, w_ref, b_ref, o_ref):
    a = _silu(c_ref[...]).astype(BF16)
    o_ref[...] = jnp.dot(a, w_ref[...].astype(BF16), preferred_element_type=F32) + b_ref[...]


def _ada_modulation(c_pad, ada_w, ada_b):
    rows, d = c_pad.shape
    n = ada_w.shape[1]
    tn = 1024
    est = 2 * d * tn * 4 + d * tn * 2 + 4 * rows * n
    return pl.pallas_call(
        _ada_kernel,
        out_shape=jax.ShapeDtypeStruct((rows, n), F32),
        grid=(n // tn,),
        in_specs=[pl.BlockSpec((rows, d), lambda j: (0, 0)),
                  pl.BlockSpec((d, tn), lambda j: (0, j)),
                  pl.BlockSpec((1, tn), lambda j: (0, j))],
        out_specs=pl.BlockSpec((rows, tn), lambda j: (0, j)),
        compiler_params=_params(("arbitrary",), est),
        name="ada_modulation",
    )(c_pad, ada_w, ada_b.reshape(1, n))


def _rope_table_kernel(pos_ref, invf_ref, sign_ref, cos_ref, sin_ref):
    ang = pos_ref[...].astype(F32) * invf_ref[...]
    cos_ref[...] = jnp.cos(ang)
    sin_ref[...] = jnp.sin(ang) * sign_ref[...]


def _rope_tables(pos_col):
    n = pos_col.shape[0]
    half = HEAD_DIM // 2
    lane = jnp.arange(V7X_LANES)
    inv_freq = ROPE_THETA ** (-jnp.arange(half, dtype=F32) / half)
    invf = inv_freq[lane % half].reshape(1, V7X_LANES)
    sign = jnp.where((lane % HEAD_DIM) < half, -1.0, 1.0).astype(F32).reshape(1, V7X_LANES)
    tm = 2048
    est = 2 * (tm * V7X_LANES * 4) * 3
    return pl.pallas_call(
        _rope_table_kernel,
        out_shape=(jax.ShapeDtypeStruct((n, V7X_LANES), F32),) * 2,
        grid=(n // tm,),
        in_specs=[pl.BlockSpec((tm, 1), lambda i: (i, 0)),
                  pl.BlockSpec((1, V7X_LANES), lambda i: (0, 0)),
                  pl.BlockSpec((1, V7X_LANES), lambda i: (0, 0))],
        out_specs=(pl.BlockSpec((tm, V7X_LANES), lambda i: (i, 0)),) * 2,
        compiler_params=_params(("arbitrary",), est),
        name="rope_tables",
    )(pos_col, invf, sign)


def _rope(a, cos, sin_signed, first_half):
    outs = []
    for s in range(a.shape[1] // V7X_LANES):
        blk = a[:, s * V7X_LANES:(s + 1) * V7X_LANES]
        partner = jnp.where(first_half,
                            pltpu.roll(blk, V7X_LANES - HEAD_DIM // 2, 1),
                            pltpu.roll(blk, HEAD_DIM // 2, 1))
        outs.append(blk * cos + partner * sin_signed)
    return jnp.concatenate(outs, axis=1) if len(outs) > 1 else outs[0]


PROJ_COLS = 512


def _inproj_even_kernel(x_ref, shift_ref, scale_ref, w_ref, cos_ref, sin_ref, lbl_ref,
                        qa_ref, ka_ref, va_ref, qb_ref, kb_ref, lf_ref, ib_ref, gb_ref, h_scr, *, layer):
    tm = x_ref.shape[0]
    h_scr[...] = (x_ref[...] * (1.0 + scale_ref[...]) + shift_ref[...]).astype(BF16)

    def proj(c0, width=PROJ_COLS):
        return jnp.dot(h_scr[...], w_ref[:, c0:c0 + width], preferred_element_type=F32)

    cos = cos_ref[...]
    sin_signed = sin_ref[...]
    lane = lax.broadcasted_iota(jnp.int32, (tm, V7X_LANES), 1)
    first_half = (lane % HEAD_DIM) < (HEAD_DIM // 2)

    for c0 in range(0, A_Q_WIDTH, PROJ_COLS):
        qa_ref[:, c0:c0 + PROJ_COLS] = (_rope(proj(c0), cos, sin_signed, first_half)
                                        * (HEAD_DIM ** -0.5)).astype(BF16)
    kv = proj(A_Q_WIDTH, 2 * A_KV_WIDTH)
    ka_ref[...] = _rope(kv[:, :A_KV_WIDTH], cos, sin_signed, first_half).astype(BF16)
    va_ref[...] = kv[:, A_KV_WIDTH:].astype(BF16)

    base = A_Q_WIDTH + 2 * A_KV_WIDTH
    lg = lbl_ref[...]
    e = jnp.exp(lg - jnp.max(lg, axis=0, keepdims=True))
    sm = e / jnp.sum(e, axis=0, keepdims=True)
    lb = jnp.sum(sm[0:layer + 1], axis=0, keepdims=True)
    for c0 in range(0, B_WIDTH, PROJ_COLS):
        sl = slice(c0, c0 + PROJ_COLS)
        qb_ref[:, sl] = proj(base + c0)
        lbc = lb[:, sl]
        fg = lbc + (1.0 - lbc) * jax.nn.sigmoid(proj(base + B_WIDTH + c0))
        kb_ref[:, sl] = 1.0 - fg
        lf_ref[:, sl] = jnp.log(fg)
        ib_ref[:, sl] = proj(base + 2 * B_WIDTH + c0).astype(BF16)
        gb_ref[:, sl] = proj(base + 3 * B_WIDTH + c0)


def _inproj_even(x2, mod3, w_bf, cos, sin_signed, lb_logits, *, seq, layer):
    n, d = x2.shape
    tm = 256
    per_seq = seq // tm
    width = w_bf.shape[1]
    row = lambda i: (i, 0)
    out_shapes = (
        jax.ShapeDtypeStruct((n, A_Q_WIDTH), BF16), jax.ShapeDtypeStruct((n, A_KV_WIDTH), BF16),
        jax.ShapeDtypeStruct((n, A_KV_WIDTH), BF16), jax.ShapeDtypeStruct((n, B_WIDTH), F32),
        jax.ShapeDtypeStruct((n, B_WIDTH), F32), jax.ShapeDtypeStruct((n, B_WIDTH), F32),
        jax.ShapeDtypeStruct((n, B_WIDTH), BF16), jax.ShapeDtypeStruct((n, B_WIDTH), F32))
    out_bytes = sum(tm * s.shape[1] * s.dtype.itemsize for s in out_shapes)
    est = d * width * 2 + 2 * tm * d * 4 + 2 * out_bytes + tm * d * 2 + 6 * tm * PROJ_COLS * 4
    return pl.pallas_call(
        functools.partial(_inproj_even_kernel, layer=layer),
        out_shape=out_shapes,
        grid=(n // tm,),
        in_specs=[pl.BlockSpec((tm, d), row),
                  pl.BlockSpec((None, 1, d), lambda i: ((i // per_seq) * 6 + 0, 0, 0)),
                  pl.BlockSpec((None, 1, d), lambda i: ((i // per_seq) * 6 + 1, 0, 0)),
                  _resident((d, width)),
                  pl.BlockSpec((tm, V7X_LANES), row),
                  pl.BlockSpec((tm, V7X_LANES), row),
                  pl.BlockSpec(lb_logits.shape, lambda i: (0, 0))],
        out_specs=tuple(pl.BlockSpec((tm, s.shape[1]), row) for s in out_shapes),
        scratch_shapes=[pltpu.VMEM((tm, d), BF16)],
        compiler_params=_params(("arbitrary",), est),
        name="inproj_even",
    )(x2, mod3, mod3, w_bf, cos, sin_signed, lb_logits)


def _swa_kernel(sink_ref, q_ref, kp_ref, kc_ref, vp_ref, vc_ref, o_ref):
    n = pl.program_id(1)
    grp = A_Q_HEADS // A_KV_HEADS
    assert HEAD_DIM * 2 == V7X_LANES and grp == 4
    r = lax.broadcasted_iota(jnp.int32, (WINDOW, 2 * WINDOW), 0)
    c = lax.broadcasted_iota(jnp.int32, (WINDOW, 2 * WINDOW), 1)
    rel = r + WINDOW - c
    valid = (rel >= 0) & (rel < WINDOW) & ((c >= WINDOW) | (n > 0))
    lane = lax.broadcasted_iota(jnp.int32, (1, V7X_LANES), 1)
    half_mask = [jnp.where(lane < HEAD_DIM, 1.0, 0.0).astype(BF16), jnp.where(lane >= HEAD_DIM, 1.0, 0.0).astype(BF16)]
    low_half = lax.broadcasted_iota(jnp.int32, (WINDOW, V7X_LANES), 1) < HEAD_DIM
    for g in range(A_KV_HEADS):
        kv_tile = slice((g // 2) * V7X_LANES, (g // 2 + 1) * V7X_LANES)
        kv_half = g % 2
        k2 = jnp.concatenate([kp_ref[:, kv_tile], kc_ref[:, kv_tile]], axis=0)
        v2 = jnp.concatenate([vp_ref[:, kv_tile], vc_ref[:, kv_tile]], axis=0)
        q_rows = []
        for e in range(grp):
            q_tile = (2 * g + e // 2) * V7X_LANES
            qt = q_ref[:, q_tile:q_tile + V7X_LANES]
            if e % 2 != kv_half:
                qt = pltpu.roll(qt, HEAD_DIM, 1)
            q_rows.append(qt * half_mask[kv_half])
        qs = jnp.concatenate(q_rows, axis=0)
        s_all = lax.dot_general(qs, k2, (((1,), (1,)), ((), ())), preferred_element_type=F32)
        p_rows = []
        denoms = []
        for e in range(grp):
            s = jnp.where(valid, s_all[e * WINDOW:(e + 1) * WINDOW], NEG_INF)
            sink = sink_ref[grp * g + e]
            m = jnp.maximum(jnp.max(s, axis=-1, keepdims=True), sink)
            p = jnp.exp(s - m)
            denoms.append(jnp.sum(p, axis=-1, keepdims=True) + jnp.exp(sink - m))
            p_rows.append(p.astype(BF16))
        o_all = jnp.dot(jnp.concatenate(p_rows, axis=0), v2, preferred_element_type=F32)
        for u in range(grp // 2):
            o_even = o_all[(2 * u) * WINDOW:(2 * u + 1) * WINDOW] / denoms[2 * u]
            o_odd = o_all[(2 * u + 1) * WINDOW:(2 * u + 2) * WINDOW] / denoms[2 * u + 1]
            if kv_half == 0:
                tile = jnp.where(low_half, o_even, pltpu.roll(o_odd, HEAD_DIM, 1))
            else:
                tile = jnp.where(low_half, pltpu.roll(o_even, HEAD_DIM, 1), o_odd)
            out_tile = (2 * g + u) * V7X_LANES
            o_ref[:, out_tile:out_tile + V7X_LANES] = tile.astype(BF16)


def _swa_attention(q_a, k_a, v_a, sinks, *, batch, seq):
    n = q_a.shape[0]
    nb = seq // WINDOW
    cur = lambda b, i: (b * nb + i, 0)
    prev = lambda b, i: (b * nb + jnp.maximum(i - 1, 0), 0)
    est = 2 * (2 * WINDOW * A_Q_WIDTH * 2 + 4 * WINDOW * A_KV_WIDTH * 2) + 16 * WINDOW * 2 * WINDOW * 4
    return pl.pallas_call(
        _swa_kernel,
        out_shape=jax.ShapeDtypeStruct((n, A_Q_WIDTH), BF16),
        grid=(batch, nb),
        in_specs=[pl.BlockSpec(memory_space=pltpu.SMEM),
                  pl.BlockSpec((WINDOW, A_Q_WIDTH), cur),
                  pl.BlockSpec((WINDOW, A_KV_WIDTH), prev),
                  pl.BlockSpec((WINDOW, A_KV_WIDTH), cur),
                  pl.BlockSpec((WINDOW, A_KV_WIDTH), prev),
                  pl.BlockSpec((WINDOW, A_KV_WIDTH), cur)],
        out_specs=pl.BlockSpec((WINDOW, A_Q_WIDTH), cur),
        compiler_params=_params(("arbitrary", "arbitrary"), est),
        name="swa_attention",
    )(sinks, q_a, k_a, k_a, v_a, v_a)


def _split3(x):
    hi = x.astype(BF16)
    r1 = x - hi.astype(F32)
    mid = r1.astype(BF16)
    lo = (r1 - mid.astype(F32)).astype(BF16)
    return hi, mid, lo


GLA_GROUP = 256


def _gla_kernel(q_ref, k_ref, v_ref, lf_ref, g_ref, nw_ref, o_ref, s_scr, *, dk, dv, heads_per_step):
    @pl.when(pl.program_id(2) == 0)
    def _():
        s_scr[...] = jnp.zeros_like(s_scr)

    tb = q_ref.shape[0]
    assert GLA_GROUP == 4 * CHUNK
    row = lax.broadcasted_iota(jnp.int32, (GLA_GROUP, GLA_GROUP), 0)
    col = lax.broadcasted_iota(jnp.int32, (GLA_GROUP, GLA_GROUP), 1)
    rc = row // CHUNK
    cc = col // CHUNK
    same_chunk = (rc == cc) & (row >= col)
    next_chunk = (rc == cc + 1) & (rc % 2 == 1)
    far_chunk = (rc >= 2) & (cc <= 1)
    tril = jnp.where(same_chunk, 1.0, 0.0).astype(BF16)
    nw = nw_ref[...]
    one = jnp.ones((1, dk), F32)
    nt = (((1,), (1,)), ((), ()))

    def by_chunk(vecs):
        return jnp.concatenate([jnp.broadcast_to(v, (CHUNK, dk)) for v in vecs], axis=0)

    for hh in range(heads_per_step):
        kcols = slice(hh * dk, (hh + 1) * dk)
        vcols = slice(hh * dv, (hh + 1) * dv)
        state = s_scr[hh]
        for gi in range(tb // GLA_GROUP):
            rows = slice(gi * GLA_GROUP, (gi + 1) * GLA_GROUP)
            hi, mid, lo = _split3(lf_ref[rows, kcols])
            bb = jnp.dot(tril, jnp.concatenate([hi, mid, lo], axis=1), preferred_element_type=F32)
            b = bb[:, :dk] + bb[:, dk:2 * dk] + bb[:, 2 * dk:]
            l0, l1, l2, l3 = [b[(c + 1) * CHUNK - 1:(c + 1) * CHUNK, :] for c in range(4)]
            b_mid = by_chunk([b[c * CHUNK + CHUNK // 2 - 1:c * CHUNK + CHUNK // 2, :] for c in range(4)])
            q = q_ref[rows, kcols]
            k = k_ref[rows, kcols]
            v = v_ref[rows, vcols]
            q_loc = q * jnp.exp(b)
            k_loc = k * jnp.exp(by_chunk([l0, l1, l2, l3]) - b)
            q_i = (q * jnp.exp(b - b_mid)).astype(BF16)
            k_i = (k * jnp.exp(b_mid - b)).astype(BF16)
            q_far = (q_loc * by_chunk([one, one, one, jnp.exp(l2)])).astype(BF16)
            k_far = (k_loc * by_chunk([jnp.exp(l1), one, one, one])).astype(BF16)
            q_grp = (q_loc * by_chunk([one, jnp.exp(l0), jnp.exp(l0 + l1), jnp.exp(l0 + l1 + l2)])).astype(BF16)
            k_grp = (k_loc * by_chunk([jnp.exp(l1 + l2 + l3), jnp.exp(l2 + l3), jnp.exp(l3), one])).astype(BF16)
            a_same = lax.dot_general(q_i, k_i, nt, preferred_element_type=F32)
            a_next = lax.dot_general(q_loc.astype(BF16), k_loc.astype(BF16), nt, preferred_element_type=F32)
            a_far = lax.dot_general(q_far, k_far, nt, preferred_element_type=F32)
            a = jnp.where(same_chunk, a_same, jnp.where(next_chunk, a_next, jnp.where(far_chunk, a_far, 0.0)))
            o = (jnp.dot(a.astype(BF16), v, preferred_element_type=F32)
                 + jnp.dot(q_grp, state.astype(BF16), preferred_element_type=F32))
            kv = lax.dot_general(k_grp, v, (((0,), (0,)), ((), ())), preferred_element_type=F32)
            decay = jnp.exp(jnp.transpose(jnp.broadcast_to(l0 + l1 + l2 + l3, (V7X_LANES, dk))))
            state = state * jnp.tile(decay, (1, dv // V7X_LANES)) + kv
            o = o * lax.rsqrt(jnp.mean(o * o, axis=-1, keepdims=True) + RMS_EPS)
            o_ref[rows, vcols] = ((o * nw) * _silu(g_ref[rows, vcols])).astype(BF16)
        s_scr[hh] = state


def _gla(q, k, v, log_f, g, norm_w, *, batch, seq, heads, dk, dv, heads_per_step, name):
    n = q.shape[0]
    tb = 1024
    nt = seq // tb
    hp = heads_per_step
    idx = lambda b, h, t: (b * nt + t, h)
    est = (2 * tb * hp * (3 * dk * 4 + dv * 2 + dv * 4 + dv * 2) + hp * dk * dv * 4 + 4 * dk * dv * 4
           + 16 * GLA_GROUP * max(3 * dk, dv) * 4)
    return pl.pallas_call(
        functools.partial(_gla_kernel, dk=dk, dv=dv, heads_per_step=hp),
        out_shape=jax.ShapeDtypeStruct((n, heads * dv), BF16),
        grid=(batch, heads // hp, nt),
        in_specs=[pl.BlockSpec((tb, hp * dk), idx), pl.BlockSpec((tb, hp * dk), idx),
                  pl.BlockSpec((tb, hp * dv), idx), pl.BlockSpec((tb, hp * dk), idx),
                  pl.BlockSpec((tb, hp * dv), idx),
                  pl.BlockSpec((1, dv), lambda b, h, t: (0, 0))],
        out_specs=pl.BlockSpec((tb, hp * dv), idx),
        scratch_shapes=[pltpu.VMEM((hp, dk, dv), F32)],
        compiler_params=_params(("arbitrary", "arbitrary", "arbitrary"), est),
        name=name,
    )(q, k, v, log_f, g, norm_w.reshape(1, dv))


OUTPROJ_SUB_ROWS = 256


def _outproj_ln_kernel(*refs, k_sizes):
    lhs_refs = refs[:len(k_sizes)]
    w_ref, x_ref, gate_ref, g_ref, b_ref, o_ref = refs[len(k_sizes):]
    for r0 in range(0, o_ref.shape[0], OUTPROJ_SUB_ROWS):
        rows = slice(r0, r0 + OUTPROJ_SUB_ROWS)
        y = None
        off = 0
        for r, ks in zip(lhs_refs, k_sizes):
            part = jnp.dot(r[rows, :], w_ref[off:off + ks, :], preferred_element_type=F32)
            y = part if y is None else y + part
            off += ks
        o_ref[rows, :] = _layer_norm(ALPHA * x_ref[rows, :] + gate_ref[...] * y, g_ref[...], b_ref[...])


def _outproj_ln(lhs_list, w_bf, x2, mod3, gate_slot, ln_g, ln_b, *, seq):
    n, d = x2.shape
    tm = 512
    per_seq = seq // tm
    row = lambda i: (i, 0)
    k_sizes = tuple(a.shape[1] for a in lhs_list)
    k_total = sum(k_sizes)
    vec = pl.BlockSpec((1, d), lambda i: (0, 0))
    est = k_total * d * 2 + 2 * tm * k_total * 2 + 4 * tm * d * 4 + 4 * tm * d * 4
    return pl.pallas_call(
        functools.partial(_outproj_ln_kernel, k_sizes=k_sizes),
        out_shape=jax.ShapeDtypeStruct((n, d), F32),
        grid=(n // tm,),
        in_specs=[pl.BlockSpec((tm, ks), row) for ks in k_sizes] + [
            _resident((k_total, d)),
            pl.BlockSpec((tm, d), row),
            pl.BlockSpec((None, 1, d), lambda i: ((i // per_seq) * 6 + gate_slot, 0, 0)),
            vec, vec],
        out_specs=pl.BlockSpec((tm, d), row),
        compiler_params=_params(("arbitrary",), est),
        name="outproj_ln",
    )(*lhs_list, w_bf, x2, mod3, ln_g.reshape(1, d), ln_b.reshape(1, d))


FFN_HALO = BF16_SUBLANE_PACK


def _ffn_kernel(x_ref, halo_ref, shift_ref, scale_ref, gate_ref, wu_ref, wv_ref, cw_ref, cb_ref, wd_ref,
                lng_ref, lnb_ref, o_ref, h_scr, u_scr, *, per_seq):
    i = pl.program_id(0)
    f = pl.program_id(1)
    tm = x_ref.shape[0]

    @pl.when(f == 0)
    def _():
        sc = 1.0 + scale_ref[...]
        sh = shift_ref[...]
        h_scr[0:FFN_HALO, :] = (halo_ref[...] * sc + sh).astype(BF16)
        h_scr[FFN_HALO:, :] = (x_ref[...] * sc + sh).astype(BF16)
        o_ref[...] = jnp.zeros_like(o_ref)

    u_full = jnp.dot(h_scr[...], wu_ref[...], preferred_element_type=F32)
    keep = jnp.where(i % per_seq == 0, 0.0, 1.0)
    u_scr[0:FFN_HALO, :] = u_full[0:FFN_HALO, :] * keep
    u_scr[FFN_HALO:, :] = u_full[FFN_HALO:, :]
    cw = cw_ref[...]
    u = (cw[2:3, :] * u_scr[FFN_HALO:FFN_HALO + tm, :]
         + cw[1:2, :] * u_scr[FFN_HALO - 1:FFN_HALO - 1 + tm, :]
         + cw[0:1, :] * u_scr[FFN_HALO - 2:FFN_HALO - 2 + tm, :]) + cb_ref[...]
    v = jnp.dot(h_scr[FFN_HALO:, :], wv_ref[...], preferred_element_type=F32)
    act = (_silu(u) * v).astype(BF16)
    o_ref[...] += jnp.dot(act, wd_ref[...], preferred_element_type=F32)

    @pl.when(f == pl.num_programs(1) - 1)
    def _():
        o_ref[...] = _layer_norm(ALPHA * x_ref[...] + gate_ref[...] * o_ref[...], lng_ref[...], lnb_ref[...])


def _ffn(x2, mod3, w_up_bf, conv_w, conv_b, w_down_bf, ln_g, ln_b, *, seq):
    n, d = x2.shape
    tm = 512
    tf = 512
    per_seq = seq // tm
    nf = D_FF // tf
    halo_blocks = tm // FFN_HALO
    slot = lambda s: (lambda i, f: ((i // per_seq) * 6 + s, 0, 0))
    vec = pl.BlockSpec((1, d), lambda i, f: (0, 0))
    est = (4 * tm * d * 4 + 2 * FFN_HALO * d * 4 + 2 * 2 * d * tf * 2 + 2 * tf * d * 2
           + (tm + FFN_HALO) * d * 2 + (tm + FFN_HALO) * tf * 4 + 8 * tm * tf * 4 + 2 * tm * d * 4)
    return pl.pallas_call(
        functools.partial(_ffn_kernel, per_seq=per_seq),
        out_shape=jax.ShapeDtypeStruct((n, d), F32),
        grid=(n // tm, nf),
        in_specs=[pl.BlockSpec((tm, d), lambda i, f: (i, 0)),
                  pl.BlockSpec((FFN_HALO, d), lambda i, f: (jnp.maximum(i * halo_blocks - 1, 0), 0)),
                  pl.BlockSpec((None, 1, d), slot(3)),
                  pl.BlockSpec((None, 1, d), slot(4)),
                  pl.BlockSpec((None, 1, d), slot(5)),
                  pl.BlockSpec((d, tf), lambda i, f: (0, f)),
                  pl.BlockSpec((d, tf), lambda i, f: (0, nf + f)),
                  pl.BlockSpec((CONV_WIDTH, tf), lambda i, f: (0, f)),
                  pl.BlockSpec((1, tf), lambda i, f: (0, f)),
                  pl.BlockSpec((tf, d), lambda i, f: (f, 0)),
                  vec, vec],
        out_specs=pl.BlockSpec((tm, d), lambda i, f: (i, 0)),
        scratch_shapes=[pltpu.VMEM((tm + FFN_HALO, d), BF16), pltpu.VMEM((tm + FFN_HALO, tf), F32)],
        compiler_params=_params(("arbitrary", "arbitrary"), est),
        name="ffn",
    )(x2, x2, mod3, mod3, mod3, w_up_bf, w_up_bf, conv_w, conv_b.reshape(1, D_FF), w_down_bf,
      ln_g.reshape(1, d), ln_b.reshape(1, d))


def _inproj_odd_kernel(x_ref, shift_ref, scale_ref, w_ref, wgb_ref, bg_ref,
                       q_ref, k_ref, v_ref, g_ref, lf_ref, h_scr):
    h_scr[...] = (x_ref[...] * (1.0 + scale_ref[...]) + shift_ref[...]).astype(BF16)

    def proj(c0, width=PROJ_COLS):
        return jnp.dot(h_scr[...], w_ref[:, c0:c0 + width], preferred_element_type=F32)

    low = proj(2 * C_KEY_DIM + 2 * C_VALUE_DIM, RANK_PAD).astype(BF16)
    for c0 in range(0, C_KEY_DIM, PROJ_COLS):
        sl = slice(c0, c0 + PROJ_COLS)
        gk = jnp.dot(low, wgb_ref[:, sl], preferred_element_type=F32) + bg_ref[:, sl]
        lf_ref[:, sl] = jax.nn.log_sigmoid(gk) * (1.0 / GATE_NORMALIZER)
    for c0 in range(0, C_KEY_DIM, PROJ_COLS):
        sl = slice(c0, c0 + PROJ_COLS)
        q_ref[:, sl] = proj(c0) * (C_DK ** -0.5)
        k_ref[:, sl] = proj(C_KEY_DIM + c0)
    for c0 in range(0, C_VALUE_DIM, PROJ_COLS):
        sl = slice(c0, c0 + PROJ_COLS)
        v_ref[:, sl] = proj(2 * C_KEY_DIM + c0).astype(BF16)
        g_ref[:, sl] = proj(2 * C_KEY_DIM + C_VALUE_DIM + c0)


def _inproj_odd(x2, mod3, w_bf, w_gk_b_bf, b_gk, *, seq):
    n, d = x2.shape
    tm = 256
    per_seq = seq // tm
    width = w_bf.shape[1]
    row = lambda i: (i, 0)
    out_shapes = (
        jax.ShapeDtypeStruct((n, C_KEY_DIM), F32), jax.ShapeDtypeStruct((n, C_KEY_DIM), F32),
        jax.ShapeDtypeStruct((n, C_VALUE_DIM), BF16), jax.ShapeDtypeStruct((n, C_VALUE_DIM), F32),
        jax.ShapeDtypeStruct((n, C_KEY_DIM), F32))
    out_bytes = sum(tm * s.shape[1] * s.dtype.itemsize for s in out_shapes)
    est = d * width * 2 + 2 * tm * d * 4 + 2 * out_bytes + tm * d * 2 + 6 * tm * PROJ_COLS * 4
    return pl.pallas_call(
        _inproj_odd_kernel,
        out_shape=out_shapes,
        grid=(n // tm,),
        in_specs=[pl.BlockSpec((tm, d), row),
                  pl.BlockSpec((None, 1, d), lambda i: ((i // per_seq) * 6 + 0, 0, 0)),
                  pl.BlockSpec((None, 1, d), lambda i: ((i // per_seq) * 6 + 1, 0, 0)),
                  _resident((d, width)),
                  _resident((RANK_PAD, C_KEY_DIM)),
                  pl.BlockSpec((1, C_KEY_DIM), lambda i: (0, 0))],
        out_specs=tuple(pl.BlockSpec((tm, s.shape[1]), row) for s in out_shapes),
        scratch_shapes=[pltpu.VMEM((tm, d), BF16)],
        compiler_params=_params(("arbitrary",), est),
        name="inproj_odd",
    )(x2, mod3, mod3, w_bf, w_gk_b_bf, b_gk.reshape(1, C_KEY_DIM))


def kernel(x, c, positions,
           ada_w0, ada_b0, mix_w_in0, mix_w_out0, attn_sinks0, hgrn_lb_logits, hgrn_norm_w0,
           ln_mix_g0, ln_mix_b0, ffn_w_up0, ffn_conv_w0, ffn_conv_b0, ffn_w_down0, ln_ffn_g0, ln_ffn_b0,
           ada_w1, ada_b1, mix_w_in1, gla_w_gk_a1, gla_w_gk_b1, gla_b_gk1, gla_norm_w1, mix_w_out1,
           ln_mix_g1, ln_mix_b1, ffn_w_up1, ffn_conv_w1, ffn_conv_b1, ffn_w_down1, ln_ffn_g1, ln_ffn_b1):
    batch, seq, d = x.shape
    n = batch * seq
    x2 = x.reshape(n, d)
    c_pad = jnp.pad(c, ((0, 8 - batch), (0, 0)))

    def modulation(ada_w, ada_b):
        mod = _ada_modulation(c_pad, ada_w, ada_b)[:batch]
        return mod.reshape(batch * 6, 1, d)

    mod3 = modulation(ada_w0, ada_b0)
    cos, sin_signed = _rope_tables(positions.reshape(n, 1))
    q_a, k_a, v_a, q_b, k_b, lf_b, i_b, g_b = _inproj_even(
        x2, mod3, mix_w_in0.astype(BF16), cos, sin_signed, hgrn_lb_logits, seq=seq, layer=0)
    o_a = _swa_attention(q_a, k_a, v_a, attn_sinks0, batch=batch, seq=seq)
    o_b = _gla(q_b, k_b, i_b, lf_b, g_b, hgrn_norm_w0, batch=batch, seq=seq,
               heads=B_HEADS, dk=B_HEAD_DIM, dv=B_HEAD_DIM, heads_per_step=2, name="hgrn2")
    x2 = _outproj_ln([o_a, o_b], mix_w_out0.astype(BF16), x2, mod3, 2, ln_mix_g0, ln_mix_b0, seq=seq)
    x2 = _ffn(x2, mod3, ffn_w_up0.astype(BF16), ffn_conv_w0, ffn_conv_b0, ffn_w_down0.astype(BF16),
              ln_ffn_g0, ln_ffn_b0, seq=seq)

    mod3 = modulation(ada_w1, ada_b1)
    w_in1 = jnp.concatenate(
        [mix_w_in1, jnp.pad(gla_w_gk_a1, ((0, 0), (0, RANK_PAD - GATE_RANK)))], axis=1).astype(BF16)
    w_gk_b = jnp.pad(gla_w_gk_b1, ((0, RANK_PAD - GATE_RANK), (0, 0))).astype(BF16)
    q_c, k_c, v_c, g_c, lf_c = _inproj_odd(x2, mod3, w_in1, w_gk_b, gla_b_gk1, seq=seq)
    o_c = _gla(q_c, k_c, v_c, lf_c, g_c, gla_norm_w1, batch=batch, seq=seq,
               heads=C_HEADS, dk=C_DK, dv=C_DV, heads_per_step=1, name="gla")
    x2 = _outproj_ln([o_c], mix_w_out1.astype(BF16), x2, mod3, 2, ln_mix_g1, ln_mix_b1, seq=seq)
    x2 = _ffn(x2, mod3, ffn_w_up1.astype(BF16), ffn_conv_w1, ffn_conv_b1, ffn_w_down1.astype(BF16),
              ln_ffn_g1, ln_ffn_b1, seq=seq)
    return x2.reshape(batch, seq, d)
```

```python
import functools
import math

import jax
import jax.numpy as jnp
from jax import lax
from jax.experimental import pallas as pl
from jax.experimental.pallas import tpu as pltpu

F32 = jnp.float32
BF16 = jnp.bfloat16

D_MODEL = 2048
DEPTH = 2
HEAD_DIM = 64
A_Q_HEADS = 16
A_KV_HEADS = 4
A_Q_WIDTH = A_Q_HEADS * HEAD_DIM
A_KV_WIDTH = A_KV_HEADS * HEAD_DIM
WINDOW = 128
ROPE_THETA = 10000.0
B_HEADS = 8
B_HEAD_DIM = 128
B_WIDTH = B_HEADS * B_HEAD_DIM
EVEN_IN_WIDTH = A_Q_WIDTH + 2 * A_KV_WIDTH + 4 * B_WIDTH
C_HEADS = 4
C_KEY_DIM = D_MODEL // 2
C_VALUE_DIM = D_MODEL
C_DK = C_KEY_DIM // C_HEADS
C_DV = C_VALUE_DIM // C_HEADS
GATE_RANK = 16
GATE_NORMALIZER = 16.0
CHUNK = 64
D_FF = 5632
CONV_WIDTH = 3
LN_EPS = 1e-5
RMS_EPS = 1e-6
ALPHA = (2.0 * DEPTH) ** 0.25
NEG_INF = -1e30

V7X_LANES = 128
V7X_VMEM_BYTES = 64 * 1024 * 1024
V7X_VMEM_CEILING = 56 * 1024 * 1024
BF16_SUBLANE_PACK = 16

RANK_PAD = V7X_LANES


def _params(semantics, vmem_estimate):
    limit = min(V7X_VMEM_CEILING, max(16 * 1024 * 1024, int(vmem_estimate * 1.25)))
    return pltpu.CompilerParams(dimension_semantics=semantics, vmem_limit_bytes=limit)


def _resident(shape):
    return pl.BlockSpec(shape, lambda *_: (0,) * len(shape), pipeline_mode=pl.Buffered(1))


def _silu(x):
    return x * jax.nn.sigmoid(x)


def _layer_norm(z, g, b):
    mu = jnp.mean(z, axis=-1, keepdims=True)
    zc = z - mu
    var = jnp.mean(zc * zc, axis=-1, keepdims=True)
    return zc * lax.rsqrt(var + LN_EPS) * g + b


def _ada_kernel(c_ref, w_ref, b_ref, o_ref):
    a = _silu(c_ref[...]).astype(BF16)
    o_ref[...] = jnp.dot(a, w_ref[...].astype(BF16), preferred_element_type=F32) + b_ref[...]


def _ada_modulation(c_pad, ada_w, ada_b):
    rows, d = c_pad.shape
    n = ada_w.shape[1]
    tn = 1024
    est = 2 * d * tn * 4 + d * tn * 2 + 4 * rows * n
    return pl.pallas_call(
        _ada_kernel,
        out_shape=jax.ShapeDtypeStruct((rows, n), F32),
        grid=(n // tn,),
        in_specs=[pl.BlockSpec((rows, d), lambda j: (0, 0)),
                  pl.BlockSpec((d, tn), lambda j: (0, j)),
                  pl.BlockSpec((1, tn), lambda j: (0, j))],
        out_specs=pl.BlockSpec((rows, tn), lambda j: (0, j)),
        compiler_params=_params(("arbitrary",), est),
        name="ada_modulation",
    )(c_pad, ada_w, ada_b.reshape(1, n))


def _rope_table_kernel(pos_ref, invf_ref, sign_ref, cos_ref, sin_ref):
    ang = pos_ref[...].astype(F32) * invf_ref[...]
    cos_ref[...] = jnp.cos(ang)
    sin_ref[...] = jnp.sin(ang) * sign_ref[...]


def _rope_tables(pos_col):
    n = pos_col.shape[0]
    half = HEAD_DIM // 2
    lane = jnp.arange(V7X_LANES)
    inv_freq = ROPE_THETA ** (-jnp.arange(half, dtype=F32) / half)
    invf = inv_freq[lane % half].reshape(1, V7X_LANES)
    sign = jnp.where((lane % HEAD_DIM) < half, -1.0, 1.0).astype(F32).reshape(1, V7X_LANES)
    tm = 2048
    est = 2 * (tm * V7X_LANES * 4) * 3
    return pl.pallas_call(
        _rope_table_kernel,
        out_shape=(jax.ShapeDtypeStruct((n, V7X_LANES), F32),) * 2,
        grid=(n // tm,),
        in_specs=[pl.BlockSpec((tm, 1), lambda i: (i, 0)),
                  pl.BlockSpec((1, V7X_LANES), lambda i: (0, 0)),
                  pl.BlockSpec((1, V7X_LANES), lambda i: (0, 0))],
        out_specs=(pl.BlockSpec((tm, V7X_LANES), lambda i: (i, 0)),) * 2,
        compiler_params=_params(("arbitrary",), est),
        name="rope_tables",
    )(pos_col, invf, sign)


def _rope(a, cos, sin_signed, first_half):
    outs = []
    for s in range(a.shape[1] // V7X_LANES):
        blk = a[:, s * V7X_LANES:(s + 1) * V7X_LANES]
        partner = jnp.where(first_half,
                            pltpu.roll(blk, V7X_LANES - HEAD_DIM // 2, 1),
                            pltpu.roll(blk, HEAD_DIM // 2, 1))
        outs.append(blk * cos + partner * sin_signed)
    return jnp.concatenate(outs, axis=1) if len(outs) > 1 else outs[0]


PROJ_COLS = 512


def _inproj_even_kernel(x_ref, shift_ref, scale_ref, w_ref, cos_ref, sin_ref, lbl_ref,
                        qa_ref, ka_ref, va_ref, qb_ref, kb_ref, lf_ref, ib_ref, gb_ref, h_scr, *, layer):
    tm = x_ref.shape[0]
    h_scr[...] = (x_ref[...] * (1.0 + scale_ref[...]) + shift_ref[...]).astype(BF16)

    def proj(c0, width=PROJ_COLS):
        return jnp.dot(h_scr[...], w_ref[:, c0:c0 + width], preferred_element_type=F32)

    cos = cos_ref[...]
    sin_signed = sin_ref[...]
    lane = lax.broadcasted_iota(jnp.int32, (tm, V7X_LANES), 1)
    first_half = (lane % HEAD_DIM) < (HEAD_DIM // 2)

    for c0 in range(0, A_Q_WIDTH, PROJ_COLS):
        qa_ref[:, c0:c0 + PROJ_COLS] = (_rope(proj(c0), cos, sin_signed, first_half)
                                        * (HEAD_DIM ** -0.5)).astype(BF16)
    kv = proj(A_Q_WIDTH, 2 * A_KV_WIDTH)
    ka_ref[...] = _rope(kv[:, :A_KV_WIDTH], cos, sin_signed, first_half).astype(BF16)
    va_ref[...] = kv[:, A_KV_WIDTH:].astype(BF16)

    base = A_Q_WIDTH + 2 * A_KV_WIDTH
    lg = lbl_ref[...]
    e = jnp.exp(lg - jnp.max(lg, axis=0, keepdims=True))
    sm = e / jnp.sum(e, axis=0, keepdims=True)
    lb = jnp.sum(sm[0:layer + 1], axis=0, keepdims=True)
    for c0 in range(0, B_WIDTH, PROJ_COLS):
        sl = slice(c0, c0 + PROJ_COLS)
        qb_ref[:, sl] = proj(base + c0)
        lbc = lb[:, sl]
        fg = lbc + (1.0 - lbc) * jax.nn.sigmoid(proj(base + B_WIDTH + c0))
        kb_ref[:, sl] = 1.0 - fg
        lf_ref[:, sl] = jnp.log(fg)
        ib_ref[:, sl] = proj(base + 2 * B_WIDTH + c0).astype(BF16)
        gb_ref[:, sl] = proj(base + 3 * B_WIDTH + c0)


def _inproj_even(x2, mod3, w_bf, cos, sin_signed, lb_logits, *, seq, layer):
    n, d = x2.shape
    tm = 256
    per_seq = seq // tm
    width = w_bf.shape[1]
    row = lambda i: (i, 0)
    out_shapes = (
        jax.ShapeDtypeStruct((n, A_Q_WIDTH), BF16), jax.ShapeDtypeStruct((n, A_KV_WIDTH), BF16),
        jax.ShapeDtypeStruct((n, A_KV_WIDTH), BF16), jax.ShapeDtypeStruct((n, B_WIDTH), F32),
        jax.ShapeDtypeStruct((n, B_WIDTH), F32), jax.ShapeDtypeStruct((n, B_WIDTH), F32),
        jax.ShapeDtypeStruct((n, B_WIDTH), BF16), jax.ShapeDtypeStruct((n, B_WIDTH), F32))
    out_bytes = sum(tm * s.shape[1] * s.dtype.itemsize for s in out_shapes)
    est = d * width * 2 + 2 * tm * d * 4 + 2 * out_bytes + tm * d * 2 + 6 * tm * PROJ_COLS * 4
    return pl.pallas_call(
        functools.partial(_inproj_even_kernel, layer=layer),
        out_shape=out_shapes,
        grid=(n // tm,),
        in_specs=[pl.BlockSpec((tm, d), row),
                  pl.BlockSpec((None, 1, d), lambda i: ((i // per_seq) * 6 + 0, 0, 0)),
                  pl.BlockSpec((None, 1, d), lambda i: ((i // per_seq) * 6 + 1, 0, 0)),
                  _resident((d, width)),
                  pl.BlockSpec((tm, V7X_LANES), row),
                  pl.BlockSpec((tm, V7X_LANES), row),
                  pl.BlockSpec(lb_logits.shape, lambda i: (0, 0))],
        out_specs=tuple(pl.BlockSpec((tm, s.shape[1]), row) for s in out_shapes),
        scratch_shapes=[pltpu.VMEM((tm, d), BF16)],
        compiler_params=_params(("arbitrary",), est),
        name="inproj_even",
    )(x2, mod3, mod3, w_bf, cos, sin_signed, lb_logits)


def _swa_kernel(sink_ref, q_ref, kp_ref, kc_ref, vp_ref, vc_ref, o_ref):
    first_step = pl.program_id(1) == 0
    grp = A_Q_HEADS // A_KV_HEADS
    assert HEAD_DIM * 2 == V7X_LANES and grp == 4
    r = lax.broadcasted_iota(jnp.int32, (WINDOW, 2 * WINDOW), 0)
    c = lax.broadcasted_iota(jnp.int32, (WINDOW, 2 * WINDOW), 1)
    rel = r + WINDOW - c
    band = (rel >= 0) & (rel < WINDOW)
    band_first = band & ((c >= WINDOW) | jnp.logical_not(first_step))
    lane = lax.broadcasted_iota(jnp.int32, (1, V7X_LANES), 1)
    half_mask = [jnp.where(lane < HEAD_DIM, 1.0, 0.0).astype(BF16),
                 jnp.where(lane >= HEAD_DIM, 1.0, 0.0).astype(BF16)]
    low_half = lax.broadcasted_iota(jnp.int32, (WINDOW, V7X_LANES), 1) < HEAD_DIM

    def frame(prev_ref, cur_ref, sub, kv_tile):
        own = cur_ref[sub * WINDOW:(sub + 1) * WINDOW, kv_tile]
        before = prev_ref[:, kv_tile] if sub == 0 else cur_ref[(sub - 1) * WINDOW:sub * WINDOW, kv_tile]
        return jnp.concatenate([before, own], axis=0)

    def scores(sub, g):
        kv_tile = slice((g // 2) * V7X_LANES, (g // 2 + 1) * V7X_LANES)
        kv_half = g % 2
        q_rows = []
        for e in range(grp):
            q_tile = (2 * g + e // 2) * V7X_LANES
            qt = q_ref[sub * WINDOW:(sub + 1) * WINDOW, q_tile:q_tile + V7X_LANES]
            if e % 2 != kv_half:
                qt = pltpu.roll(qt, HEAD_DIM, 1)
            q_rows.append(qt * half_mask[kv_half])
        qs = jnp.concatenate(q_rows, axis=0)
        return lax.dot_general(qs, frame(kp_ref, kc_ref, sub, kv_tile), (((1,), (1,)), ((), ())),
                               preferred_element_type=F32)

    units = [(sub, g) for sub in range(q_ref.shape[0] // WINDOW) for g in range(A_KV_HEADS)]
    s_next = scores(*units[0])
    for idx, (sub, g) in enumerate(units):
        kv_tile = slice((g // 2) * V7X_LANES, (g // 2 + 1) * V7X_LANES)
        kv_half = g % 2
        qrows = slice(sub * WINDOW, (sub + 1) * WINDOW)
        valid = band_first if sub == 0 else band
        s_all = s_next
        if idx + 1 < len(units):
            s_next = scores(*units[idx + 1])
        p_rows = []
        denoms = []
        for e in range(grp):
            s = jnp.where(valid, s_all[e * WINDOW:(e + 1) * WINDOW], NEG_INF)
            sink = sink_ref[grp * g + e]
            m = jnp.maximum(jnp.max(s, axis=-1, keepdims=True), sink)
            p = jnp.exp(s - m)
            denoms.append(jnp.sum(p, axis=-1, keepdims=True) + jnp.exp(sink - m))
            p_rows.append(p.astype(BF16))
        o_all = jnp.dot(jnp.concatenate(p_rows, axis=0), frame(vp_ref, vc_ref, sub, kv_tile),
                        preferred_element_type=F32)
        for u in range(grp // 2):
            o_even = o_all[(2 * u) * WINDOW:(2 * u + 1) * WINDOW] / denoms[2 * u]
            o_odd = o_all[(2 * u + 1) * WINDOW:(2 * u + 2) * WINDOW] / denoms[2 * u + 1]
            if kv_half == 0:
                tile = jnp.where(low_half, o_even, pltpu.roll(o_odd, HEAD_DIM, 1))
            else:
                tile = jnp.where(low_half, pltpu.roll(o_even, HEAD_DIM, 1), o_odd)
            out_tile = (2 * g + u) * V7X_LANES
            o_ref[qrows, out_tile:out_tile + V7X_LANES] = tile.astype(BF16)


SWA_BLOCKS_PER_STEP = 1


def _swa_attention(q_a, k_a, v_a, sinks, *, batch, seq):
    n = q_a.shape[0]
    tq = SWA_BLOCKS_PER_STEP * WINDOW
    nb = seq // tq
    cur = lambda b, i: (b * nb + i, 0)
    prev = lambda b, i: (jnp.maximum((b * nb + i) * SWA_BLOCKS_PER_STEP - 1, 0), 0)
    est = 2 * (2 * tq * A_Q_WIDTH * 2 + 2 * (tq + WINDOW) * A_KV_WIDTH * 2) + 24 * WINDOW * 2 * WINDOW * 4
    return pl.pallas_call(
        _swa_kernel,
        out_shape=jax.ShapeDtypeStruct((n, A_Q_WIDTH), BF16),
        grid=(batch, nb),
        in_specs=[pl.BlockSpec(memory_space=pltpu.SMEM),
                  pl.BlockSpec((tq, A_Q_WIDTH), cur),
                  pl.BlockSpec((WINDOW, A_KV_WIDTH), prev),
                  pl.BlockSpec((tq, A_KV_WIDTH), cur),
                  pl.BlockSpec((WINDOW, A_KV_WIDTH), prev),
                  pl.BlockSpec((tq, A_KV_WIDTH), cur)],
        out_specs=pl.BlockSpec((tq, A_Q_WIDTH), cur),
        compiler_params=_params(("arbitrary", "arbitrary"), est),
        name="swa_attention",
    )(sinks, q_a, k_a, k_a, v_a, v_a)


def _split3(x):
    hi = x.astype(BF16)
    r1 = x - hi.astype(F32)
    mid = r1.astype(BF16)
    lo = (r1 - mid.astype(F32)).astype(BF16)
    return hi, mid, lo


GLA_GROUP = 256


def _gla_kernel(q_ref, k_ref, v_ref, lf_ref, g_ref, nw_ref, o_ref, s_scr, *, dk, dv, heads_per_step):
    @pl.when(pl.program_id(2) == 0)
    def _():
        s_scr[...] = jnp.zeros_like(s_scr)

    tb = q_ref.shape[0]
    assert GLA_GROUP == 4 * CHUNK
    row = lax.broadcasted_iota(jnp.int32, (GLA_GROUP, GLA_GROUP), 0)
    col = lax.broadcasted_iota(jnp.int32, (GLA_GROUP, GLA_GROUP), 1)
    rc = row // CHUNK
    cc = col // CHUNK
    same_chunk = (rc == cc) & (row >= col)
    next_chunk = (rc == cc + 1) & (rc % 2 == 1)
    far_chunk = (rc >= 2) & (cc <= 1)
    tril = jnp.where(same_chunk, 1.0, 0.0).astype(BF16)
    nw = nw_ref[...]
    one = jnp.ones((1, dk), F32)
    nt = (((1,), (1,)), ((), ()))

    def by_chunk(vecs):
        return jnp.concatenate([jnp.broadcast_to(v, (CHUNK, dk)) for v in vecs], axis=0)

    def prepare(hh, gi):
        rows = slice(gi * GLA_GROUP, (gi + 1) * GLA_GROUP)
        kcols = slice(hh * dk, (hh + 1) * dk)
        vcols = slice(hh * dv, (hh + 1) * dv)
        hi, mid, lo = _split3(lf_ref[rows, kcols])
        bb = jnp.dot(tril, jnp.concatenate([hi, mid, lo], axis=1), preferred_element_type=F32)
        b = bb[:, :dk] + bb[:, dk:2 * dk] + bb[:, 2 * dk:]
        l0, l1, l2, l3 = [b[(c + 1) * CHUNK - 1:(c + 1) * CHUNK, :] for c in range(4)]
        b_mid = by_chunk([b[c * CHUNK + CHUNK // 2 - 1:c * CHUNK + CHUNK // 2, :] for c in range(4)])
        q = q_ref[rows, kcols]
        k = k_ref[rows, kcols]
        v = v_ref[rows, vcols]
        q_loc = q * jnp.exp(b)
        k_loc = k * jnp.exp(by_chunk([l0, l1, l2, l3]) - b)
        q_i = (q * jnp.exp(b - b_mid)).astype(BF16)
        k_i = (k * jnp.exp(b_mid - b)).astype(BF16)
        q_far = (q_loc * by_chunk([one, one, one, jnp.exp(l2)])).astype(BF16)
        k_far = (k_loc * by_chunk([jnp.exp(l1), one, one, one])).astype(BF16)
        q_grp = (q_loc * by_chunk([one, jnp.exp(l0), jnp.exp(l0 + l1), jnp.exp(l0 + l1 + l2)])).astype(BF16)
        k_grp = (k_loc * by_chunk([jnp.exp(l1 + l2 + l3), jnp.exp(l2 + l3), jnp.exp(l3), one])).astype(BF16)
        a_same = lax.dot_general(q_i, k_i, nt, preferred_element_type=F32)
        a_next = lax.dot_general(q_loc.astype(BF16), k_loc.astype(BF16), nt, preferred_element_type=F32)
        a_far = lax.dot_general(q_far, k_far, nt, preferred_element_type=F32)
        a = jnp.where(same_chunk, a_same, jnp.where(next_chunk, a_next, jnp.where(far_chunk, a_far, 0.0)))
        o_intra = jnp.dot(a.astype(BF16), v, preferred_element_type=F32)
        kv = lax.dot_general(k_grp, v, (((0,), (0,)), ((), ())), preferred_element_type=F32)
        decay = jnp.exp(jnp.transpose(jnp.broadcast_to(l0 + l1 + l2 + l3, (V7X_LANES, dk))))
        return rows, vcols, o_intra, q_grp, kv, decay

    def finish(state, prepared):
        rows, vcols, o_intra, q_grp, kv, decay = prepared
        o = o_intra + jnp.dot(q_grp, state.astype(BF16), preferred_element_type=F32)
        o = o * lax.rsqrt(jnp.mean(o * o, axis=-1, keepdims=True) + RMS_EPS)
        o_ref[rows, vcols] = ((o * nw) * _silu(g_ref[rows, vcols])).astype(BF16)
        return state * jnp.tile(decay, (1, dv // V7X_LANES)) + kv

    units = [(hh, gi) for gi in range(tb // GLA_GROUP) for hh in range(heads_per_step)]
    states = [s_scr[hh] for hh in range(heads_per_step)]
    pending = prepare(*units[0])
    for idx, (hh, gi) in enumerate(units):
        upcoming = prepare(*units[idx + 1]) if idx + 1 < len(units) else None
        states[hh] = finish(states[hh], pending)
        pending = upcoming
    for hh in range(heads_per_step):
        s_scr[hh] = states[hh]


def _gla(q, k, v, log_f, g, norm_w, *, batch, seq, heads, dk, dv, heads_per_step, name):
    n = q.shape[0]
    tb = 1024
    nt = seq // tb
    hp = heads_per_step
    idx = lambda b, h, t: (b * nt + t, h)
    est = (2 * tb * hp * (3 * dk * 4 + dv * 2 + dv * 4 + dv * 2) + hp * dk * dv * 4 + 4 * dk * dv * 4
           + 16 * GLA_GROUP * max(3 * dk, dv) * 4)
    return pl.pallas_call(
        functools.partial(_gla_kernel, dk=dk, dv=dv, heads_per_step=hp),
        out_shape=jax.ShapeDtypeStruct((n, heads * dv), BF16),
        grid=(batch, heads // hp, nt),
        in_specs=[pl.BlockSpec((tb, hp * dk), idx), pl.BlockSpec((tb, hp * dk), idx),
                  pl.BlockSpec((tb, hp * dv), idx), pl.BlockSpec((tb, hp * dk), idx),
                  pl.BlockSpec((tb, hp * dv), idx),
                  pl.BlockSpec((1, dv), lambda b, h, t: (0, 0))],
        out_specs=pl.BlockSpec((tb, hp * dv), idx),
        scratch_shapes=[pltpu.VMEM((hp, dk, dv), F32)],
        compiler_params=_params(("arbitrary", "arbitrary", "arbitrary"), est),
        name=name,
    )(q, k, v, log_f, g, norm_w.reshape(1, dv))


OUTPROJ_SUB_ROWS = 128


def _outproj_ln_kernel(*refs, k_sizes):
    lhs_refs = refs[:len(k_sizes)]
    w_ref, x_ref, gate_ref, g_ref, b_ref, o_ref = refs[len(k_sizes):]
    def project(rows):
        y = None
        off = 0
        for r, ks in zip(lhs_refs, k_sizes):
            part = jnp.dot(r[rows, :], w_ref[off:off + ks, :], preferred_element_type=F32)
            y = part if y is None else y + part
            off += ks
        return y

    subs = [slice(r0, r0 + OUTPROJ_SUB_ROWS) for r0 in range(0, o_ref.shape[0], OUTPROJ_SUB_ROWS)]
    y_next = project(subs[0])
    for idx, rows in enumerate(subs):
        y = y_next
        if idx + 1 < len(subs):
            y_next = project(subs[idx + 1])
        o_ref[rows, :] = _layer_norm(ALPHA * x_ref[rows, :] + gate_ref[...] * y, g_ref[...], b_ref[...])


def _outproj_ln(lhs_list, w_bf, x2, mod3, gate_slot, ln_g, ln_b, *, seq):
    n, d = x2.shape
    tm = 512
    per_seq = seq // tm
    row = lambda i: (i, 0)
    k_sizes = tuple(a.shape[1] for a in lhs_list)
    k_total = sum(k_sizes)
    vec = pl.BlockSpec((1, d), lambda i: (0, 0))
    est = k_total * d * 2 + 2 * tm * k_total * 2 + 4 * tm * d * 4 + 4 * tm * d * 4
    return pl.pallas_call(
        functools.partial(_outproj_ln_kernel, k_sizes=k_sizes),
        out_shape=jax.ShapeDtypeStruct((n, d), F32),
        grid=(n // tm,),
        in_specs=[pl.BlockSpec((tm, ks), row) for ks in k_sizes] + [
            _resident((k_total, d)),
            pl.BlockSpec((tm, d), row),
            pl.BlockSpec((None, 1, d), lambda i: ((i // per_seq) * 6 + gate_slot, 0, 0)),
            vec, vec],
        out_specs=pl.BlockSpec((tm, d), row),
        compiler_params=_params(("arbitrary",), est),
        name="outproj_ln",
    )(*lhs_list, w_bf, x2, mod3, ln_g.reshape(1, d), ln_b.reshape(1, d))


FFN_HALO = BF16_SUBLANE_PACK


def _ffn_kernel(x_ref, halo_ref, shift_ref, scale_ref, gate_ref, wu_ref, wv_ref, cw_ref, cb_ref, wd_ref,
                lng_ref, lnb_ref, o_ref, h_scr, u_scr, *, per_seq):
    i = pl.program_id(0)
    f = pl.program_id(1)
    tm = x_ref.shape[0]

    @pl.when(f == 0)
    def _():
        sc = 1.0 + scale_ref[...]
        sh = shift_ref[...]
        h_scr[0:FFN_HALO, :] = (halo_ref[...] * sc + sh).astype(BF16)
        h_scr[FFN_HALO:, :] = (x_ref[...] * sc + sh).astype(BF16)
        o_ref[...] = jnp.zeros_like(o_ref)

    u_full = jnp.dot(h_scr[...], wu_ref[...], preferred_element_type=F32)
    keep = jnp.where(i % per_seq == 0, 0.0, 1.0)
    u_scr[0:FFN_HALO, :] = u_full[0:FFN_HALO, :] * keep
    u_scr[FFN_HALO:, :] = u_full[FFN_HALO:, :]
    cw = cw_ref[...]
    u = (cw[2:3, :] * u_scr[FFN_HALO:FFN_HALO + tm, :]
         + cw[1:2, :] * u_scr[FFN_HALO - 1:FFN_HALO - 1 + tm, :]
         + cw[0:1, :] * u_scr[FFN_HALO - 2:FFN_HALO - 2 + tm, :]) + cb_ref[...]
    v = jnp.dot(h_scr[FFN_HALO:, :], wv_ref[...], preferred_element_type=F32)
    act = (_silu(u) * v).astype(BF16)
    o_ref[...] += jnp.dot(act, wd_ref[...], preferred_element_type=F32)

    @pl.when(f == pl.num_programs(1) - 1)
    def _():
        o_ref[...] = _layer_norm(ALPHA * x_ref[...] + gate_ref[...] * o_ref[...], lng_ref[...], lnb_ref[...])


def _ffn(x2, mod3, w_up_bf, conv_w, conv_b, w_down_bf, ln_g, ln_b, *, seq):
    n, d = x2.shape
    tm = 512
    tf = 512
    per_seq = seq // tm
    nf = D_FF // tf
    halo_blocks = tm // FFN_HALO
    slot = lambda s: (lambda i, f: ((i // per_seq) * 6 + s, 0, 0))
    vec = pl.BlockSpec((1, d), lambda i, f: (0, 0))
    est = (4 * tm * d * 4 + 2 * FFN_HALO * d * 4 + 2 * 2 * d * tf * 2 + 2 * tf * d * 2
           + (tm + FFN_HALO) * d * 2 + (tm + FFN_HALO) * tf * 4 + 8 * tm * tf * 4 + 2 * tm * d * 4)
    return pl.pallas_call(
        functools.partial(_ffn_kernel, per_seq=per_seq),
        out_shape=jax.ShapeDtypeStruct((n, d), F32),
        grid=(n // tm, nf),
        in_specs=[pl.BlockSpec((tm, d), lambda i, f: (i, 0)),
                  pl.BlockSpec((FFN_HALO, d), lambda i, f: (jnp.maximum(i * halo_blocks - 1, 0), 0)),
                  pl.BlockSpec((None, 1, d), slot(3)),
                  pl.BlockSpec((None, 1, d), slot(4)),
                  pl.BlockSpec((None, 1, d), slot(5)),
                  pl.BlockSpec((d, tf), lambda i, f: (0, f)),
                  pl.BlockSpec((d, tf), lambda i, f: (0, nf + f)),
                  pl.BlockSpec((CONV_WIDTH, tf), lambda i, f: (0, f)),
                  pl.BlockSpec((1, tf), lambda i, f: (0, f)),
                  pl.BlockSpec((tf, d), lambda i, f: (f, 0)),
                  vec, vec],
        out_specs=pl.BlockSpec((tm, d), lambda i, f: (i, 0)),
        scratch_shapes=[pltpu.VMEM((tm + FFN_HALO, d), BF16), pltpu.VMEM((tm + FFN_HALO, tf), F32)],
        compiler_params=_params(("arbitrary", "arbitrary"), est),
        name="ffn",
    )(x2, x2, mod3, mod3, mod3, w_up_bf, w_up_bf, conv_w, conv_b.reshape(1, D_FF), w_down_bf,
      ln_g.reshape(1, d), ln_b.reshape(1, d))


def _inproj_odd_kernel(x_ref, shift_ref, scale_ref, w_ref, wgb_ref, bg_ref,
                       q_ref, k_ref, v_ref, g_ref, lf_ref, h_scr):
    h_scr[...] = (x_ref[...] * (1.0 + scale_ref[...]) + shift_ref[...]).astype(BF16)

    def proj(c0, width=PROJ_COLS):
        return jnp.dot(h_scr[...], w_ref[:, c0:c0 + width], preferred_element_type=F32)

    low = proj(2 * C_KEY_DIM + 2 * C_VALUE_DIM, RANK_PAD).astype(BF16)

    def log_decay(c0, width):
        sl = slice(c0, c0 + width)
        gk = jnp.dot(low, wgb_ref[:, sl], preferred_element_type=F32) + bg_ref[:, sl]
        lf_ref[:, sl] = jax.nn.log_sigmoid(gk) * (1.0 / GATE_NORMALIZER)

    piece = C_KEY_DIM // 4
    for c0 in range(0, C_KEY_DIM, PROJ_COLS):
        sl = slice(c0, c0 + PROJ_COLS)
        q_ref[:, sl] = proj(c0) * (C_DK ** -0.5)
        log_decay(2 * c0 // PROJ_COLS * piece, piece)
        k_ref[:, sl] = proj(C_KEY_DIM + c0)
        log_decay((2 * c0 // PROJ_COLS + 1) * piece, piece)
    for c0 in range(0, C_VALUE_DIM, PROJ_COLS):
        sl = slice(c0, c0 + PROJ_COLS)
        v_ref[:, sl] = proj(2 * C_KEY_DIM + c0).astype(BF16)
        g_ref[:, sl] = proj(2 * C_KEY_DIM + C_VALUE_DIM + c0)


def _inproj_odd(x2, mod3, w_bf, w_gk_b_bf, b_gk, *, seq):
    n, d = x2.shape
    tm = 256
    per_seq = seq // tm
    width = w_bf.shape[1]
    row = lambda i: (i, 0)
    out_shapes = (
        jax.ShapeDtypeStruct((n, C_KEY_DIM), F32), jax.ShapeDtypeStruct((n, C_KEY_DIM), F32),
        jax.ShapeDtypeStruct((n, C_VALUE_DIM), BF16), jax.ShapeDtypeStruct((n, C_VALUE_DIM), F32),
        jax.ShapeDtypeStruct((n, C_KEY_DIM), F32))
    out_bytes = sum(tm * s.shape[1] * s.dtype.itemsize for s in out_shapes)
    est = d * width * 2 + 2 * tm * d * 4 + 2 * out_bytes + tm * d * 2 + 6 * tm * PROJ_COLS * 4
    return pl.pallas_call(
        _inproj_odd_kernel,
        out_shape=out_shapes,
        grid=(n // tm,),
        in_specs=[pl.BlockSpec((tm, d), row),
                  pl.BlockSpec((None, 1, d), lambda i: ((i // per_seq) * 6 + 0, 0, 0)),
                  pl.BlockSpec((None, 1, d), lambda i: ((i // per_seq) * 6 + 1, 0, 0)),
                  _resident((d, width)),
                  _resident((RANK_PAD, C_KEY_DIM)),
                  pl.BlockSpec((1, C_KEY_DIM), lambda i: (0, 0))],
        out_specs=tuple(pl.BlockSpec((tm, s.shape[1]), row) for s in out_shapes),
        scratch_shapes=[pltpu.VMEM((tm, d), BF16)],
        compiler_params=_params(("arbitrary",), est),
        name="inproj_odd",
    )(x2, mod3, mod3, w_bf, w_gk_b_bf, b_gk.reshape(1, C_KEY_DIM))


def kernel(x, c, positions,
           ada_w0, ada_b0, mix_w_in0, mix_w_out0, attn_sinks0, hgrn_lb_logits, hgrn_norm_w0,
           ln_mix_g0, ln_mix_b0, ffn_w_up0, ffn_conv_w0, ffn_conv_b0, ffn_w_down0, ln_ffn_g0, ln_ffn_b0,
           ada_w1, ada_b1, mix_w_in1, gla_w_gk_a1, gla_w_gk_b1, gla_b_gk1, gla_norm_w1, mix_w_out1,
           ln_mix_g1, ln_mix_b1, ffn_w_up1, ffn_conv_w1, ffn_conv_b1, ffn_w_down1, ln_ffn_g1, ln_ffn_b1):
    batch, seq, d = x.shape
    n = batch * seq
    x2 = x.reshape(n, d)
    c_pad = jnp.pad(c, ((0, 8 - batch), (0, 0)))

    def modulation(ada_w, ada_b):
        mod = _ada_modulation(c_pad, ada_w, ada_b)[:batch]
        return mod.reshape(batch * 6, 1, d)

    mod3 = modulation(ada_w0, ada_b0)
    cos, sin_signed = _rope_tables(positions.reshape(n, 1))
    q_a, k_a, v_a, q_b, k_b, lf_b, i_b, g_b = _inproj_even(
        x2, mod3, mix_w_in0.astype(BF16), cos, sin_signed, hgrn_lb_logits, seq=seq, layer=0)
    o_a = _swa_attention(q_a, k_a, v_a, attn_sinks0, batch=batch, seq=seq)
    o_b = _gla(q_b, k_b, i_b, lf_b, g_b, hgrn_norm_w0, batch=batch, seq=seq,
               heads=B_HEADS, dk=B_HEAD_DIM, dv=B_HEAD_DIM, heads_per_step=2, name="hgrn2")
    x2 = _outproj_ln([o_a, o_b], mix_w_out0.astype(BF16), x2, mod3, 2, ln_mix_g0, ln_mix_b0, seq=seq)
    x2 = _ffn(x2, mod3, ffn_w_up0.astype(BF16), ffn_conv_w0, ffn_conv_b0, ffn_w_down0.astype(BF16),
              ln_ffn_g0, ln_ffn_b0, seq=seq)

    mod3 = modulation(ada_w1, ada_b1)
    w_in1 = jnp.concatenate(
        [mix_w_in1, jnp.pad(gla_w_gk_a1, ((0, 0), (0, RANK_PAD - GATE_RANK)))], axis=1).astype(BF16)
    w_gk_b = jnp.pad(gla_w_gk_b1, ((0, RANK_PAD - GATE_RANK), (0, 0))).astype(BF16)
    q_c, k_c, v_c, g_c, lf_c = _inproj_odd(x2, mod3, w_in1, w_gk_b, gla_b_gk1, seq=seq)
    o_c = _gla(q_c, k_c, v_c, lf_c, g_c, gla_norm_w1, batch=batch, seq=seq,
               heads=C_HEADS, dk=C_DK, dv=C_DV, heads_per_step=1, name="gla")
    x2 = _outproj_ln([o_c], mix_w_out1.astype(BF16), x2, mod3, 2, ln_mix_g1, ln_mix_b1, seq=seq)
    x2 = _ffn(x2, mod3, ffn_w_up1.astype(BF16), ffn_conv_w1, ffn_conv_b1, ffn_w_down1.astype(BF16),
              ln_ffn_g1, ln_ffn_b1, seq=seq)
    return x2.reshape(batch, seq, d)
```

```python
import functools
import math

import jax
import jax.numpy as jnp
from jax import lax
from jax.experimental import pallas as pl
from jax.experimental.pallas import tpu as pltpu

F32 = jnp.float32
BF16 = jnp.bfloat16

D_MODEL = 2048
DEPTH = 2
HEAD_DIM = 64
A_Q_HEADS = 16
A_KV_HEADS = 4
A_Q_WIDTH = A_Q_HEADS * HEAD_DIM
A_KV_WIDTH = A_KV_HEADS * HEAD_DIM
WINDOW = 128
ROPE_THETA = 10000.0
B_HEADS = 8
B_HEAD_DIM = 128
B_WIDTH = B_HEADS * B_HEAD_DIM
EVEN_IN_WIDTH = A_Q_WIDTH + 2 * A_KV_WIDTH + 4 * B_WIDTH
C_HEADS = 4
C_KEY_DIM = D_MODEL // 2
C_VALUE_DIM = D_MODEL
C_DK = C_KEY_DIM // C_HEADS
C_DV = C_VALUE_DIM // C_HEADS
GATE_RANK = 16
GATE_NORMALIZER = 16.0
CHUNK = 64
D_FF = 5632
CONV_WIDTH = 3
LN_EPS = 1e-5
RMS_EPS = 1e-6
ALPHA = (2.0 * DEPTH) ** 0.25
NEG_INF = -1e30

V7X_LANES = 128
V7X_VMEM_BYTES = 64 * 1024 * 1024
V7X_VMEM_CEILING = 56 * 1024 * 1024
BF16_SUBLANE_PACK = 16

RANK_PAD = V7X_LANES


def _params(semantics, vmem_estimate):
    limit = min(V7X_VMEM_CEILING, max(16 * 1024 * 1024, int(vmem_estimate * 1.25)))
    return pltpu.CompilerParams(dimension_semantics=semantics, vmem_limit_bytes=limit)


def _resident(shape):
    return pl.BlockSpec(shape, lambda *_: (0,) * len(shape), pipeline_mode=pl.Buffered(1))


def _silu(x):
    return x * jax.nn.sigmoid(x)


class _SideCasts:
    def __init__(self, weights, grid):
        self.groups = [w if isinstance(w, (tuple, list)) else (w,) for w in weights]
        self.sources = [m for group in self.groups for m in group]
        self.group_sizes = tuple(len(group) for group in self.groups)
        steps = math.prod(grid)
        strides = [math.prod(grid[i + 1:]) for i in range(len(grid))]
        self.in_specs, self.out_specs, self.out_shapes = [], [], []
        self.vmem_bytes = 0
        for group in self.groups:
            rows = group[0].shape[0]
            cols = sum(m.shape[1] for m in group)
            share = 1
            while (rows * share) % (steps * BF16_SUBLANE_PACK):
                share *= 2
            block_rows = rows * share // steps

            def index(*ids, share=share):
                return (sum(i * s for i, s in zip(ids, strides)) // share, 0)

            self.in_specs += [pl.BlockSpec((block_rows, m.shape[1]), index) for m in group]
            self.out_specs.append(pl.BlockSpec((block_rows, cols), index))
            self.out_shapes.append(jax.ShapeDtypeStruct((rows, cols), BF16))
            self.vmem_bytes += 2 * block_rows * cols * (4 + 2)


def _copy_side_casts(src_refs, dst_refs, group_sizes):
    src_refs = list(src_refs)
    for dst, size in zip(dst_refs, group_sizes):
        col = 0
        for src in src_refs[:size]:
            dst[:, col:col + src.shape[1]] = src[...].astype(BF16)
            col += src.shape[1]
        src_refs = src_refs[size:]


def _layer_norm(z, g, b):
    mu = jnp.mean(z, axis=-1, keepdims=True)
    zc = z - mu
    var = jnp.mean(zc * zc, axis=-1, keepdims=True)
    return zc * lax.rsqrt(var + LN_EPS) * g + b


def _ada_kernel(c_ref, w_ref, b_ref, o_ref):
    a = _silu(c_ref[...]).astype(BF16)
    o_ref[...] = jnp.dot(a, w_ref[...].astype(BF16), preferred_element_type=F32) + b_ref[...]


def _ada_modulation(c_pad, ada_w, ada_b):
    rows, d = c_pad.shape
    n = ada_w.shape[1]
    tn = 1024
    est = 2 * d * tn * 4 + d * tn * 2 + 4 * rows * n
    return pl.pallas_call(
        _ada_kernel,
        out_shape=jax.ShapeDtypeStruct((rows, n), F32),
        grid=(n // tn,),
        in_specs=[pl.BlockSpec((rows, d), lambda j: (0, 0)),
                  pl.BlockSpec((d, tn), lambda j: (0, j)),
                  pl.BlockSpec((1, tn), lambda j: (0, j))],
        out_specs=pl.BlockSpec((rows, tn), lambda j: (0, j)),
        compiler_params=_params(("arbitrary",), est),
        name="ada_modulation",
    )(c_pad, ada_w, ada_b.reshape(1, n))


def _rope_table_kernel(pos_ref, invf_ref, sign_ref, cos_ref, sin_ref):
    ang = pos_ref[...].astype(F32) * invf_ref[...]
    cos_ref[...] = jnp.cos(ang)
    sin_ref[...] = jnp.sin(ang) * sign_ref[...]


def _rope_tables(pos_col):
    n = pos_col.shape[0]
    half = HEAD_DIM // 2
    lane = jnp.arange(V7X_LANES)
    inv_freq = ROPE_THETA ** (-jnp.arange(half, dtype=F32) / half)
    invf = inv_freq[lane % half].reshape(1, V7X_LANES)
    sign = jnp.where((lane % HEAD_DIM) < half, -1.0, 1.0).astype(F32).reshape(1, V7X_LANES)
    tm = 2048
    est = 2 * (tm * V7X_LANES * 4) * 3
    return pl.pallas_call(
        _rope_table_kernel,
        out_shape=(jax.ShapeDtypeStruct((n, V7X_LANES), F32),) * 2,
        grid=(n // tm,),
        in_specs=[pl.BlockSpec((tm, 1), lambda i: (i, 0)),
                  pl.BlockSpec((1, V7X_LANES), lambda i: (0, 0)),
                  pl.BlockSpec((1, V7X_LANES), lambda i: (0, 0))],
        out_specs=(pl.BlockSpec((tm, V7X_LANES), lambda i: (i, 0)),) * 2,
        compiler_params=_params(("arbitrary",), est),
        name="rope_tables",
    )(pos_col, invf, sign)


def _rope(a, cos, sin_signed, first_half):
    outs = []
    for s in range(a.shape[1] // V7X_LANES):
        blk = a[:, s * V7X_LANES:(s + 1) * V7X_LANES]
        partner = jnp.where(first_half,
                            pltpu.roll(blk, V7X_LANES - HEAD_DIM // 2, 1),
                            pltpu.roll(blk, HEAD_DIM // 2, 1))
        outs.append(blk * cos + partner * sin_signed)
    return jnp.concatenate(outs, axis=1) if len(outs) > 1 else outs[0]


PROJ_COLS = 512


def _inproj_even_kernel(x_ref, shift_ref, scale_ref, w_ref, cos_ref, sin_ref, lbl_ref,
                        qa_ref, ka_ref, va_ref, qb_ref, kb_ref, lf_ref, ib_ref, gb_ref, h_scr, *, layer):
    tm = x_ref.shape[0]
    h_scr[...] = (x_ref[...] * (1.0 + scale_ref[...]) + shift_ref[...]).astype(BF16)

    def proj(c0, width=PROJ_COLS):
        return jnp.dot(h_scr[...], w_ref[:, c0:c0 + width], preferred_element_type=F32)

    cos = cos_ref[...]
    sin_signed = sin_ref[...]
    lane = lax.broadcasted_iota(jnp.int32, (tm, V7X_LANES), 1)
    first_half = (lane % HEAD_DIM) < (HEAD_DIM // 2)

    for c0 in range(0, A_Q_WIDTH, PROJ_COLS):
        qa_ref[:, c0:c0 + PROJ_COLS] = (_rope(proj(c0), cos, sin_signed, first_half)
                                        * (HEAD_DIM ** -0.5)).astype(BF16)
    kv = proj(A_Q_WIDTH, 2 * A_KV_WIDTH)
    ka_ref[...] = _rope(kv[:, :A_KV_WIDTH], cos, sin_signed, first_half).astype(BF16)
    va_ref[...] = kv[:, A_KV_WIDTH:].astype(BF16)

    base = A_Q_WIDTH + 2 * A_KV_WIDTH
    lg = lbl_ref[...]
    e = jnp.exp(lg - jnp.max(lg, axis=0, keepdims=True))
    sm = e / jnp.sum(e, axis=0, keepdims=True)
    lb = jnp.sum(sm[0:layer + 1], axis=0, keepdims=True)
    for c0 in range(0, B_WIDTH, PROJ_COLS):
        sl = slice(c0, c0 + PROJ_COLS)
        qb_ref[:, sl] = proj(base + c0)
        lbc = lb[:, sl]
        fg = lbc + (1.0 - lbc) * jax.nn.sigmoid(proj(base + B_WIDTH + c0))
        kb_ref[:, sl] = 1.0 - fg
        lf_ref[:, sl] = jnp.log(fg)
        ib_ref[:, sl] = proj(base + 2 * B_WIDTH + c0).astype(BF16)
        gb_ref[:, sl] = proj(base + 3 * B_WIDTH + c0)


def _inproj_even(x2, mod3, w_bf, cos, sin_signed, lb_logits, *, seq, layer):
    n, d = x2.shape
    tm = 256
    per_seq = seq // tm
    width = w_bf.shape[1]
    row = lambda i: (i, 0)
    out_shapes = (
        jax.ShapeDtypeStruct((n, A_Q_WIDTH), BF16), jax.ShapeDtypeStruct((n, A_KV_WIDTH), BF16),
        jax.ShapeDtypeStruct((n, A_KV_WIDTH), BF16), jax.ShapeDtypeStruct((n, B_WIDTH), F32),
        jax.ShapeDtypeStruct((n, B_WIDTH), F32), jax.ShapeDtypeStruct((n, B_WIDTH), F32),
        jax.ShapeDtypeStruct((n, B_WIDTH), BF16), jax.ShapeDtypeStruct((n, B_WIDTH), F32))
    out_bytes = sum(tm * s.shape[1] * s.dtype.itemsize for s in out_shapes)
    est = d * width * 2 + 2 * tm * d * 4 + 2 * out_bytes + tm * d * 2 + 6 * tm * PROJ_COLS * 4
    return pl.pallas_call(
        functools.partial(_inproj_even_kernel, layer=layer),
        out_shape=out_shapes,
        grid=(n // tm,),
        in_specs=[pl.BlockSpec((tm, d), row),
                  pl.BlockSpec((None, 1, d), lambda i: ((i // per_seq) * 6 + 0, 0, 0)),
                  pl.BlockSpec((None, 1, d), lambda i: ((i // per_seq) * 6 + 1, 0, 0)),
                  _resident((d, width)),
                  pl.BlockSpec((tm, V7X_LANES), row),
                  pl.BlockSpec((tm, V7X_LANES), row),
                  pl.BlockSpec(lb_logits.shape, lambda i: (0, 0))],
        out_specs=tuple(pl.BlockSpec((tm, s.shape[1]), row) for s in out_shapes),
        scratch_shapes=[pltpu.VMEM((tm, d), BF16)],
        compiler_params=_params(("arbitrary",), est),
        name="inproj_even",
    )(x2, mod3, mod3, w_bf, cos, sin_signed, lb_logits)


def _swa_kernel(*refs, cast_groups):
    sink_ref, q_ref, kp_ref, kc_ref, vp_ref, vc_ref = refs[:6]
    n_src = sum(cast_groups)
    o_ref = refs[6 + n_src]
    _copy_side_casts(refs[6:6 + n_src], refs[7 + n_src:], cast_groups)
    first_step = pl.program_id(1) == 0
    grp = A_Q_HEADS // A_KV_HEADS
    assert HEAD_DIM * 2 == V7X_LANES and grp == 4
    r = lax.broadcasted_iota(jnp.int32, (WINDOW, 2 * WINDOW), 0)
    c = lax.broadcasted_iota(jnp.int32, (WINDOW, 2 * WINDOW), 1)
    rel = r + WINDOW - c
    band = (rel >= 0) & (rel < WINDOW)
    band_first = band & ((c >= WINDOW) | jnp.logical_not(first_step))
    lane = lax.broadcasted_iota(jnp.int32, (1, V7X_LANES), 1)
    half_mask = [jnp.where(lane < HEAD_DIM, 1.0, 0.0).astype(BF16),
                 jnp.where(lane >= HEAD_DIM, 1.0, 0.0).astype(BF16)]
    low_half = lax.broadcasted_iota(jnp.int32, (WINDOW, V7X_LANES), 1) < HEAD_DIM

    def frame(prev_ref, cur_ref, sub, kv_tile):
        own = cur_ref[sub * WINDOW:(sub + 1) * WINDOW, kv_tile]
        before = prev_ref[:, kv_tile] if sub == 0 else cur_ref[(sub - 1) * WINDOW:sub * WINDOW, kv_tile]
        return jnp.concatenate([before, own], axis=0)

    def scores(sub, g):
        kv_tile = slice((g // 2) * V7X_LANES, (g // 2 + 1) * V7X_LANES)
        kv_half = g % 2
        q_rows = []
        for e in range(grp):
            q_tile = (2 * g + e // 2) * V7X_LANES
            qt = q_ref[sub * WINDOW:(sub + 1) * WINDOW, q_tile:q_tile + V7X_LANES]
            if e % 2 != kv_half:
                qt = pltpu.roll(qt, HEAD_DIM, 1)
            q_rows.append(qt * half_mask[kv_half])
        qs = jnp.concatenate(q_rows, axis=0)
        return lax.dot_general(qs, frame(kp_ref, kc_ref, sub, kv_tile), (((1,), (1,)), ((), ())),
                               preferred_element_type=F32)

    units = [(sub, g) for sub in range(q_ref.shape[0] // WINDOW) for g in range(A_KV_HEADS)]
    s_next = scores(*units[0])
    for idx, (sub, g) in enumerate(units):
        kv_tile = slice((g // 2) * V7X_LANES, (g // 2 + 1) * V7X_LANES)
        kv_half = g % 2
        qrows = slice(sub * WINDOW, (sub + 1) * WINDOW)
        valid = band_first if sub == 0 else band
        s_all = s_next
        if idx + 1 < len(units):
            s_next = scores(*units[idx + 1])
        p_rows = []
        denoms = []
        for e in range(grp):
            s = jnp.where(valid, s_all[e * WINDOW:(e + 1) * WINDOW], NEG_INF)
            sink = sink_ref[grp * g + e]
            m = jnp.maximum(jnp.max(s, axis=-1, keepdims=True), sink)
            p = jnp.exp(s - m)
            denoms.append(jnp.sum(p, axis=-1, keepdims=True) + jnp.exp(sink - m))
            p_rows.append(p.astype(BF16))
        o_all = jnp.dot(jnp.concatenate(p_rows, axis=0), frame(vp_ref, vc_ref, sub, kv_tile),
                        preferred_element_type=F32)
        for u in range(grp // 2):
            o_even = o_all[(2 * u) * WINDOW:(2 * u + 1) * WINDOW] / denoms[2 * u]
            o_odd = o_all[(2 * u + 1) * WINDOW:(2 * u + 2) * WINDOW] / denoms[2 * u + 1]
            if kv_half == 0:
                tile = jnp.where(low_half, o_even, pltpu.roll(o_odd, HEAD_DIM, 1))
            else:
                tile = jnp.where(low_half, pltpu.roll(o_even, HEAD_DIM, 1), o_odd)
            out_tile = (2 * g + u) * V7X_LANES
            o_ref[qrows, out_tile:out_tile + V7X_LANES] = tile.astype(BF16)


SWA_BLOCKS_PER_STEP = 1


def _swa_attention(q_a, k_a, v_a, sinks, cast_weights, *, batch, seq):
    n = q_a.shape[0]
    tq = SWA_BLOCKS_PER_STEP * WINDOW
    nb = seq // tq
    grid = (batch, nb)
    casts = _SideCasts(cast_weights, grid)
    cur = lambda b, i: (b * nb + i, 0)
    prev = lambda b, i: (jnp.maximum((b * nb + i) * SWA_BLOCKS_PER_STEP - 1, 0), 0)
    est = (2 * (2 * tq * A_Q_WIDTH * 2 + 2 * (tq + WINDOW) * A_KV_WIDTH * 2) + 24 * WINDOW * 2 * WINDOW * 4
           + casts.vmem_bytes)
    outs = pl.pallas_call(
        functools.partial(_swa_kernel, cast_groups=casts.group_sizes),
        out_shape=[jax.ShapeDtypeStruct((n, A_Q_WIDTH), BF16)] + casts.out_shapes,
        grid=grid,
        in_specs=[pl.BlockSpec(memory_space=pltpu.SMEM),
                  pl.BlockSpec((tq, A_Q_WIDTH), cur),
                  pl.BlockSpec((WINDOW, A_KV_WIDTH), prev),
                  pl.BlockSpec((tq, A_KV_WIDTH), cur),
                  pl.BlockSpec((WINDOW, A_KV_WIDTH), prev),
                  pl.BlockSpec((tq, A_KV_WIDTH), cur)] + casts.in_specs,
        out_specs=[pl.BlockSpec((tq, A_Q_WIDTH), cur)] + casts.out_specs,
        compiler_params=_params(("arbitrary", "arbitrary"), est),
        name="swa_attention",
    )(sinks, q_a, k_a, k_a, v_a, v_a, *casts.sources)
    return outs[0], outs[1:]


def _split3(x):
    hi = x.astype(BF16)
    r1 = x - hi.astype(F32)
    mid = r1.astype(BF16)
    lo = (r1 - mid.astype(F32)).astype(BF16)
    return hi, mid, lo


GLA_GROUP = 256


def _gla_kernel(*refs, dk, dv, heads_per_step, cast_groups):
    q_ref, k_ref, v_ref, lf_ref, g_ref, nw_ref = refs[:6]
    n_src = sum(cast_groups)
    o_ref = refs[6 + n_src]
    s_scr = refs[-1]
    _copy_side_casts(refs[6:6 + n_src], refs[7 + n_src:-1], cast_groups)

    @pl.when(pl.program_id(2) == 0)
    def _():
        s_scr[...] = jnp.zeros_like(s_scr)

    tb = q_ref.shape[0]
    assert GLA_GROUP == 4 * CHUNK
    row = lax.broadcasted_iota(jnp.int32, (GLA_GROUP, GLA_GROUP), 0)
    col = lax.broadcasted_iota(jnp.int32, (GLA_GROUP, GLA_GROUP), 1)
    rc = row // CHUNK
    cc = col // CHUNK
    same_chunk = (rc == cc) & (row >= col)
    next_chunk = (rc == cc + 1) & (rc % 2 == 1)
    far_chunk = (rc >= 2) & (cc <= 1)
    tril = jnp.where(same_chunk, 1.0, 0.0).astype(BF16)
    nw = nw_ref[...]
    one = jnp.ones((1, dk), F32)
    nt = (((1,), (1,)), ((), ()))

    def by_chunk(vecs):
        return jnp.concatenate([jnp.broadcast_to(v, (CHUNK, dk)) for v in vecs], axis=0)

    def prepare(hh, gi):
        rows = slice(gi * GLA_GROUP, (gi + 1) * GLA_GROUP)
        kcols = slice(hh * dk, (hh + 1) * dk)
        vcols = slice(hh * dv, (hh + 1) * dv)
        hi, mid, lo = _split3(lf_ref[rows, kcols])
        bb = jnp.dot(tril, jnp.concatenate([hi, mid, lo], axis=1), preferred_element_type=F32)
        b = bb[:, :dk] + bb[:, dk:2 * dk] + bb[:, 2 * dk:]
        l0, l1, l2, l3 = [b[(c + 1) * CHUNK - 1:(c + 1) * CHUNK, :] for c in range(4)]
        b_mid = by_chunk([b[c * CHUNK + CHUNK // 2 - 1:c * CHUNK + CHUNK // 2, :] for c in range(4)])
        q = q_ref[rows, kcols]
        k = k_ref[rows, kcols]
        v = v_ref[rows, vcols]
        q_loc = q * jnp.exp(b)
        k_loc = k * jnp.exp(by_chunk([l0, l1, l2, l3]) - b)
        q_i = (q * jnp.exp(b - b_mid)).astype(BF16)
        k_i = (k * jnp.exp(b_mid - b)).astype(BF16)
        q_far = (q_loc * by_chunk([one, one, one, jnp.exp(l2)])).astype(BF16)
        k_far = (k_loc * by_chunk([jnp.exp(l1), one, one, one])).astype(BF16)
        q_grp = (q_loc * by_chunk([one, jnp.exp(l0), jnp.exp(l0 + l1), jnp.exp(l0 + l1 + l2)])).astype(BF16)
        k_grp = (k_loc * by_chunk([jnp.exp(l1 + l2 + l3), jnp.exp(l2 + l3), jnp.exp(l3), one])).astype(BF16)
        a_same = lax.dot_general(q_i, k_i, nt, preferred_element_type=F32)
        a_next = lax.dot_general(q_loc.astype(BF16), k_loc.astype(BF16), nt, preferred_element_type=F32)
        a_far = lax.dot_general(q_far, k_far, nt, preferred_element_type=F32)
        a = jnp.where(same_chunk, a_same, jnp.where(next_chunk, a_next, jnp.where(far_chunk, a_far, 0.0)))
        o_intra = jnp.dot(a.astype(BF16), v, preferred_element_type=F32)
        kv = lax.dot_general(k_grp, v, (((0,), (0,)), ((), ())), preferred_element_type=F32)
        decay = jnp.exp(jnp.transpose(jnp.broadcast_to(l0 + l1 + l2 + l3, (V7X_LANES, dk))))
        return rows, vcols, o_intra, q_grp, kv, decay

    def finish(state, prepared):
        rows, vcols, o_intra, q_grp, kv, decay = prepared
        o = o_intra + jnp.dot(q_grp, state.astype(BF16), preferred_element_type=F32)
        o = o * lax.rsqrt(jnp.mean(o * o, axis=-1, keepdims=True) + RMS_EPS)
        o_ref[rows, vcols] = ((o * nw) * _silu(g_ref[rows, vcols])).astype(BF16)
        return state * jnp.tile(decay, (1, dv // V7X_LANES)) + kv

    units = [(hh, gi) for gi in range(tb // GLA_GROUP) for hh in range(heads_per_step)]
    states = [s_scr[hh] for hh in range(heads_per_step)]
    pending = prepare(*units[0])
    for idx, (hh, gi) in enumerate(units):
        upcoming = prepare(*units[idx + 1]) if idx + 1 < len(units) else None
        states[hh] = finish(states[hh], pending)
        pending = upcoming
    for hh in range(heads_per_step):
        s_scr[hh] = states[hh]


def _gla(q, k, v, log_f, g, norm_w, cast_weights, *, batch, seq, heads, dk, dv, heads_per_step, name):
    n = q.shape[0]
    tb = 1024
    nt = seq // tb
    hp = heads_per_step
    grid = (batch, heads // hp, nt)
    casts = _SideCasts(cast_weights, grid)
    idx = lambda b, h, t: (b * nt + t, h)
    est = (2 * tb * hp * (3 * dk * 4 + dv * 2 + dv * 4 + dv * 2) + hp * dk * dv * 4 + 4 * dk * dv * 4
           + 16 * GLA_GROUP * max(3 * dk, dv) * 4 + casts.vmem_bytes)
    outs = pl.pallas_call(
        functools.partial(_gla_kernel, dk=dk, dv=dv, heads_per_step=hp, cast_groups=casts.group_sizes),
        out_shape=[jax.ShapeDtypeStruct((n, heads * dv), BF16)] + casts.out_shapes,
        grid=grid,
        in_specs=[pl.BlockSpec((tb, hp * dk), idx), pl.BlockSpec((tb, hp * dk), idx),
                  pl.BlockSpec((tb, hp * dv), idx), pl.BlockSpec((tb, hp * dk), idx),
                  pl.BlockSpec((tb, hp * dv), idx),
                  pl.BlockSpec((1, dv), lambda b, h, t: (0, 0))] + casts.in_specs,
        out_specs=[pl.BlockSpec((tb, hp * dv), idx)] + casts.out_specs,
        scratch_shapes=[pltpu.VMEM((hp, dk, dv), F32)],
        compiler_params=_params(("arbitrary", "arbitrary", "arbitrary"), est),
        name=name,
    )(q, k, v, log_f, g, norm_w.reshape(1, dv), *casts.sources)
    return outs[0], outs[1:]


OUTPROJ_SUB_ROWS = 128


def _outproj_ln_kernel(*refs, k_sizes):
    lhs_refs = refs[:len(k_sizes)]
    w_ref, x_ref, gate_ref, g_ref, b_ref, o_ref = refs[len(k_sizes):]
    def project(rows):
        y = None
        off = 0
        for r, ks in zip(lhs_refs, k_sizes):
            part = jnp.dot(r[rows, :], w_ref[off:off + ks, :], preferred_element_type=F32)
            y = part if y is None else y + part
            off += ks
        return y

    subs = [slice(r0, r0 + OUTPROJ_SUB_ROWS) for r0 in range(0, o_ref.shape[0], OUTPROJ_SUB_ROWS)]
    y_next = project(subs[0])
    for idx, rows in enumerate(subs):
        y = y_next
        if idx + 1 < len(subs):
            y_next = project(subs[idx + 1])
        o_ref[rows, :] = _layer_norm(ALPHA * x_ref[rows, :] + gate_ref[...] * y, g_ref[...], b_ref[...])


def _outproj_ln(lhs_list, w_bf, x2, mod3, gate_slot, ln_g, ln_b, *, seq):
    n, d = x2.shape
    tm = 512
    per_seq = seq // tm
    row = lambda i: (i, 0)
    k_sizes = tuple(a.shape[1] for a in lhs_list)
    k_total = sum(k_sizes)
    vec = pl.BlockSpec((1, d), lambda i: (0, 0))
    est = k_total * d * 2 + 2 * tm * k_total * 2 + 4 * tm * d * 4 + 4 * tm * d * 4
    return pl.pallas_call(
        functools.partial(_outproj_ln_kernel, k_sizes=k_sizes),
        out_shape=jax.ShapeDtypeStruct((n, d), F32),
        grid=(n // tm,),
        in_specs=[pl.BlockSpec((tm, ks), row) for ks in k_sizes] + [
            _resident((k_total, d)),
            pl.BlockSpec((tm, d), row),
            pl.BlockSpec((None, 1, d), lambda i: ((i // per_seq) * 6 + gate_slot, 0, 0)),
            vec, vec],
        out_specs=pl.BlockSpec((tm, d), row),
        compiler_params=_params(("arbitrary",), est),
        name="outproj_ln",
    )(*lhs_list, w_bf, x2, mod3, ln_g.reshape(1, d), ln_b.reshape(1, d))


FFN_HALO = BF16_SUBLANE_PACK


def _ffn_kernel(x_ref, halo_ref, shift_ref, scale_ref, gate_ref, wu_ref, wv_ref, cw_ref, cb_ref, wd_ref,
                lng_ref, lnb_ref, o_ref, h_scr, u_scr, *, per_seq):
    i = pl.program_id(0)
    f = pl.program_id(1)
    tm = x_ref.shape[0]

    @pl.when(f == 0)
    def _():
        sc = 1.0 + scale_ref[...]
        sh = shift_ref[...]
        h_scr[0:FFN_HALO, :] = (halo_ref[...] * sc + sh).astype(BF16)
        h_scr[FFN_HALO:, :] = (x_ref[...] * sc + sh).astype(BF16)
        o_ref[...] = jnp.zeros_like(o_ref)

    u_full = jnp.dot(h_scr[...], wu_ref[...], preferred_element_type=F32)
    keep = jnp.where(i % per_seq == 0, 0.0, 1.0)
    u_scr[0:FFN_HALO, :] = u_full[0:FFN_HALO, :] * keep
    u_scr[FFN_HALO:, :] = u_full[FFN_HALO:, :]
    cw = cw_ref[...]
    u = (cw[2:3, :] * u_scr[FFN_HALO:FFN_HALO + tm, :]
         + cw[1:2, :] * u_scr[FFN_HALO - 1:FFN_HALO - 1 + tm, :]
         + cw[0:1, :] * u_scr[FFN_HALO - 2:FFN_HALO - 2 + tm, :]) + cb_ref[...]
    v = jnp.dot(h_scr[FFN_HALO:, :], wv_ref[...], preferred_element_type=F32)
    act = (_silu(u) * v).astype(BF16)
    o_ref[...] += jnp.dot(act, wd_ref[...], preferred_element_type=F32)

    @pl.when(f == pl.num_programs(1) - 1)
    def _():
        o_ref[...] = _layer_norm(ALPHA * x_ref[...] + gate_ref[...] * o_ref[...], lng_ref[...], lnb_ref[...])


def _ffn(x2, mod3, w_up_bf, conv_w, conv_b, w_down_bf, ln_g, ln_b, *, seq):
    n, d = x2.shape
    tm = 512
    tf = 512
    per_seq = seq // tm
    nf = D_FF // tf
    halo_blocks = tm // FFN_HALO
    slot = lambda s: (lambda i, f: ((i // per_seq) * 6 + s, 0, 0))
    vec = pl.BlockSpec((1, d), lambda i, f: (0, 0))
    est = (4 * tm * d * 4 + 2 * FFN_HALO * d * 4 + 2 * 2 * d * tf * 2 + 2 * tf * d * 2
           + (tm + FFN_HALO) * d * 2 + (tm + FFN_HALO) * tf * 4 + 8 * tm * tf * 4 + 2 * tm * d * 4)
    return pl.pallas_call(
        functools.partial(_ffn_kernel, per_seq=per_seq),
        out_shape=jax.ShapeDtypeStruct((n, d), F32),
        grid=(n // tm, nf),
        in_specs=[pl.BlockSpec((tm, d), lambda i, f: (i, 0)),
                  pl.BlockSpec((FFN_HALO, d), lambda i, f: (jnp.maximum(i * halo_blocks - 1, 0), 0)),
                  pl.BlockSpec((None, 1, d), slot(3)),
                  pl.BlockSpec((None, 1, d), slot(4)),
                  pl.BlockSpec((None, 1, d), slot(5)),
                  pl.BlockSpec((d, tf), lambda i, f: (0, f)),
                  pl.BlockSpec((d, tf), lambda i, f: (0, nf + f)),
                  pl.BlockSpec((CONV_WIDTH, tf), lambda i, f: (0, f)),
                  pl.BlockSpec((1, tf), lambda i, f: (0, f)),
                  pl.BlockSpec((tf, d), lambda i, f: (f, 0)),
                  vec, vec],
        out_specs=pl.BlockSpec((tm, d), lambda i, f: (i, 0)),
        scratch_shapes=[pltpu.VMEM((tm + FFN_HALO, d), BF16), pltpu.VMEM((tm + FFN_HALO, tf), F32)],
        compiler_params=_params(("arbitrary", "arbitrary"), est),
        name="ffn",
    )(x2, x2, mod3, mod3, mod3, w_up_bf, w_up_bf, conv_w, conv_b.reshape(1, D_FF), w_down_bf,
      ln_g.reshape(1, d), ln_b.reshape(1, d))


def _inproj_odd_kernel(x_ref, shift_ref, scale_ref, w_ref, wgb_ref, bg_ref,
                       q_ref, k_ref, v_ref, g_ref, lf_ref, h_scr):
    h_scr[...] = (x_ref[...] * (1.0 + scale_ref[...]) + shift_ref[...]).astype(BF16)

    def proj(c0, width=PROJ_COLS):
        return jnp.dot(h_scr[...], w_ref[:, c0:c0 + width], preferred_element_type=F32)

    low = proj(2 * C_KEY_DIM + 2 * C_VALUE_DIM, RANK_PAD).astype(BF16)

    def log_decay(c0, width):
        sl = slice(c0, c0 + width)
        gk = jnp.dot(low, wgb_ref[:, sl], preferred_element_type=F32) + bg_ref[:, sl]
        lf_ref[:, sl] = jax.nn.log_sigmoid(gk) * (1.0 / GATE_NORMALIZER)

    piece = C_KEY_DIM // 4
    for c0 in range(0, C_KEY_DIM, PROJ_COLS):
        sl = slice(c0, c0 + PROJ_COLS)
        q_ref[:, sl] = proj(c0) * (C_DK ** -0.5)
        log_decay(2 * c0 // PROJ_COLS * piece, piece)
        k_ref[:, sl] = proj(C_KEY_DIM + c0)
        log_decay((2 * c0 // PROJ_COLS + 1) * piece, piece)
    for c0 in range(0, C_VALUE_DIM, PROJ_COLS):
        sl = slice(c0, c0 + PROJ_COLS)
        v_ref[:, sl] = proj(2 * C_KEY_DIM + c0).astype(BF16)
        g_ref[:, sl] = proj(2 * C_KEY_DIM + C_VALUE_DIM + c0)


def _inproj_odd(x2, mod3, w_bf, w_gk_b_bf, b_gk, *, seq):
    n, d = x2.shape
    tm = 256
    per_seq = seq // tm
    width = w_bf.shape[1]
    row = lambda i: (i, 0)
    out_shapes = (
        jax.ShapeDtypeStruct((n, C_KEY_DIM), F32), jax.ShapeDtypeStruct((n, C_KEY_DIM), F32),
        jax.ShapeDtypeStruct((n, C_VALUE_DIM), BF16), jax.ShapeDtypeStruct((n, C_VALUE_DIM), F32),
        jax.ShapeDtypeStruct((n, C_KEY_DIM), F32))
    out_bytes = sum(tm * s.shape[1] * s.dtype.itemsize for s in out_shapes)
    est = d * width * 2 + 2 * tm * d * 4 + 2 * out_bytes + tm * d * 2 + 6 * tm * PROJ_COLS * 4
    return pl.pallas_call(
        _inproj_odd_kernel,
        out_shape=out_shapes,
        grid=(n // tm,),
        in_specs=[pl.BlockSpec((tm, d), row),
                  pl.BlockSpec((None, 1, d), lambda i: ((i // per_seq) * 6 + 0, 0, 0)),
                  pl.BlockSpec((None, 1, d), lambda i: ((i // per_seq) * 6 + 1, 0, 0)),
                  _resident((d, width)),
                  _resident((RANK_PAD, C_KEY_DIM)),
                  pl.BlockSpec((1, C_KEY_DIM), lambda i: (0, 0))],
        out_specs=tuple(pl.BlockSpec((tm, s.shape[1]), row) for s in out_shapes),
        scratch_shapes=[pltpu.VMEM((tm, d), BF16)],
        compiler_params=_params(("arbitrary",), est),
        name="inproj_odd",
    )(x2, mod3, mod3, w_bf, w_gk_b_bf, b_gk.reshape(1, C_KEY_DIM))


def kernel(x, c, positions,
           ada_w0, ada_b0, mix_w_in0, mix_w_out0, attn_sinks0, hgrn_lb_logits, hgrn_norm_w0,
           ln_mix_g0, ln_mix_b0, ffn_w_up0, ffn_conv_w0, ffn_conv_b0, ffn_w_down0, ln_ffn_g0, ln_ffn_b0,
           ada_w1, ada_b1, mix_w_in1, gla_w_gk_a1, gla_w_gk_b1, gla_b_gk1, gla_norm_w1, mix_w_out1,
           ln_mix_g1, ln_mix_b1, ffn_w_up1, ffn_conv_w1, ffn_conv_b1, ffn_w_down1, ln_ffn_g1, ln_ffn_b1):
    batch, seq, d = x.shape
    n = batch * seq
    x2 = x.reshape(n, d)
    c_pad = jnp.pad(c, ((0, 8 - batch), (0, 0)))

    def modulation(ada_w, ada_b):
        mod = _ada_modulation(c_pad, ada_w, ada_b)[:batch]
        return mod.reshape(batch * 6, 1, d)

    mod3 = modulation(ada_w0, ada_b0)
    cos, sin_signed = _rope_tables(positions.reshape(n, 1))
    q_a, k_a, v_a, q_b, k_b, lf_b, i_b, g_b = _inproj_even(
        x2, mod3, mix_w_in0.astype(BF16), cos, sin_signed, hgrn_lb_logits, seq=seq, layer=0)
    w_gk_a = jnp.pad(gla_w_gk_a1, ((0, 0), (0, RANK_PAD - GATE_RANK)))
    o_a, (w_out0, w_in1, w_out1) = _swa_attention(
        q_a, k_a, v_a, attn_sinks0, [mix_w_out0, (mix_w_in1, w_gk_a), mix_w_out1], batch=batch, seq=seq)
    o_b, (w_up0, w_down0) = _gla(
        q_b, k_b, i_b, lf_b, g_b, hgrn_norm_w0, [ffn_w_up0, ffn_w_down0], batch=batch, seq=seq,
        heads=B_HEADS, dk=B_HEAD_DIM, dv=B_HEAD_DIM, heads_per_step=2, name="hgrn2")
    x2 = _outproj_ln([o_a, o_b], w_out0, x2, mod3, 2, ln_mix_g0, ln_mix_b0, seq=seq)
    x2 = _ffn(x2, mod3, w_up0, ffn_conv_w0, ffn_conv_b0, w_down0, ln_ffn_g0, ln_ffn_b0, seq=seq)

    mod3 = modulation(ada_w1, ada_b1)
    w_gk_b = jnp.pad(gla_w_gk_b1, ((0, RANK_PAD - GATE_RANK), (0, 0))).astype(BF16)
    q_c, k_c, v_c, g_c, lf_c = _inproj_odd(x2, mod3, w_in1, w_gk_b, gla_b_gk1, seq=seq)
    o_c, (w_up1, w_down1) = _gla(
        q_c, k_c, v_c, lf_c, g_c, gla_norm_w1, [ffn_w_up1, ffn_w_down1], batch=batch, seq=seq,
        heads=C_HEADS, dk=C_DK, dv=C_DV, heads_per_step=1, name="gla")
    x2 = _outproj_ln([o_c], w_out1, x2, mod3, 2, ln_mix_g1, ln_mix_b1, seq=seq)
    x2 = _ffn(x2, mod3, w_up1, ffn_conv_w1, ffn_conv_b1, w_down1, ln_ffn_g1, ln_ffn_b1, seq=seq)
    return x2.reshape(batch, seq, d)
```

```python
import functools
import math

import jax
import jax.numpy as jnp
from jax import lax
from jax.experimental import pallas as pl
from jax.experimental.pallas import tpu as pltpu

F32 = jnp.float32
BF16 = jnp.bfloat16

D_MODEL = 2048
DEPTH = 2
HEAD_DIM = 64
A_Q_HEADS = 16
A_KV_HEADS = 4
A_Q_WIDTH = A_Q_HEADS * HEAD_DIM
A_KV_WIDTH = A_KV_HEADS * HEAD_DIM
WINDOW = 128
ROPE_THETA = 10000.0
B_HEADS = 8
B_HEAD_DIM = 128
B_WIDTH = B_HEADS * B_HEAD_DIM
EVEN_IN_WIDTH = A_Q_WIDTH + 2 * A_KV_WIDTH + 4 * B_WIDTH
C_HEADS = 4
C_KEY_DIM = D_MODEL // 2
C_VALUE_DIM = D_MODEL
C_DK = C_KEY_DIM // C_HEADS
C_DV = C_VALUE_DIM // C_HEADS
GATE_RANK = 16
GATE_NORMALIZER = 16.0
CHUNK = 64
D_FF = 5632
CONV_WIDTH = 3
LN_EPS = 1e-5
RMS_EPS = 1e-6
ALPHA = (2.0 * DEPTH) ** 0.25
NEG_INF = -1e30

V7X_LANES = 128
V7X_VMEM_BYTES = 64 * 1024 * 1024
V7X_VMEM_CEILING = 56 * 1024 * 1024
BF16_SUBLANE_PACK = 16

RANK_PAD = V7X_LANES


def _params(semantics, vmem_estimate):
    limit = min(V7X_VMEM_CEILING, max(16 * 1024 * 1024, int(vmem_estimate * 1.25)))
    return pltpu.CompilerParams(dimension_semantics=semantics, vmem_limit_bytes=limit)


def _resident(shape):
    return pl.BlockSpec(shape, lambda *_: (0,) * len(shape), pipeline_mode=pl.Buffered(1))


def _silu(x):
    return x * jax.nn.sigmoid(x)


class _SideCasts:
    def __init__(self, weights, grid):
        self.groups = [w if isinstance(w, (tuple, list)) else (w,) for w in weights]
        self.sources = [m for group in self.groups for m in group]
        self.group_sizes = tuple(len(group) for group in self.groups)
        steps = math.prod(grid)
        strides = [math.prod(grid[i + 1:]) for i in range(len(grid))]
        self.in_specs, self.out_specs, self.out_shapes = [], [], []
        self.vmem_bytes = 0
        for group in self.groups:
            rows = group[0].shape[0]
            cols = sum(m.shape[1] for m in group)
            share = 1
            while (rows * share) % (steps * BF16_SUBLANE_PACK):
                share *= 2
            block_rows = rows * share // steps

            def index(*ids, share=share):
                return (sum(i * s for i, s in zip(ids, strides)) // share, 0)

            self.in_specs += [pl.BlockSpec((block_rows, m.shape[1]), index) for m in group]
            self.out_specs.append(pl.BlockSpec((block_rows, cols), index))
            self.out_shapes.append(jax.ShapeDtypeStruct((rows, cols), BF16))
            self.vmem_bytes += 2 * block_rows * cols * (4 + 2)


def _copy_side_casts(src_refs, dst_refs, group_sizes):
    src_refs = list(src_refs)
    for dst, size in zip(dst_refs, group_sizes):
        col = 0
        for src in src_refs[:size]:
            dst[:, col:col + src.shape[1]] = src[...].astype(BF16)
            col += src.shape[1]
        src_refs = src_refs[size:]


def _layer_norm(z, g, b):
    mu = jnp.mean(z, axis=-1, keepdims=True)
    zc = z - mu
    var = jnp.mean(zc * zc, axis=-1, keepdims=True)
    return zc * lax.rsqrt(var + LN_EPS) * g + b


def _ada_kernel(c_ref, w_ref, b_ref, o_ref):
    a = _silu(c_ref[...]).astype(BF16)
    o_ref[...] = jnp.dot(a, w_ref[...].astype(BF16), preferred_element_type=F32) + b_ref[...]


def _ada_modulation(c_pad, ada_w, ada_b):
    rows, d = c_pad.shape
    n = ada_w.shape[1]
    tn = 1024
    est = 2 * d * tn * 4 + d * tn * 2 + 4 * rows * n
    return pl.pallas_call(
        _ada_kernel,
        out_shape=jax.ShapeDtypeStruct((rows, n), F32),
        grid=(n // tn,),
        in_specs=[pl.BlockSpec((rows, d), lambda j: (0, 0)),
                  pl.BlockSpec((d, tn), lambda j: (0, j)),
                  pl.BlockSpec((1, tn), lambda j: (0, j))],
        out_specs=pl.BlockSpec((rows, tn), lambda j: (0, j)),
        compiler_params=_params(("arbitrary",), est),
        name="ada_modulation",
    )(c_pad, ada_w, ada_b.reshape(1, n))


def _rope_table_kernel(*refs, cast_groups):
    pos_ref, invf_ref, sign_ref = refs[:3]
    n_src = sum(cast_groups)
    cos_ref, sin_ref = refs[3 + n_src:5 + n_src]
    _copy_side_casts(refs[3:3 + n_src], refs[5 + n_src:], cast_groups)
    ang = pos_ref[...].astype(F32) * invf_ref[...]
    cos_ref[...] = jnp.cos(ang)
    sin_ref[...] = jnp.sin(ang) * sign_ref[...]


def _rope_tables(pos_col, cast_weights):
    n = pos_col.shape[0]
    half = HEAD_DIM // 2
    lane = jnp.arange(V7X_LANES)
    inv_freq = ROPE_THETA ** (-jnp.arange(half, dtype=F32) / half)
    invf = inv_freq[lane % half].reshape(1, V7X_LANES)
    sign = jnp.where((lane % HEAD_DIM) < half, -1.0, 1.0).astype(F32).reshape(1, V7X_LANES)
    tm = 2048
    grid = (n // tm,)
    casts = _SideCasts(cast_weights, grid)
    est = 2 * (tm * V7X_LANES * 4) * 3 + casts.vmem_bytes
    outs = pl.pallas_call(
        functools.partial(_rope_table_kernel, cast_groups=casts.group_sizes),
        out_shape=[jax.ShapeDtypeStruct((n, V7X_LANES), F32)] * 2 + casts.out_shapes,
        grid=grid,
        in_specs=[pl.BlockSpec((tm, 1), lambda i: (i, 0)),
                  pl.BlockSpec((1, V7X_LANES), lambda i: (0, 0)),
                  pl.BlockSpec((1, V7X_LANES), lambda i: (0, 0))] + casts.in_specs,
        out_specs=[pl.BlockSpec((tm, V7X_LANES), lambda i: (i, 0))] * 2 + casts.out_specs,
        compiler_params=_params(("arbitrary",), est),
        name="rope_tables",
    )(pos_col, invf, sign, *casts.sources)
    return outs[0], outs[1], outs[2:]


def _rope(a, cos, sin_signed, first_half):
    outs = []
    for s in range(a.shape[1] // V7X_LANES):
        blk = a[:, s * V7X_LANES:(s + 1) * V7X_LANES]
        partner = jnp.where(first_half,
                            pltpu.roll(blk, V7X_LANES - HEAD_DIM // 2, 1),
                            pltpu.roll(blk, HEAD_DIM // 2, 1))
        outs.append(blk * cos + partner * sin_signed)
    return jnp.concatenate(outs, axis=1) if len(outs) > 1 else outs[0]


PROJ_COLS = 512


def _inproj_even_kernel(x_ref, shift_ref, scale_ref, w_ref, cos_ref, sin_ref, lbl_ref,
                        qa_ref, ka_ref, va_ref, qb_ref, kb_ref, lf_ref, ib_ref, gb_ref, h_scr, *, layer):
    tm = x_ref.shape[0]
    h_scr[...] = (x_ref[...] * (1.0 + scale_ref[...]) + shift_ref[...]).astype(BF16)

    def proj(c0, width=PROJ_COLS):
        return jnp.dot(h_scr[...], w_ref[:, c0:c0 + width], preferred_element_type=F32)

    cos = cos_ref[...]
    sin_signed = sin_ref[...]
    lane = lax.broadcasted_iota(jnp.int32, (tm, V7X_LANES), 1)
    first_half = (lane % HEAD_DIM) < (HEAD_DIM // 2)

    for c0 in range(0, A_Q_WIDTH, PROJ_COLS):
        qa_ref[:, c0:c0 + PROJ_COLS] = (_rope(proj(c0), cos, sin_signed, first_half)
                                        * (HEAD_DIM ** -0.5)).astype(BF16)
    kv = proj(A_Q_WIDTH, 2 * A_KV_WIDTH)
    ka_ref[...] = _rope(kv[:, :A_KV_WIDTH], cos, sin_signed, first_half).astype(BF16)
    va_ref[...] = kv[:, A_KV_WIDTH:].astype(BF16)

    base = A_Q_WIDTH + 2 * A_KV_WIDTH
    lg = lbl_ref[...]
    e = jnp.exp(lg - jnp.max(lg, axis=0, keepdims=True))
    sm = e / jnp.sum(e, axis=0, keepdims=True)
    lb = jnp.sum(sm[0:layer + 1], axis=0, keepdims=True)
    for c0 in range(0, B_WIDTH, PROJ_COLS):
        sl = slice(c0, c0 + PROJ_COLS)
        qb_ref[:, sl] = proj(base + c0)
        lbc = lb[:, sl]
        fg = lbc + (1.0 - lbc) * jax.nn.sigmoid(proj(base + B_WIDTH + c0))
        kb_ref[:, sl] = 1.0 - fg
        lf_ref[:, sl] = jnp.log(fg)
        ib_ref[:, sl] = proj(base + 2 * B_WIDTH + c0).astype(BF16)
        gb_ref[:, sl] = proj(base + 3 * B_WIDTH + c0)


def _inproj_even(x2, mod3, w_bf, cos, sin_signed, lb_logits, *, seq, layer):
    n, d = x2.shape
    tm = 256
    per_seq = seq // tm
    width = w_bf.shape[1]
    row = lambda i: (i, 0)
    out_shapes = (
        jax.ShapeDtypeStruct((n, A_Q_WIDTH), BF16), jax.ShapeDtypeStruct((n, A_KV_WIDTH), BF16),
        jax.ShapeDtypeStruct((n, A_KV_WIDTH), BF16), jax.ShapeDtypeStruct((n, B_WIDTH), F32),
        jax.ShapeDtypeStruct((n, B_WIDTH), F32), jax.ShapeDtypeStruct((n, B_WIDTH), F32),
        jax.ShapeDtypeStruct((n, B_WIDTH), BF16), jax.ShapeDtypeStruct((n, B_WIDTH), F32))
    out_bytes = sum(tm * s.shape[1] * s.dtype.itemsize for s in out_shapes)
    est = d * width * 2 + 2 * tm * d * 4 + 2 * out_bytes + tm * d * 2 + 6 * tm * PROJ_COLS * 4
    return pl.pallas_call(
        functools.partial(_inproj_even_kernel, layer=layer),
        out_shape=out_shapes,
        grid=(n // tm,),
        in_specs=[pl.BlockSpec((tm, d), row),
                  pl.BlockSpec((None, 1, d), lambda i: ((i // per_seq) * 6 + 0, 0, 0)),
                  pl.BlockSpec((None, 1, d), lambda i: ((i // per_seq) * 6 + 1, 0, 0)),
                  _resident((d, width)),
                  pl.BlockSpec((tm, V7X_LANES), row),
                  pl.BlockSpec((tm, V7X_LANES), row),
                  pl.BlockSpec(lb_logits.shape, lambda i: (0, 0))],
        out_specs=tuple(pl.BlockSpec((tm, s.shape[1]), row) for s in out_shapes),
        scratch_shapes=[pltpu.VMEM((tm, d), BF16)],
        compiler_params=_params(("arbitrary",), est),
        name="inproj_even",
    )(x2, mod3, mod3, w_bf, cos, sin_signed, lb_logits)


def _swa_kernel(*refs, cast_groups):
    sink_ref, q_ref, kp_ref, kc_ref, vp_ref, vc_ref = refs[:6]
    n_src = sum(cast_groups)
    o_ref = refs[6 + n_src]
    _copy_side_casts(refs[6:6 + n_src], refs[7 + n_src:], cast_groups)
    first_step = pl.program_id(1) == 0
    grp = A_Q_HEADS // A_KV_HEADS
    assert HEAD_DIM * 2 == V7X_LANES and grp == 4
    r = lax.broadcasted_iota(jnp.int32, (WINDOW, 2 * WINDOW), 0)
    c = lax.broadcasted_iota(jnp.int32, (WINDOW, 2 * WINDOW), 1)
    rel = r + WINDOW - c
    band = (rel >= 0) & (rel < WINDOW)
    band_first = band & ((c >= WINDOW) | jnp.logical_not(first_step))
    lane = lax.broadcasted_iota(jnp.int32, (1, V7X_LANES), 1)
    half_mask = [jnp.where(lane < HEAD_DIM, 1.0, 0.0).astype(BF16),
                 jnp.where(lane >= HEAD_DIM, 1.0, 0.0).astype(BF16)]
    low_half = lax.broadcasted_iota(jnp.int32, (WINDOW, V7X_LANES), 1) < HEAD_DIM

    def frame(prev_ref, cur_ref, sub, kv_tile):
        own = cur_ref[sub * WINDOW:(sub + 1) * WINDOW, kv_tile]
        before = prev_ref[:, kv_tile] if sub == 0 else cur_ref[(sub - 1) * WINDOW:sub * WINDOW, kv_tile]
        return jnp.concatenate([before, own], axis=0)

    def scores(sub, g):
        kv_tile = slice((g // 2) * V7X_LANES, (g // 2 + 1) * V7X_LANES)
        kv_half = g % 2
        q_rows = []
        for e in range(grp):
            q_tile = (2 * g + e // 2) * V7X_LANES
            qt = q_ref[sub * WINDOW:(sub + 1) * WINDOW, q_tile:q_tile + V7X_LANES]
            if e % 2 != kv_half:
                qt = pltpu.roll(qt, HEAD_DIM, 1)
            q_rows.append(qt * half_mask[kv_half])
        qs = jnp.concatenate(q_rows, axis=0)
        return lax.dot_general(qs, frame(kp_ref, kc_ref, sub, kv_tile), (((1,), (1,)), ((), ())),
                               preferred_element_type=F32)

    units = [(sub, g) for sub in range(q_ref.shape[0] // WINDOW) for g in range(A_KV_HEADS)]
    s_next = scores(*units[0])
    for idx, (sub, g) in enumerate(units):
        kv_tile = slice((g // 2) * V7X_LANES, (g // 2 + 1) * V7X_LANES)
        kv_half = g % 2
        qrows = slice(sub * WINDOW, (sub + 1) * WINDOW)
        valid = band_first if sub == 0 else band
        s_all = s_next
        if idx + 1 < len(units):
            s_next = scores(*units[idx + 1])
        p_rows = []
        denoms = []
        for e in range(grp):
            s = jnp.where(valid, s_all[e * WINDOW:(e + 1) * WINDOW], NEG_INF)
            sink = sink_ref[grp * g + e]
            m = jnp.maximum(jnp.max(s, axis=-1, keepdims=True), sink)
            p = jnp.exp(s - m)
            denoms.append(jnp.sum(p, axis=-1, keepdims=True) + jnp.exp(sink - m))
            p_rows.append(p.astype(BF16))
        o_all = jnp.dot(jnp.concatenate(p_rows, axis=0), frame(vp_ref, vc_ref, sub, kv_tile),
                        preferred_element_type=F32)
        for u in range(grp // 2):
            o_even = o_all[(2 * u) * WINDOW:(2 * u + 1) * WINDOW] / denoms[2 * u]
            o_odd = o_all[(2 * u + 1) * WINDOW:(2 * u + 2) * WINDOW] / denoms[2 * u + 1]
            if kv_half == 0:
                tile = jnp.where(low_half, o_even, pltpu.roll(o_odd, HEAD_DIM, 1))
            else:
                tile = jnp.where(low_half, pltpu.roll(o_even, HEAD_DIM, 1), o_odd)
            out_tile = (2 * g + u) * V7X_LANES
            o_ref[qrows, out_tile:out_tile + V7X_LANES] = tile.astype(BF16)


SWA_BLOCKS_PER_STEP = 1


def _swa_attention(q_a, k_a, v_a, sinks, cast_weights, *, batch, seq):
    n = q_a.shape[0]
    tq = SWA_BLOCKS_PER_STEP * WINDOW
    nb = seq // tq
    grid = (batch, nb)
    casts = _SideCasts(cast_weights, grid)
    cur = lambda b, i: (b * nb + i, 0)
    prev = lambda b, i: (jnp.maximum((b * nb + i) * SWA_BLOCKS_PER_STEP - 1, 0), 0)
    est = (2 * (2 * tq * A_Q_WIDTH * 2 + 2 * (tq + WINDOW) * A_KV_WIDTH * 2) + 24 * WINDOW * 2 * WINDOW * 4
           + casts.vmem_bytes)
    outs = pl.pallas_call(
        functools.partial(_swa_kernel, cast_groups=casts.group_sizes),
        out_shape=[jax.ShapeDtypeStruct((n, A_Q_WIDTH), BF16)] + casts.out_shapes,
        grid=grid,
        in_specs=[pl.BlockSpec(memory_space=pltpu.SMEM),
                  pl.BlockSpec((tq, A_Q_WIDTH), cur),
                  pl.BlockSpec((WINDOW, A_KV_WIDTH), prev),
                  pl.BlockSpec((tq, A_KV_WIDTH), cur),
                  pl.BlockSpec((WINDOW, A_KV_WIDTH), prev),
                  pl.BlockSpec((tq, A_KV_WIDTH), cur)] + casts.in_specs,
        out_specs=[pl.BlockSpec((tq, A_Q_WIDTH), cur)] + casts.out_specs,
        compiler_params=_params(("arbitrary", "arbitrary"), est),
        name="swa_attention",
    )(sinks, q_a, k_a, k_a, v_a, v_a, *casts.sources)
    return outs[0], outs[1:]


def _split3(x):
    hi = x.astype(BF16)
    r1 = x - hi.astype(F32)
    mid = r1.astype(BF16)
    lo = (r1 - mid.astype(F32)).astype(BF16)
    return hi, mid, lo


GLA_GROUP = 256


def _gla_kernel(*refs, dk, dv, heads_per_step, cast_groups):
    q_ref, k_ref, v_ref, lf_ref, g_ref, nw_ref = refs[:6]
    n_src = sum(cast_groups)
    o_ref = refs[6 + n_src]
    s_scr = refs[-1]
    _copy_side_casts(refs[6:6 + n_src], refs[7 + n_src:-1], cast_groups)

    @pl.when(pl.program_id(2) == 0)
    def _():
        s_scr[...] = jnp.zeros_like(s_scr)

    tb = q_ref.shape[0]
    assert GLA_GROUP == 4 * CHUNK
    row = lax.broadcasted_iota(jnp.int32, (GLA_GROUP, GLA_GROUP), 0)
    col = lax.broadcasted_iota(jnp.int32, (GLA_GROUP, GLA_GROUP), 1)
    rc = row // CHUNK
    cc = col // CHUNK
    same_chunk = (rc == cc) & (row >= col)
    next_chunk = (rc == cc + 1) & (rc % 2 == 1)
    far_chunk = (rc >= 2) & (cc <= 1)
    tril = jnp.where(same_chunk, 1.0, 0.0).astype(BF16)
    nw = nw_ref[...]
    one = jnp.ones((1, dk), F32)
    nt = (((1,), (1,)), ((), ()))

    def by_chunk(vecs):
        return jnp.concatenate([jnp.broadcast_to(v, (CHUNK, dk)) for v in vecs], axis=0)

    def prepare(hh, gi):
        rows = slice(gi * GLA_GROUP, (gi + 1) * GLA_GROUP)
        kcols = slice(hh * dk, (hh + 1) * dk)
        vcols = slice(hh * dv, (hh + 1) * dv)
        hi, mid, lo = _split3(lf_ref[rows, kcols])
        bb = jnp.dot(tril, jnp.concatenate([hi, mid, lo], axis=1), preferred_element_type=F32)
        b = bb[:, :dk] + bb[:, dk:2 * dk] + bb[:, 2 * dk:]
        l0, l1, l2, l3 = [b[(c + 1) * CHUNK - 1:(c + 1) * CHUNK, :] for c in range(4)]
        b_mid = by_chunk([b[c * CHUNK + CHUNK // 2 - 1:c * CHUNK + CHUNK // 2, :] for c in range(4)])
        q = q_ref[rows, kcols]
        k = k_ref[rows, kcols]
        v = v_ref[rows, vcols]
        q_loc = q * jnp.exp(b)
        k_loc = k * jnp.exp(by_chunk([l0, l1, l2, l3]) - b)
        q_i = (q * jnp.exp(b - b_mid)).astype(BF16)
        k_i = (k * jnp.exp(b_mid - b)).astype(BF16)
        q_far = (q_loc * by_chunk([one, one, one, jnp.exp(l2)])).astype(BF16)
        k_far = (k_loc * by_chunk([jnp.exp(l1), one, one, one])).astype(BF16)
        q_grp = (q_loc * by_chunk([one, jnp.exp(l0), jnp.exp(l0 + l1), jnp.exp(l0 + l1 + l2)])).astype(BF16)
        k_grp = (k_loc * by_chunk([jnp.exp(l1 + l2 + l3), jnp.exp(l2 + l3), jnp.exp(l3), one])).astype(BF16)
        a_same = lax.dot_general(q_i, k_i, nt, preferred_element_type=F32)
        a_next = lax.dot_general(q_loc.astype(BF16), k_loc.astype(BF16), nt, preferred_element_type=F32)
        a_far = lax.dot_general(q_far, k_far, nt, preferred_element_type=F32)
        a = jnp.where(same_chunk, a_same, jnp.where(next_chunk, a_next, jnp.where(far_chunk, a_far, 0.0)))
        o_intra = jnp.dot(a.astype(BF16), v, preferred_element_type=F32)
        kv = lax.dot_general(k_grp, v, (((0,), (0,)), ((), ())), preferred_element_type=F32)
        decay = jnp.exp(jnp.transpose(jnp.broadcast_to(l0 + l1 + l2 + l3, (V7X_LANES, dk))))
        return rows, vcols, o_intra, q_grp, kv, decay

    def finish(state, prepared):
        rows, vcols, o_intra, q_grp, kv, decay = prepared
        o = o_intra + jnp.dot(q_grp, state.astype(BF16), preferred_element_type=F32)
        o = o * lax.rsqrt(jnp.mean(o * o, axis=-1, keepdims=True) + RMS_EPS)
        o_ref[rows, vcols] = ((o * nw) * _silu(g_ref[rows, vcols])).astype(BF16)
        return state * jnp.tile(decay, (1, dv // V7X_LANES)) + kv

    units = [(hh, gi) for gi in range(tb // GLA_GROUP) for hh in range(heads_per_step)]
    states = [s_scr[hh] for hh in range(heads_per_step)]
    pending = prepare(*units[0])
    for idx, (hh, gi) in enumerate(units):
        upcoming = prepare(*units[idx + 1]) if idx + 1 < len(units) else None
        states[hh] = finish(states[hh], pending)
        pending = upcoming
    for hh in range(heads_per_step):
        s_scr[hh] = states[hh]


def _gla(q, k, v, log_f, g, norm_w, cast_weights, *, batch, seq, heads, dk, dv, heads_per_step, name):
    n = q.shape[0]
    tb = 1024
    nt = seq // tb
    hp = heads_per_step
    grid = (batch, heads // hp, nt)
    casts = _SideCasts(cast_weights, grid)
    idx = lambda b, h, t: (b * nt + t, h)
    est = (2 * tb * hp * (3 * dk * 4 + dv * 2 + dv * 4 + dv * 2) + hp * dk * dv * 4 + 4 * dk * dv * 4
           + 16 * GLA_GROUP * max(3 * dk, dv) * 4 + casts.vmem_bytes)
    outs = pl.pallas_call(
        functools.partial(_gla_kernel, dk=dk, dv=dv, heads_per_step=hp, cast_groups=casts.group_sizes),
        out_shape=[jax.ShapeDtypeStruct((n, heads * dv), BF16)] + casts.out_shapes,
        grid=grid,
        in_specs=[pl.BlockSpec((tb, hp * dk), idx), pl.BlockSpec((tb, hp * dk), idx),
                  pl.BlockSpec((tb, hp * dv), idx), pl.BlockSpec((tb, hp * dk), idx),
                  pl.BlockSpec((tb, hp * dv), idx),
                  pl.BlockSpec((1, dv), lambda b, h, t: (0, 0))] + casts.in_specs,
        out_specs=[pl.BlockSpec((tb, hp * dv), idx)] + casts.out_specs,
        scratch_shapes=[pltpu.VMEM((hp, dk, dv), F32)],
        compiler_params=_params(("arbitrary", "arbitrary", "arbitrary"), est),
        name=name,
    )(q, k, v, log_f, g, norm_w.reshape(1, dv), *casts.sources)
    return outs[0], outs[1:]


OUTPROJ_SUB_ROWS = 128


def _outproj_ln_kernel(*refs, k_sizes):
    lhs_refs = refs[:len(k_sizes)]
    w_ref, x_ref, gate_ref, g_ref, b_ref, o_ref = refs[len(k_sizes):]
    def project(rows):
        y = None
        off = 0
        for r, ks in zip(lhs_refs, k_sizes):
            part = jnp.dot(r[rows, :], w_ref[off:off + ks, :], preferred_element_type=F32)
            y = part if y is None else y + part
            off += ks
        return y

    subs = [slice(r0, r0 + OUTPROJ_SUB_ROWS) for r0 in range(0, o_ref.shape[0], OUTPROJ_SUB_ROWS)]
    y_next = project(subs[0])
    for idx, rows in enumerate(subs):
        y = y_next
        if idx + 1 < len(subs):
            y_next = project(subs[idx + 1])
        o_ref[rows, :] = _layer_norm(ALPHA * x_ref[rows, :] + gate_ref[...] * y, g_ref[...], b_ref[...])


def _outproj_ln(lhs_list, w_bf, x2, mod3, gate_slot, ln_g, ln_b, *, seq):
    n, d = x2.shape
    tm = 512
    per_seq = seq // tm
    row = lambda i: (i, 0)
    k_sizes = tuple(a.shape[1] for a in lhs_list)
    k_total = sum(k_sizes)
    vec = pl.BlockSpec((1, d), lambda i: (0, 0))
    est = k_total * d * 2 + 2 * tm * k_total * 2 + 4 * tm * d * 4 + 4 * tm * d * 4
    return pl.pallas_call(
        functools.partial(_outproj_ln_kernel, k_sizes=k_sizes),
        out_shape=jax.ShapeDtypeStruct((n, d), F32),
        grid=(n // tm,),
        in_specs=[pl.BlockSpec((tm, ks), row) for ks in k_sizes] + [
            _resident((k_total, d)),
            pl.BlockSpec((tm, d), row),
            pl.BlockSpec((None, 1, d), lambda i: ((i // per_seq) * 6 + gate_slot, 0, 0)),
            vec, vec],
        out_specs=pl.BlockSpec((tm, d), row),
        compiler_params=_params(("arbitrary",), est),
        name="outproj_ln",
    )(*lhs_list, w_bf, x2, mod3, ln_g.reshape(1, d), ln_b.reshape(1, d))


FFN_HALO = BF16_SUBLANE_PACK
FFN_ROW_SPLITS = 1


def _ffn_kernel(x_ref, halo_ref, shift_ref, scale_ref, gate_ref, wu_ref, wv_ref, cw_ref, cb_ref, wd_ref,
                lng_ref, lnb_ref, o_ref, h_scr, u_scr, *, per_seq):
    i = pl.program_id(0)
    f = pl.program_id(1)
    tm = x_ref.shape[0]

    @pl.when(f == 0)
    def _():
        sc = 1.0 + scale_ref[...]
        sh = shift_ref[...]
        h_scr[0:FFN_HALO, :] = (halo_ref[...] * sc + sh).astype(BF16)
        h_scr[FFN_HALO:, :] = (x_ref[...] * sc + sh).astype(BF16)
        o_ref[...] = jnp.zeros_like(o_ref)

    keep = jnp.where(i % per_seq == 0, 0.0, 1.0)
    cw = cw_ref[...]
    cb = cb_ref[...]
    sub = tm // FFN_ROW_SPLITS

    def up(s):
        lo = FFN_HALO + s * sub
        if s == 0:
            u_full = jnp.dot(h_scr[0:lo + sub, :], wu_ref[...], preferred_element_type=F32)
            u_scr[0:FFN_HALO, :] = u_full[0:FFN_HALO, :] * keep
            u_scr[FFN_HALO:lo + sub, :] = u_full[FFN_HALO:, :]
        else:
            u_scr[lo:lo + sub, :] = jnp.dot(h_scr[lo:lo + sub, :], wu_ref[...], preferred_element_type=F32)
        return jnp.dot(h_scr[lo:lo + sub, :], wv_ref[...], preferred_element_type=F32)

    def down(s, v):
        lo = FFN_HALO + s * sub
        u = (cw[2:3, :] * u_scr[lo:lo + sub, :]
             + cw[1:2, :] * u_scr[lo - 1:lo - 1 + sub, :]
             + cw[0:1, :] * u_scr[lo - 2:lo - 2 + sub, :]) + cb
        act = (_silu(u) * v).astype(BF16)
        o_ref[s * sub:(s + 1) * sub, :] += jnp.dot(act, wd_ref[...], preferred_element_type=F32)

    v_next = up(0)
    for s in range(FFN_ROW_SPLITS):
        v = v_next
        if s + 1 < FFN_ROW_SPLITS:
            v_next = up(s + 1)
        down(s, v)

    @pl.when(f == pl.num_programs(1) - 1)
    def _():
        o_ref[...] = _layer_norm(ALPHA * x_ref[...] + gate_ref[...] * o_ref[...], lng_ref[...], lnb_ref[...])


def _ffn(x2, mod3, w_up_bf, conv_w, conv_b, w_down_bf, ln_g, ln_b, *, seq):
    n, d = x2.shape
    tm = 512
    tf = 512
    per_seq = seq // tm
    nf = D_FF // tf
    halo_blocks = tm // FFN_HALO
    slot = lambda s: (lambda i, f: ((i // per_seq) * 6 + s, 0, 0))
    vec = pl.BlockSpec((1, d), lambda i, f: (0, 0))
    est = (4 * tm * d * 4 + 2 * FFN_HALO * d * 4 + 2 * 2 * d * tf * 2 + 2 * tf * d * 2
           + (tm + FFN_HALO) * d * 2 + (tm + FFN_HALO) * tf * 4 + 8 * tm * tf * 4 + 2 * tm * d * 4)
    return pl.pallas_call(
        functools.partial(_ffn_kernel, per_seq=per_seq),
        out_shape=jax.ShapeDtypeStruct((n, d), F32),
        grid=(n // tm, nf),
        in_specs=[pl.BlockSpec((tm, d), lambda i, f: (i, 0)),
                  pl.BlockSpec((FFN_HALO, d), lambda i, f: (jnp.maximum(i * halo_blocks - 1, 0), 0)),
                  pl.BlockSpec((None, 1, d), slot(3)),
                  pl.BlockSpec((None, 1, d), slot(4)),
                  pl.BlockSpec((None, 1, d), slot(5)),
                  pl.BlockSpec((d, tf), lambda i, f: (0, f)),
                  pl.BlockSpec((d, tf), lambda i, f: (0, nf + f)),
                  pl.BlockSpec((CONV_WIDTH, tf), lambda i, f: (0, f)),
                  pl.BlockSpec((1, tf), lambda i, f: (0, f)),
                  pl.BlockSpec((tf, d), lambda i, f: (f, 0)),
                  vec, vec],
        out_specs=pl.BlockSpec((tm, d), lambda i, f: (i, 0)),
        scratch_shapes=[pltpu.VMEM((tm + FFN_HALO, d), BF16), pltpu.VMEM((tm + FFN_HALO, tf), F32)],
        compiler_params=_params(("arbitrary", "arbitrary"), est),
        name="ffn",
    )(x2, x2, mod3, mod3, mod3, w_up_bf, w_up_bf, conv_w, conv_b.reshape(1, D_FF), w_down_bf,
      ln_g.reshape(1, d), ln_b.reshape(1, d))


FFN_PASSES = 2
FFN_PASS_TILE = 512
FFN_CARRY = 8


def _ffn_pass_kernel(*refs, per_seq, last):
    x_ref, shift_ref, scale_ref, wu_ref, wv_ref, cw_ref, cb_ref, wd_ref = refs[:8]
    if last:
        part_ref, gate_ref, lng_ref, lnb_ref, o_ref, h_scr, u_scr = refs[8:]
    else:
        o_ref, h_scr, u_scr = refs[8:]
    i = pl.program_id(0)
    tm = x_ref.shape[0]
    width = wd_ref.shape[0]
    h_scr[...] = (x_ref[...] * (1.0 + scale_ref[...]) + shift_ref[...]).astype(BF16)

    @pl.when(i == 0)
    def _():
        u_scr[...] = jnp.zeros_like(u_scr)

    inside_sequence = i % per_seq != 0
    u_scr[0:FFN_CARRY, :] = jnp.where(inside_sequence, u_scr[tm:tm + FFN_CARRY, :], 0.0)
    tiles = [(c0, min(FFN_PASS_TILE, width - c0)) for c0 in range(0, width, FFN_PASS_TILE)]

    def up(t):
        c0, w = tiles[t]
        cols = slice(c0, c0 + w)
        u_scr[FFN_CARRY:, cols] = jnp.dot(h_scr[...], wu_ref[:, cols], preferred_element_type=F32)
        return jnp.dot(h_scr[...], wv_ref[:, cols], preferred_element_type=F32)

    def down(t, v):
        c0, w = tiles[t]
        cols = slice(c0, c0 + w)
        u = (cw_ref[2:3, cols] * u_scr[FFN_CARRY:FFN_CARRY + tm, cols]
             + cw_ref[1:2, cols] * u_scr[FFN_CARRY - 1:FFN_CARRY - 1 + tm, cols]
             + cw_ref[0:1, cols] * u_scr[FFN_CARRY - 2:FFN_CARRY - 2 + tm, cols]) + cb_ref[:, cols]
        act = (_silu(u) * v).astype(BF16)
        return jnp.dot(act, wd_ref[cols, :], preferred_element_type=F32)

    acc = None
    v_next = up(0)
    for t in range(len(tiles)):
        v = v_next
        if t + 1 < len(tiles):
            v_next = up(t + 1)
        part = down(t, v)
        acc = part if acc is None else acc + part
    if last:
        y = acc + part_ref[...]
        o_ref[...] = _layer_norm(ALPHA * x_ref[...] + gate_ref[...] * y, lng_ref[...], lnb_ref[...])
    else:
        o_ref[...] = acc


def _ffn_two_pass(x2, mod3, w_up_bf, conv_w, conv_b, w_down_bf, ln_g, ln_b, *, seq):
    assert FFN_PASSES == 2
    n, d = x2.shape
    tm = 256
    per_seq = seq // tm
    width = D_FF // FFN_PASSES
    row = lambda i: (i, 0)
    slot = lambda s: (lambda i: ((i // per_seq) * 6 + s, 0, 0))
    vec = pl.BlockSpec((1, d), lambda i: (0, 0))
    conv_b2 = conv_b.reshape(1, D_FF)
    est = (3 * d * width * 2 + 6 * tm * d * 4 + tm * d * 2 + (tm + FFN_CARRY) * width * 4
           + 3 * tm * d * 4 + 8 * tm * FFN_PASS_TILE * 4)
    partial = None
    for p in range(FFN_PASSES):
        last = p == FFN_PASSES - 1
        in_specs = [pl.BlockSpec((tm, d), row),
                    pl.BlockSpec((None, 1, d), slot(3)),
                    pl.BlockSpec((None, 1, d), slot(4)),
                    pl.BlockSpec((d, width), lambda i, p=p: (0, p), pipeline_mode=pl.Buffered(1)),
                    pl.BlockSpec((d, width), lambda i, p=p: (0, FFN_PASSES + p), pipeline_mode=pl.Buffered(1)),
                    pl.BlockSpec((CONV_WIDTH, width), lambda i, p=p: (0, p)),
                    pl.BlockSpec((1, width), lambda i, p=p: (0, p)),
                    pl.BlockSpec((width, d), lambda i, p=p: (p, 0), pipeline_mode=pl.Buffered(1))]
        args = [x2, mod3, mod3, w_up_bf, w_up_bf, conv_w, conv_b2, w_down_bf]
        if partial is not None:
            in_specs.append(pl.BlockSpec((tm, d), row))
            args.append(partial)
        if last:
            in_specs += [pl.BlockSpec((None, 1, d), slot(5)), vec, vec]
            args += [mod3, ln_g.reshape(1, d), ln_b.reshape(1, d)]
        partial = pl.pallas_call(
            functools.partial(_ffn_pass_kernel, per_seq=per_seq, last=last),
            out_shape=jax.ShapeDtypeStruct((n, d), F32),
            grid=(n // tm,),
            in_specs=in_specs,
            out_specs=pl.BlockSpec((tm, d), row),
            scratch_shapes=[pltpu.VMEM((tm, d), BF16), pltpu.VMEM((tm + FFN_CARRY, width), F32)],
            compiler_params=_params(("arbitrary",), est),
            name="ffn_last" if last else "ffn_part",
        )(*args)
    return partial


def _inproj_odd_kernel(x_ref, shift_ref, scale_ref, w_ref, wgb_ref, bg_ref,
                       q_ref, k_ref, v_ref, g_ref, lf_ref, h_scr):
    h_scr[...] = (x_ref[...] * (1.0 + scale_ref[...]) + shift_ref[...]).astype(BF16)

    def proj(c0, width=PROJ_COLS):
        return jnp.dot(h_scr[...], w_ref[:, c0:c0 + width], preferred_element_type=F32)

    low = proj(2 * C_KEY_DIM + 2 * C_VALUE_DIM, RANK_PAD).astype(BF16)

    def log_decay(c0, width):
        sl = slice(c0, c0 + width)
        gk = jnp.dot(low, wgb_ref[:, sl], preferred_element_type=F32) + bg_ref[:, sl]
        lf_ref[:, sl] = jax.nn.log_sigmoid(gk) * (1.0 / GATE_NORMALIZER)

    piece = C_KEY_DIM // 4
    for c0 in range(0, C_KEY_DIM, PROJ_COLS):
        sl = slice(c0, c0 + PROJ_COLS)
        q_ref[:, sl] = proj(c0) * (C_DK ** -0.5)
        log_decay(2 * c0 // PROJ_COLS * piece, piece)
        k_ref[:, sl] = proj(C_KEY_DIM + c0)
        log_decay((2 * c0 // PROJ_COLS + 1) * piece, piece)
    for c0 in range(0, C_VALUE_DIM, PROJ_COLS):
        sl = slice(c0, c0 + PROJ_COLS)
        v_ref[:, sl] = proj(2 * C_KEY_DIM + c0).astype(BF16)
        g_ref[:, sl] = proj(2 * C_KEY_DIM + C_VALUE_DIM + c0)


def _inproj_odd(x2, mod3, w_bf, w_gk_b_bf, b_gk, *, seq):
    n, d = x2.shape
    tm = 256
    per_seq = seq // tm
    width = w_bf.shape[1]
    row = lambda i: (i, 0)
    out_shapes = (
        jax.ShapeDtypeStruct((n, C_KEY_DIM), F32), jax.ShapeDtypeStruct((n, C_KEY_DIM), F32),
        jax.ShapeDtypeStruct((n, C_VALUE_DIM), BF16), jax.ShapeDtypeStruct((n, C_VALUE_DIM), F32),
        jax.ShapeDtypeStruct((n, C_KEY_DIM), F32))
    out_bytes = sum(tm * s.shape[1] * s.dtype.itemsize for s in out_shapes)
    est = d * width * 2 + 2 * tm * d * 4 + 2 * out_bytes + tm * d * 2 + 6 * tm * PROJ_COLS * 4
    return pl.pallas_call(
        _inproj_odd_kernel,
        out_shape=out_shapes,
        grid=(n // tm,),
        in_specs=[pl.BlockSpec((tm, d), row),
                  pl.BlockSpec((None, 1, d), lambda i: ((i // per_seq) * 6 + 0, 0, 0)),
                  pl.BlockSpec((None, 1, d), lambda i: ((i // per_seq) * 6 + 1, 0, 0)),
                  _resident((d, width)),
                  _resident((RANK_PAD, C_KEY_DIM)),
                  pl.BlockSpec((1, C_KEY_DIM), lambda i: (0, 0))],
        out_specs=tuple(pl.BlockSpec((tm, s.shape[1]), row) for s in out_shapes),
        scratch_shapes=[pltpu.VMEM((tm, d), BF16)],
        compiler_params=_params(("arbitrary",), est),
        name="inproj_odd",
    )(x2, mod3, mod3, w_bf, w_gk_b_bf, b_gk.reshape(1, C_KEY_DIM))


def kernel(x, c, positions,
           ada_w0, ada_b0, mix_w_in0, mix_w_out0, attn_sinks0, hgrn_lb_logits, hgrn_norm_w0,
           ln_mix_g0, ln_mix_b0, ffn_w_up0, ffn_conv_w0, ffn_conv_b0, ffn_w_down0, ln_ffn_g0, ln_ffn_b0,
           ada_w1, ada_b1, mix_w_in1, gla_w_gk_a1, gla_w_gk_b1, gla_b_gk1, gla_norm_w1, mix_w_out1,
           ln_mix_g1, ln_mix_b1, ffn_w_up1, ffn_conv_w1, ffn_conv_b1, ffn_w_down1, ln_ffn_g1, ln_ffn_b1):
    batch, seq, d = x.shape
    n = batch * seq
    x2 = x.reshape(n, d)
    c_pad = jnp.pad(c, ((0, 8 - batch), (0, 0)))

    def modulation(ada_w, ada_b):
        mod = _ada_modulation(c_pad, ada_w, ada_b)[:batch]
        return mod.reshape(batch * 6, 1, d)

    mod3 = modulation(ada_w0, ada_b0)
    cos, sin_signed, (w_in0,) = _rope_tables(positions.reshape(n, 1), [mix_w_in0])
    q_a, k_a, v_a, q_b, k_b, lf_b, i_b, g_b = _inproj_even(
        x2, mod3, w_in0, cos, sin_signed, hgrn_lb_logits, seq=seq, layer=0)
    w_gk_a = jnp.pad(gla_w_gk_a1, ((0, 0), (0, RANK_PAD - GATE_RANK)))
    o_a, (w_out0, w_in1, w_out1) = _swa_attention(
        q_a, k_a, v_a, attn_sinks0, [mix_w_out0, (mix_w_in1, w_gk_a), mix_w_out1], batch=batch, seq=seq)
    o_b, (w_up0, w_down0) = _gla(
        q_b, k_b, i_b, lf_b, g_b, hgrn_norm_w0, [ffn_w_up0, ffn_w_down0], batch=batch, seq=seq,
        heads=B_HEADS, dk=B_HEAD_DIM, dv=B_HEAD_DIM, heads_per_step=2, name="hgrn2")
    x2 = _outproj_ln([o_a, o_b], w_out0, x2, mod3, 2, ln_mix_g0, ln_mix_b0, seq=seq)
    x2 = _ffn_two_pass(x2, mod3, w_up0, ffn_conv_w0, ffn_conv_b0, w_down0, ln_ffn_g0, ln_ffn_b0, seq=seq)

    mod3 = modulation(ada_w1, ada_b1)
    w_gk_b = jnp.pad(gla_w_gk_b1, ((0, RANK_PAD - GATE_RANK), (0, 0))).astype(BF16)
    q_c, k_c, v_c, g_c, lf_c = _inproj_odd(x2, mod3, w_in1, w_gk_b, gla_b_gk1, seq=seq)
    o_c, (w_up1, w_down1) = _gla(
        q_c, k_c, v_c, lf_c, g_c, gla_norm_w1, [ffn_w_up1, ffn_w_down1], batch=batch, seq=seq,
        heads=C_HEADS, dk=C_DK, dv=C_DV, heads_per_step=1, name="gla")
    x2 = _outproj_ln([o_c], w_out1, x2, mod3, 2, ln_mix_g1, ln_mix_b1, seq=seq)
    x2 = _ffn_two_pass(x2, mod3, w_up1, ffn_conv_w1, ffn_conv_b1, w_down1, ln_ffn_g1, ln_ffn_b1, seq=seq)
    return x2.reshape(batch, seq, d)
```

```python
import functools
import math

import jax
import jax.numpy as jnp
from jax import lax
from jax.experimental import pallas as pl
from jax.experimental.pallas import tpu as pltpu

F32 = jnp.float32
BF16 = jnp.bfloat16

D_MODEL = 2048
DEPTH = 2
HEAD_DIM = 64
A_Q_HEADS = 16
A_KV_HEADS = 4
A_Q_WIDTH = A_Q_HEADS * HEAD_DIM
A_KV_WIDTH = A_KV_HEADS * HEAD_DIM
WINDOW = 128
ROPE_THETA = 10000.0
B_HEADS = 8
B_HEAD_DIM = 128
B_WIDTH = B_HEADS * B_HEAD_DIM
EVEN_IN_WIDTH = A_Q_WIDTH + 2 * A_KV_WIDTH + 4 * B_WIDTH
C_HEADS = 4
C_KEY_DIM = D_MODEL // 2
C_VALUE_DIM = D_MODEL
C_DK = C_KEY_DIM // C_HEADS
C_DV = C_VALUE_DIM // C_HEADS
GATE_RANK = 16
GATE_NORMALIZER = 16.0
CHUNK = 64
D_FF = 5632
CONV_WIDTH = 3
LN_EPS = 1e-5
RMS_EPS = 1e-6
ALPHA = (2.0 * DEPTH) ** 0.25
NEG_INF = -1e30

V7X_LANES = 128
V7X_VMEM_BYTES = 64 * 1024 * 1024
V7X_VMEM_CEILING = 56 * 1024 * 1024
BF16_SUBLANE_PACK = 16

RANK_PAD = V7X_LANES


def _params(semantics, vmem_estimate):
    limit = min(V7X_VMEM_CEILING, max(16 * 1024 * 1024, int(vmem_estimate * 1.25)))
    return pltpu.CompilerParams(dimension_semantics=semantics, vmem_limit_bytes=limit)


def _resident(shape):
    return pl.BlockSpec(shape, lambda *_: (0,) * len(shape), pipeline_mode=pl.Buffered(1))


def _silu(x):
    return x * jax.nn.sigmoid(x)


class _SideCasts:
    def __init__(self, weights, grid):
        self.groups = [w if isinstance(w, (tuple, list)) else (w,) for w in weights]
        self.sources = [m for group in self.groups for m in group]
        self.group_sizes = tuple(len(group) for group in self.groups)
        steps = math.prod(grid)
        strides = [math.prod(grid[i + 1:]) for i in range(len(grid))]
        self.in_specs, self.out_specs, self.out_shapes = [], [], []
        self.vmem_bytes = 0
        for group in self.groups:
            rows = group[0].shape[0]
            cols = sum(m.shape[1] for m in group)
            share = 1
            while (rows * share) % (steps * BF16_SUBLANE_PACK):
                share *= 2
            block_rows = rows * share // steps

            def index(*ids, share=share):
                return (sum(i * s for i, s in zip(ids, strides)) // share, 0)

            self.in_specs += [pl.BlockSpec((block_rows, m.shape[1]), index) for m in group]
            self.out_specs.append(pl.BlockSpec((block_rows, cols), index))
            self.out_shapes.append(jax.ShapeDtypeStruct((rows, cols), BF16))
            self.vmem_bytes += 2 * block_rows * cols * (4 + 2)


def _copy_side_casts(src_refs, dst_refs, group_sizes):
    src_refs = list(src_refs)
    for dst, size in zip(dst_refs, group_sizes):
        col = 0
        for src in src_refs[:size]:
            dst[:, col:col + src.shape[1]] = src[...].astype(BF16)
            col += src.shape[1]
        src_refs = src_refs[size:]


def _layer_norm(z, g, b):
    mu = jnp.mean(z, axis=-1, keepdims=True)
    zc = z - mu
    var = jnp.mean(zc * zc, axis=-1, keepdims=True)
    return zc * lax.rsqrt(var + LN_EPS) * g + b


def _ada_kernel(c_ref, w_ref, b_ref, o_ref):
    a = _silu(c_ref[...]).astype(BF16)
    o_ref[...] = jnp.dot(a, w_ref[...].astype(BF16), preferred_element_type=F32) + b_ref[...]


def _ada_modulation(c_pad, ada_w, ada_b):
    rows, d = c_pad.shape
    n = ada_w.shape[1]
    tn = 1024
    est = 2 * d * tn * 4 + d * tn * 2 + 4 * rows * n
    return pl.pallas_call(
        _ada_kernel,
        out_shape=jax.ShapeDtypeStruct((rows, n), F32),
        grid=(n // tn,),
        in_specs=[pl.BlockSpec((rows, d), lambda j: (0, 0)),
                  pl.BlockSpec((d, tn), lambda j: (0, j)),
                  pl.BlockSpec((1, tn), lambda j: (0, j))],
        out_specs=pl.BlockSpec((rows, tn), lambda j: (0, j)),
        compiler_params=_params(("arbitrary",), est),
        name="ada_modulation",
    )(c_pad, ada_w, ada_b.reshape(1, n))


def _rope_table_kernel(*refs, cast_groups):
    pos_ref, invf_ref, sign_ref = refs[:3]
    n_src = sum(cast_groups)
    cos_ref, sin_ref = refs[3 + n_src:5 + n_src]
    _copy_side_casts(refs[3:3 + n_src], refs[5 + n_src:], cast_groups)
    ang = pos_ref[...].astype(F32) * invf_ref[...]
    cos_ref[...] = jnp.cos(ang)
    sin_ref[...] = jnp.sin(ang) * sign_ref[...]


def _rope_tables(pos_col, cast_weights):
    n = pos_col.shape[0]
    half = HEAD_DIM // 2
    lane = jnp.arange(V7X_LANES)
    inv_freq = ROPE_THETA ** (-jnp.arange(half, dtype=F32) / half)
    invf = inv_freq[lane % half].reshape(1, V7X_LANES)
    sign = jnp.where((lane % HEAD_DIM) < half, -1.0, 1.0).astype(F32).reshape(1, V7X_LANES)
    tm = 2048
    grid = (n // tm,)
    casts = _SideCasts(cast_weights, grid)
    est = 2 * (tm * V7X_LANES * 4) * 3 + casts.vmem_bytes
    outs = pl.pallas_call(
        functools.partial(_rope_table_kernel, cast_groups=casts.group_sizes),
        out_shape=[jax.ShapeDtypeStruct((n, V7X_LANES), F32)] * 2 + casts.out_shapes,
        grid=grid,
        in_specs=[pl.BlockSpec((tm, 1), lambda i: (i, 0)),
                  pl.BlockSpec((1, V7X_LANES), lambda i: (0, 0)),
                  pl.BlockSpec((1, V7X_LANES), lambda i: (0, 0))] + casts.in_specs,
        out_specs=[pl.BlockSpec((tm, V7X_LANES), lambda i: (i, 0))] * 2 + casts.out_specs,
        compiler_params=_params(("arbitrary",), est),
        name="rope_tables",
    )(pos_col, invf, sign, *casts.sources)
    return outs[0], outs[1], outs[2:]


def _rope(a, cos, sin_signed, first_half):
    outs = []
    for s in range(a.shape[1] // V7X_LANES):
        blk = a[:, s * V7X_LANES:(s + 1) * V7X_LANES]
        partner = jnp.where(first_half,
                            pltpu.roll(blk, V7X_LANES - HEAD_DIM // 2, 1),
                            pltpu.roll(blk, HEAD_DIM // 2, 1))
        outs.append(blk * cos + partner * sin_signed)
    return jnp.concatenate(outs, axis=1) if len(outs) > 1 else outs[0]


PROJ_COLS = 512


def _inproj_even_kernel(x_ref, shift_ref, scale_ref, w_ref, cos_ref, sin_ref, lbl_ref,
                        qa_ref, ka_ref, va_ref, qb_ref, kb_ref, lf_ref, ib_ref, gb_ref, h_scr, *, layer):
    tm = x_ref.shape[0]
    h_scr[...] = (x_ref[...] * (1.0 + scale_ref[...]) + shift_ref[...]).astype(BF16)

    def proj(c0, width=PROJ_COLS):
        return jnp.dot(h_scr[...], w_ref[:, c0:c0 + width], preferred_element_type=F32)

    cos = cos_ref[...]
    sin_signed = sin_ref[...]
    lane = lax.broadcasted_iota(jnp.int32, (tm, V7X_LANES), 1)
    first_half = (lane % HEAD_DIM) < (HEAD_DIM // 2)

    for c0 in range(0, A_Q_WIDTH, PROJ_COLS):
        qa_ref[:, c0:c0 + PROJ_COLS] = (_rope(proj(c0), cos, sin_signed, first_half)
                                        * (HEAD_DIM ** -0.5)).astype(BF16)
    kv = proj(A_Q_WIDTH, 2 * A_KV_WIDTH)
    ka_ref[...] = _rope(kv[:, :A_KV_WIDTH], cos, sin_signed, first_half).astype(BF16)
    va_ref[...] = kv[:, A_KV_WIDTH:].astype(BF16)

    base = A_Q_WIDTH + 2 * A_KV_WIDTH
    lg = lbl_ref[...]
    e = jnp.exp(lg - jnp.max(lg, axis=0, keepdims=True))
    sm = e / jnp.sum(e, axis=0, keepdims=True)
    lb = jnp.sum(sm[0:layer + 1], axis=0, keepdims=True)
    for c0 in range(0, B_WIDTH, PROJ_COLS):
        sl = slice(c0, c0 + PROJ_COLS)
        qb_ref[:, sl] = proj(base + c0).astype(BF16)
        lbc = lb[:, sl]
        fg = lbc + (1.0 - lbc) * jax.nn.sigmoid(proj(base + B_WIDTH + c0))
        kb_ref[:, sl] = (1.0 - fg).astype(BF16)
        lf_ref[:, sl] = jnp.log(fg)
        ib_ref[:, sl] = proj(base + 2 * B_WIDTH + c0).astype(BF16)
        gb_ref[:, sl] = proj(base + 3 * B_WIDTH + c0).astype(BF16)


def _inproj_even(x2, mod3, w_bf, cos, sin_signed, lb_logits, *, seq, layer):
    n, d = x2.shape
    tm = 512
    per_seq = seq // tm
    width = w_bf.shape[1]
    row = lambda i: (i, 0)
    out_shapes = (
        jax.ShapeDtypeStruct((n, A_Q_WIDTH), BF16), jax.ShapeDtypeStruct((n, A_KV_WIDTH), BF16),
        jax.ShapeDtypeStruct((n, A_KV_WIDTH), BF16), jax.ShapeDtypeStruct((n, B_WIDTH), BF16),
        jax.ShapeDtypeStruct((n, B_WIDTH), BF16), jax.ShapeDtypeStruct((n, B_WIDTH), F32),
        jax.ShapeDtypeStruct((n, B_WIDTH), BF16), jax.ShapeDtypeStruct((n, B_WIDTH), BF16))
    out_bytes = sum(tm * s.shape[1] * s.dtype.itemsize for s in out_shapes)
    est = d * width * 2 + 2 * tm * d * 4 + 2 * out_bytes + tm * d * 2 + 6 * tm * PROJ_COLS * 4
    return pl.pallas_call(
        functools.partial(_inproj_even_kernel, layer=layer),
        out_shape=out_shapes,
        grid=(n // tm,),
        in_specs=[pl.BlockSpec((tm, d), row),
                  pl.BlockSpec((None, 1, d), lambda i: ((i // per_seq) * 6 + 0, 0, 0)),
                  pl.BlockSpec((None, 1, d), lambda i: ((i // per_seq) * 6 + 1, 0, 0)),
                  _resident((d, width)),
                  pl.BlockSpec((tm, V7X_LANES), row),
                  pl.BlockSpec((tm, V7X_LANES), row),
                  pl.BlockSpec(lb_logits.shape, lambda i: (0, 0))],
        out_specs=tuple(pl.BlockSpec((tm, s.shape[1]), row) for s in out_shapes),
        scratch_shapes=[pltpu.VMEM((tm, d), BF16)],
        compiler_params=_params(("arbitrary",), est),
        name="inproj_even",
    )(x2, mod3, mod3, w_bf, cos, sin_signed, lb_logits)


def _swa_kernel(*refs, cast_groups):
    sink_ref, q_ref, kp_ref, kc_ref, vp_ref, vc_ref = refs[:6]
    n_src = sum(cast_groups)
    o_ref = refs[6 + n_src]
    _copy_side_casts(refs[6:6 + n_src], refs[7 + n_src:], cast_groups)
    first_step = pl.program_id(1) == 0
    grp = A_Q_HEADS // A_KV_HEADS
    assert HEAD_DIM * 2 == V7X_LANES and grp == 4
    r = lax.broadcasted_iota(jnp.int32, (WINDOW, 2 * WINDOW), 0)
    c = lax.broadcasted_iota(jnp.int32, (WINDOW, 2 * WINDOW), 1)
    rel = r + WINDOW - c
    band = (rel >= 0) & (rel < WINDOW)
    band_first = band & ((c >= WINDOW) | jnp.logical_not(first_step))
    lane = lax.broadcasted_iota(jnp.int32, (1, V7X_LANES), 1)
    half_mask = [jnp.where(lane < HEAD_DIM, 1.0, 0.0).astype(BF16),
                 jnp.where(lane >= HEAD_DIM, 1.0, 0.0).astype(BF16)]
    low_half = lax.broadcasted_iota(jnp.int32, (WINDOW, V7X_LANES), 1) < HEAD_DIM

    def frame(prev_ref, cur_ref, sub, kv_tile):
        own = cur_ref[sub * WINDOW:(sub + 1) * WINDOW, kv_tile]
        before = prev_ref[:, kv_tile] if sub == 0 else cur_ref[(sub - 1) * WINDOW:sub * WINDOW, kv_tile]
        return jnp.concatenate([before, own], axis=0)

    def scores(sub, g):
        kv_tile = slice((g // 2) * V7X_LANES, (g // 2 + 1) * V7X_LANES)
        kv_half = g % 2
        q_rows = []
        for e in range(grp):
            q_tile = (2 * g + e // 2) * V7X_LANES
            qt = q_ref[sub * WINDOW:(sub + 1) * WINDOW, q_tile:q_tile + V7X_LANES]
            if e % 2 != kv_half:
                qt = pltpu.roll(qt, HEAD_DIM, 1)
            q_rows.append(qt * half_mask[kv_half])
        qs = jnp.concatenate(q_rows, axis=0)
        return lax.dot_general(qs, frame(kp_ref, kc_ref, sub, kv_tile), (((1,), (1,)), ((), ())),
                               preferred_element_type=F32)

    units = [(sub, g) for sub in range(q_ref.shape[0] // WINDOW) for g in range(A_KV_HEADS)]
    s_next = scores(*units[0])
    for idx, (sub, g) in enumerate(units):
        kv_tile = slice((g // 2) * V7X_LANES, (g // 2 + 1) * V7X_LANES)
        kv_half = g % 2
        qrows = slice(sub * WINDOW, (sub + 1) * WINDOW)
        valid = band_first if sub == 0 else band
        s_all = s_next
        if idx + 1 < len(units):
            s_next = scores(*units[idx + 1])
        p_rows = []
        denoms = []
        for e in range(grp):
            s = jnp.where(valid, s_all[e * WINDOW:(e + 1) * WINDOW], NEG_INF)
            sink = sink_ref[grp * g + e]
            m = jnp.maximum(jnp.max(s, axis=-1, keepdims=True), sink)
            p = jnp.exp(s - m)
            denoms.append(jnp.sum(p, axis=-1, keepdims=True) + jnp.exp(sink - m))
            p_rows.append(p.astype(BF16))
        o_all = jnp.dot(jnp.concatenate(p_rows, axis=0), frame(vp_ref, vc_ref, sub, kv_tile),
                        preferred_element_type=F32)
        for u in range(grp // 2):
            o_even = o_all[(2 * u) * WINDOW:(2 * u + 1) * WINDOW] / denoms[2 * u]
            o_odd = o_all[(2 * u + 1) * WINDOW:(2 * u + 2) * WINDOW] / denoms[2 * u + 1]
            if kv_half == 0:
                tile = jnp.where(low_half, o_even, pltpu.roll(o_odd, HEAD_DIM, 1))
            else:
                tile = jnp.where(low_half, pltpu.roll(o_even, HEAD_DIM, 1), o_odd)
            out_tile = (2 * g + u) * V7X_LANES
            o_ref[qrows, out_tile:out_tile + V7X_LANES] = tile.astype(BF16)


SWA_BLOCKS_PER_STEP = 1


def _swa_attention(q_a, k_a, v_a, sinks, cast_weights, *, batch, seq):
    n = q_a.shape[0]
    tq = SWA_BLOCKS_PER_STEP * WINDOW
    nb = seq // tq
    grid = (batch, nb)
    casts = _SideCasts(cast_weights, grid)
    cur = lambda b, i: (b * nb + i, 0)
    prev = lambda b, i: (jnp.maximum((b * nb + i) * SWA_BLOCKS_PER_STEP - 1, 0), 0)
    est = (2 * (2 * tq * A_Q_WIDTH * 2 + 2 * (tq + WINDOW) * A_KV_WIDTH * 2) + 24 * WINDOW * 2 * WINDOW * 4
           + casts.vmem_bytes)
    outs = pl.pallas_call(
        functools.partial(_swa_kernel, cast_groups=casts.group_sizes),
        out_shape=[jax.ShapeDtypeStruct((n, A_Q_WIDTH), BF16)] + casts.out_shapes,
        grid=grid,
        in_specs=[pl.BlockSpec(memory_space=pltpu.SMEM),
                  pl.BlockSpec((tq, A_Q_WIDTH), cur),
                  pl.BlockSpec((WINDOW, A_KV_WIDTH), prev),
                  pl.BlockSpec((tq, A_KV_WIDTH), cur),
                  pl.BlockSpec((WINDOW, A_KV_WIDTH), prev),
                  pl.BlockSpec((tq, A_KV_WIDTH), cur)] + casts.in_specs,
        out_specs=[pl.BlockSpec((tq, A_Q_WIDTH), cur)] + casts.out_specs,
        compiler_params=_params(("arbitrary", "arbitrary"), est),
        name="swa_attention",
    )(sinks, q_a, k_a, k_a, v_a, v_a, *casts.sources)
    return outs[0], outs[1:]


def _split3(x):
    hi = x.astype(BF16)
    r1 = x - hi.astype(F32)
    mid = r1.astype(BF16)
    lo = (r1 - mid.astype(F32)).astype(BF16)
    return hi, mid, lo


GLA_GROUP = 256


def _gla_kernel(*refs, dk, dv, heads_per_step, cast_groups):
    q_ref, k_ref, v_ref, lf_ref, g_ref, nw_ref = refs[:6]
    n_src = sum(cast_groups)
    o_ref = refs[6 + n_src]
    s_scr = refs[-1]
    _copy_side_casts(refs[6:6 + n_src], refs[7 + n_src:-1], cast_groups)

    @pl.when(pl.program_id(2) == 0)
    def _():
        s_scr[...] = jnp.zeros_like(s_scr)

    tb = q_ref.shape[0]
    assert GLA_GROUP == 4 * CHUNK
    row = lax.broadcasted_iota(jnp.int32, (GLA_GROUP, GLA_GROUP), 0)
    col = lax.broadcasted_iota(jnp.int32, (GLA_GROUP, GLA_GROUP), 1)
    rc = row // CHUNK
    cc = col // CHUNK
    same_chunk = (rc == cc) & (row >= col)
    next_chunk = (rc == cc + 1) & (rc % 2 == 1)
    far_chunk = (rc >= 2) & (cc <= 1)
    tril = jnp.where(same_chunk, 1.0, 0.0).astype(BF16)
    nw = nw_ref[...]
    one = jnp.ones((1, dk), F32)
    nt = (((1,), (1,)), ((), ()))

    def by_chunk(vecs):
        return jnp.concatenate([jnp.broadcast_to(v, (CHUNK, dk)) for v in vecs], axis=0)

    def prepare(hh, gi):
        rows = slice(gi * GLA_GROUP, (gi + 1) * GLA_GROUP)
        kcols = slice(hh * dk, (hh + 1) * dk)
        vcols = slice(hh * dv, (hh + 1) * dv)
        hi, mid, lo = _split3(lf_ref[rows, kcols])
        bb = jnp.dot(tril, jnp.concatenate([hi, mid, lo], axis=1), preferred_element_type=F32)
        b = bb[:, :dk] + bb[:, dk:2 * dk] + bb[:, 2 * dk:]
        l0, l1, l2, l3 = [b[(c + 1) * CHUNK - 1:(c + 1) * CHUNK, :] for c in range(4)]
        b_mid = by_chunk([b[c * CHUNK + CHUNK // 2 - 1:c * CHUNK + CHUNK // 2, :] for c in range(4)])
        q = q_ref[rows, kcols].astype(F32)
        k = k_ref[rows, kcols].astype(F32)
        v = v_ref[rows, vcols]
        q_loc = q * jnp.exp(b)
        k_loc = k * jnp.exp(by_chunk([l0, l1, l2, l3]) - b)
        q_i = (q * jnp.exp(b - b_mid)).astype(BF16)
        k_i = (k * jnp.exp(b_mid - b)).astype(BF16)
        q_far = (q_loc * by_chunk([one, one, one, jnp.exp(l2)])).astype(BF16)
        k_far = (k_loc * by_chunk([jnp.exp(l1), one, one, one])).astype(BF16)
        q_grp = (q_loc * by_chunk([one, jnp.exp(l0), jnp.exp(l0 + l1), jnp.exp(l0 + l1 + l2)])).astype(BF16)
        k_grp = (k_loc * by_chunk([jnp.exp(l1 + l2 + l3), jnp.exp(l2 + l3), jnp.exp(l3), one])).astype(BF16)
        a_same = lax.dot_general(q_i, k_i, nt, preferred_element_type=F32)
        a_next = lax.dot_general(q_loc.astype(BF16), k_loc.astype(BF16), nt, preferred_element_type=F32)
        a_far = lax.dot_general(q_far, k_far, nt, preferred_element_type=F32)
        a = jnp.where(same_chunk, a_same, jnp.where(next_chunk, a_next, jnp.where(far_chunk, a_far, 0.0)))
        o_intra = jnp.dot(a.astype(BF16), v, preferred_element_type=F32)
        kv = lax.dot_general(k_grp, v, (((0,), (0,)), ((), ())), preferred_element_type=F32)
        decay = jnp.exp(jnp.transpose(jnp.broadcast_to(l0 + l1 + l2 + l3, (V7X_LANES, dk))))
        return rows, vcols, o_intra, q_grp, kv, decay

    def finish(state, prepared):
        rows, vcols, o_intra, q_grp, kv, decay = prepared
        o = o_intra + jnp.dot(q_grp, state.astype(BF16), preferred_element_type=F32)
        o = o * lax.rsqrt(jnp.mean(o * o, axis=-1, keepdims=True) + RMS_EPS)
        o_ref[rows, vcols] = ((o * nw) * _silu(g_ref[rows, vcols].astype(F32))).astype(BF16)
        return state * jnp.tile(decay, (1, dv // V7X_LANES)) + kv

    units = [(hh, gi) for gi in range(tb // GLA_GROUP) for hh in range(heads_per_step)]
    states = [s_scr[hh] for hh in range(heads_per_step)]
    pending = prepare(*units[0])
    for idx, (hh, gi) in enumerate(units):
        upcoming = prepare(*units[idx + 1]) if idx + 1 < len(units) else None
        states[hh] = finish(states[hh], pending)
        pending = upcoming
    for hh in range(heads_per_step):
        s_scr[hh] = states[hh]


def _gla(q, k, v, log_f, g, norm_w, cast_weights, *, batch, seq, heads, dk, dv, heads_per_step, name):
    n = q.shape[0]
    tb = 1024
    nt = seq // tb
    hp = heads_per_step
    grid = (batch, heads // hp, nt)
    casts = _SideCasts(cast_weights, grid)
    idx = lambda b, h, t: (b * nt + t, h)
    est = (2 * tb * hp * (3 * dk * 4 + dv * 2 + dv * 4 + dv * 2) + hp * dk * dv * 4 + 4 * dk * dv * 4
           + 16 * GLA_GROUP * max(3 * dk, dv) * 4 + casts.vmem_bytes)
    outs = pl.pallas_call(
        functools.partial(_gla_kernel, dk=dk, dv=dv, heads_per_step=hp, cast_groups=casts.group_sizes),
        out_shape=[jax.ShapeDtypeStruct((n, heads * dv), BF16)] + casts.out_shapes,
        grid=grid,
        in_specs=[pl.BlockSpec((tb, hp * dk), idx), pl.BlockSpec((tb, hp * dk), idx),
                  pl.BlockSpec((tb, hp * dv), idx), pl.BlockSpec((tb, hp * dk), idx),
                  pl.BlockSpec((tb, hp * dv), idx),
                  pl.BlockSpec((1, dv), lambda b, h, t: (0, 0))] + casts.in_specs,
        out_specs=[pl.BlockSpec((tb, hp * dv), idx)] + casts.out_specs,
        scratch_shapes=[pltpu.VMEM((hp, dk, dv), F32)],
        compiler_params=_params(("arbitrary", "arbitrary", "arbitrary"), est),
        name=name,
    )(q, k, v, log_f, g, norm_w.reshape(1, dv), *casts.sources)
    return outs[0], outs[1:]


OUTPROJ_SUB_ROWS = 128


def _outproj_ln_kernel(*refs, k_sizes):
    lhs_refs = refs[:len(k_sizes)]
    w_ref, x_ref, gate_ref, g_ref, b_ref, o_ref = refs[len(k_sizes):]
    def project(rows):
        y = None
        off = 0
        for r, ks in zip(lhs_refs, k_sizes):
            part = jnp.dot(r[rows, :], w_ref[off:off + ks, :], preferred_element_type=F32)
            y = part if y is None else y + part
            off += ks
        return y

    subs = [slice(r0, r0 + OUTPROJ_SUB_ROWS) for r0 in range(0, o_ref.shape[0], OUTPROJ_SUB_ROWS)]
    y_next = project(subs[0])
    for idx, rows in enumerate(subs):
        y = y_next
        if idx + 1 < len(subs):
            y_next = project(subs[idx + 1])
        o_ref[rows, :] = _layer_norm(ALPHA * x_ref[rows, :] + gate_ref[...] * y, g_ref[...], b_ref[...])


def _outproj_ln(lhs_list, w_bf, x2, mod3, gate_slot, ln_g, ln_b, *, seq):
    n, d = x2.shape
    tm = 512
    per_seq = seq // tm
    row = lambda i: (i, 0)
    k_sizes = tuple(a.shape[1] for a in lhs_list)
    k_total = sum(k_sizes)
    vec = pl.BlockSpec((1, d), lambda i: (0, 0))
    est = k_total * d * 2 + 2 * tm * k_total * 2 + 4 * tm * d * 4 + 4 * tm * d * 4
    return pl.pallas_call(
        functools.partial(_outproj_ln_kernel, k_sizes=k_sizes),
        out_shape=jax.ShapeDtypeStruct((n, d), F32),
        grid=(n // tm,),
        in_specs=[pl.BlockSpec((tm, ks), row) for ks in k_sizes] + [
            _resident((k_total, d)),
            pl.BlockSpec((tm, d), row),
            pl.BlockSpec((None, 1, d), lambda i: ((i // per_seq) * 6 + gate_slot, 0, 0)),
            vec, vec],
        out_specs=pl.BlockSpec((tm, d), row),
        compiler_params=_params(("arbitrary",), est),
        name="outproj_ln",
    )(*lhs_list, w_bf, x2, mod3, ln_g.reshape(1, d), ln_b.reshape(1, d))


FFN_PASSES = 2
FFN_PASS_TILE = 512
FFN_CARRY = 8


def _ffn_pass_kernel(*refs, per_seq, last):
    x_ref, shift_ref, scale_ref, wu_ref, wv_ref, cw_ref, cb_ref, wd_ref = refs[:8]
    if last:
        part_ref, gate_ref, lng_ref, lnb_ref, o_ref, h_scr, u_scr = refs[8:]
    else:
        o_ref, h_scr, u_scr = refs[8:]
    i = pl.program_id(0)
    tm = x_ref.shape[0]
    width = wd_ref.shape[0]
    h_scr[...] = (x_ref[...] * (1.0 + scale_ref[...]) + shift_ref[...]).astype(BF16)

    @pl.when(i == 0)
    def _():
        u_scr[...] = jnp.zeros_like(u_scr)

    inside_sequence = i % per_seq != 0
    u_scr[0:FFN_CARRY, :] = jnp.where(inside_sequence, u_scr[tm:tm + FFN_CARRY, :], 0.0)
    tiles = [(c0, min(FFN_PASS_TILE, width - c0)) for c0 in range(0, width, FFN_PASS_TILE)]

    def up(t):
        c0, w = tiles[t]
        cols = slice(c0, c0 + w)
        u_scr[FFN_CARRY:, cols] = jnp.dot(h_scr[...], wu_ref[:, cols], preferred_element_type=F32)
        return jnp.dot(h_scr[...], wv_ref[:, cols], preferred_element_type=F32)

    def down(t, v):
        c0, w = tiles[t]
        cols = slice(c0, c0 + w)
        u = (cw_ref[2:3, cols] * u_scr[FFN_CARRY:FFN_CARRY + tm, cols]
             + cw_ref[1:2, cols] * u_scr[FFN_CARRY - 1:FFN_CARRY - 1 + tm, cols]
             + cw_ref[0:1, cols] * u_scr[FFN_CARRY - 2:FFN_CARRY - 2 + tm, cols]) + cb_ref[:, cols]
        act = (_silu(u) * v).astype(BF16)
        return jnp.dot(act, wd_ref[cols, :], preferred_element_type=F32)

    acc = None
    v_next = up(0)
    for t in range(len(tiles)):
        v = v_next
        if t + 1 < len(tiles):
            v_next = up(t + 1)
        part = down(t, v)
        acc = part if acc is None else acc + part
    if last:
        y = acc + part_ref[...]
        o_ref[...] = _layer_norm(ALPHA * x_ref[...] + gate_ref[...] * y, lng_ref[...], lnb_ref[...])
    else:
        o_ref[...] = acc


def _ffn_two_pass(x2, mod3, w_up_bf, conv_w, conv_b, w_down_bf, ln_g, ln_b, *, seq):
    assert FFN_PASSES == 2
    n, d = x2.shape
    tm = 256
    per_seq = seq // tm
    width = D_FF // FFN_PASSES
    row = lambda i: (i, 0)
    slot = lambda s: (lambda i: ((i // per_seq) * 6 + s, 0, 0))
    vec = pl.BlockSpec((1, d), lambda i: (0, 0))
    conv_b2 = conv_b.reshape(1, D_FF)
    est = (3 * d * width * 2 + 6 * tm * d * 4 + tm * d * 2 + (tm + FFN_CARRY) * width * 4
           + 3 * tm * d * 4 + 8 * tm * FFN_PASS_TILE * 4)
    partial = None
    for p in range(FFN_PASSES):
        last = p == FFN_PASSES - 1
        in_specs = [pl.BlockSpec((tm, d), row),
                    pl.BlockSpec((None, 1, d), slot(3)),
                    pl.BlockSpec((None, 1, d), slot(4)),
                    pl.BlockSpec((d, width), lambda i, p=p: (0, p), pipeline_mode=pl.Buffered(1)),
                    pl.BlockSpec((d, width), lambda i, p=p: (0, FFN_PASSES + p), pipeline_mode=pl.Buffered(1)),
                    pl.BlockSpec((CONV_WIDTH, width), lambda i, p=p: (0, p)),
                    pl.BlockSpec((1, width), lambda i, p=p: (0, p)),
                    pl.BlockSpec((width, d), lambda i, p=p: (p, 0), pipeline_mode=pl.Buffered(1))]
        args = [x2, mod3, mod3, w_up_bf, w_up_bf, conv_w, conv_b2, w_down_bf]
        if partial is not None:
            in_specs.append(pl.BlockSpec((tm, d), row))
            args.append(partial)
        if last:
            in_specs += [pl.BlockSpec((None, 1, d), slot(5)), vec, vec]
            args += [mod3, ln_g.reshape(1, d), ln_b.reshape(1, d)]
        partial = pl.pallas_call(
            functools.partial(_ffn_pass_kernel, per_seq=per_seq, last=last),
            out_shape=jax.ShapeDtypeStruct((n, d), F32),
            grid=(n // tm,),
            in_specs=in_specs,
            out_specs=pl.BlockSpec((tm, d), row),
            scratch_shapes=[pltpu.VMEM((tm, d), BF16), pltpu.VMEM((tm + FFN_CARRY, width), F32)],
            compiler_params=_params(("arbitrary",), est),
            name="ffn_last" if last else "ffn_part",
        )(*args)
    return partial


def _inproj_odd_kernel(x_ref, shift_ref, scale_ref, w_ref, wgb_ref, bg_ref,
                       q_ref, k_ref, v_ref, g_ref, lf_ref, h_scr):
    h_scr[...] = (x_ref[...] * (1.0 + scale_ref[...]) + shift_ref[...]).astype(BF16)

    def proj(c0, width=PROJ_COLS):
        return jnp.dot(h_scr[...], w_ref[:, c0:c0 + width], preferred_element_type=F32)

    low = proj(2 * C_KEY_DIM + 2 * C_VALUE_DIM, RANK_PAD).astype(BF16)

    def log_decay(c0, width):
        sl = slice(c0, c0 + width)
        gk = jnp.dot(low, wgb_ref[:, sl], preferred_element_type=F32) + bg_ref[:, sl]
        lf_ref[:, sl] = jax.nn.log_sigmoid(gk) * (1.0 / GATE_NORMALIZER)

    piece = C_KEY_DIM // 4
    for c0 in range(0, C_KEY_DIM, PROJ_COLS):
        sl = slice(c0, c0 + PROJ_COLS)
        q_ref[:, sl] = (proj(c0) * (C_DK ** -0.5)).astype(BF16)
        log_decay(2 * c0 // PROJ_COLS * piece, piece)
        k_ref[:, sl] = proj(C_KEY_DIM + c0).astype(BF16)
        log_decay((2 * c0 // PROJ_COLS + 1) * piece, piece)
    for c0 in range(0, C_VALUE_DIM, PROJ_COLS):
        sl = slice(c0, c0 + PROJ_COLS)
        v_ref[:, sl] = proj(2 * C_KEY_DIM + c0).astype(BF16)
        g_ref[:, sl] = proj(2 * C_KEY_DIM + C_VALUE_DIM + c0).astype(BF16)


def _inproj_odd(x2, mod3, w_bf, w_gk_b_bf, b_gk, *, seq):
    n, d = x2.shape
    tm = 512
    per_seq = seq // tm
    width = w_bf.shape[1]
    row = lambda i: (i, 0)
    out_shapes = (
        jax.ShapeDtypeStruct((n, C_KEY_DIM), BF16), jax.ShapeDtypeStruct((n, C_KEY_DIM), BF16),
        jax.ShapeDtypeStruct((n, C_VALUE_DIM), BF16), jax.ShapeDtypeStruct((n, C_VALUE_DIM), BF16),
        jax.ShapeDtypeStruct((n, C_KEY_DIM), F32))
    out_bytes = sum(tm * s.shape[1] * s.dtype.itemsize for s in out_shapes)
    est = d * width * 2 + 2 * tm * d * 4 + 2 * out_bytes + tm * d * 2 + 6 * tm * PROJ_COLS * 4
    return pl.pallas_call(
        _inproj_odd_kernel,
        out_shape=out_shapes,
        grid=(n // tm,),
        in_specs=[pl.BlockSpec((tm, d), row),
                  pl.BlockSpec((None, 1, d), lambda i: ((i // per_seq) * 6 + 0, 0, 0)),
                  pl.BlockSpec((None, 1, d), lambda i: ((i // per_seq) * 6 + 1, 0, 0)),
                  _resident((d, width)),
                  _resident((RANK_PAD, C_KEY_DIM)),
                  pl.BlockSpec((1, C_KEY_DIM), lambda i: (0, 0))],
        out_specs=tuple(pl.BlockSpec((tm, s.shape[1]), row) for s in out_shapes),
        scratch_shapes=[pltpu.VMEM((tm, d), BF16)],
        compiler_params=_params(("arbitrary",), est),
        name="inproj_odd",
    )(x2, mod3, mod3, w_bf, w_gk_b_bf, b_gk.reshape(1, C_KEY_DIM))


def kernel(x, c, positions,
           ada_w0, ada_b0, mix_w_in0, mix_w_out0, attn_sinks0, hgrn_lb_logits, hgrn_norm_w0,
           ln_mix_g0, ln_mix_b0, ffn_w_up0, ffn_conv_w0, ffn_conv_b0, ffn_w_down0, ln_ffn_g0, ln_ffn_b0,
           ada_w1, ada_b1, mix_w_in1, gla_w_gk_a1, gla_w_gk_b1, gla_b_gk1, gla_norm_w1, mix_w_out1,
           ln_mix_g1, ln_mix_b1, ffn_w_up1, ffn_conv_w1, ffn_conv_b1, ffn_w_down1, ln_ffn_g1, ln_ffn_b1):
    batch, seq, d = x.shape
    n = batch * seq
    x2 = x.reshape(n, d)
    c_pad = jnp.pad(c, ((0, 8 - batch), (0, 0)))

    def modulation(ada_w, ada_b):
        mod = _ada_modulation(c_pad, ada_w, ada_b)[:batch]
        return mod.reshape(batch * 6, 1, d)

    mod3 = modulation(ada_w0, ada_b0)
    cos, sin_signed, (w_in0,) = _rope_tables(positions.reshape(n, 1), [mix_w_in0])
    q_a, k_a, v_a, q_b, k_b, lf_b, i_b, g_b = _inproj_even(
        x2, mod3, w_in0, cos, sin_signed, hgrn_lb_logits, seq=seq, layer=0)
    w_gk_a = jnp.pad(gla_w_gk_a1, ((0, 0), (0, RANK_PAD - GATE_RANK)))
    o_a, (w_out0, w_in1, w_out1) = _swa_attention(
        q_a, k_a, v_a, attn_sinks0, [mix_w_out0, (mix_w_in1, w_gk_a), mix_w_out1], batch=batch, seq=seq)
    o_b, (w_up0, w_down0) = _gla(
        q_b, k_b, i_b, lf_b, g_b, hgrn_norm_w0, [ffn_w_up0, ffn_w_down0], batch=batch, seq=seq,
        heads=B_HEADS, dk=B_HEAD_DIM, dv=B_HEAD_DIM, heads_per_step=4, name="hgrn2")
    x2 = _outproj_ln([o_a, o_b], w_out0, x2, mod3, 2, ln_mix_g0, ln_mix_b0, seq=seq)
    x2 = _ffn_two_pass(x2, mod3, w_up0, ffn_conv_w0, ffn_conv_b0, w_down0, ln_ffn_g0, ln_ffn_b0, seq=seq)

    mod3 = modulation(ada_w1, ada_b1)
    w_gk_b = jnp.pad(gla_w_gk_b1, ((0, RANK_PAD - GATE_RANK), (0, 0))).astype(BF16)
    q_c, k_c, v_c, g_c, lf_c = _inproj_odd(x2, mod3, w_in1, w_gk_b, gla_b_gk1, seq=seq)
    o_c, (w_up1, w_down1) = _gla(
        q_c, k_c, v_c, lf_c, g_c, gla_norm_w1, [ffn_w_up1, ffn_w_down1], batch=batch, seq=seq,
        heads=C_HEADS, dk=C_DK, dv=C_DV, heads_per_step=1, name="gla")
    x2 = _outproj_ln([o_c], w_out1, x2, mod3, 2, ln_mix_g1, ln_mix_b1, seq=seq)
    x2 = _ffn_two_pass(x2, mod3, w_up1, ffn_conv_w1, ffn_conv_b1, w_down1, ln_ffn_g1, ln_ffn_b1, seq=seq)
    return x2.reshape(batch, seq, d)
```

```python
import functools
import math

import jax
import jax.numpy as jnp
from jax import lax
from jax.experimental import pallas as pl
from jax.experimental.pallas import tpu as pltpu

F32 = jnp.float32
BF16 = jnp.bfloat16

D_MODEL = 2048
DEPTH = 2
HEAD_DIM = 64
A_Q_HEADS = 16
A_KV_HEADS = 4
A_Q_WIDTH = A_Q_HEADS * HEAD_DIM
A_KV_WIDTH = A_KV_HEADS * HEAD_DIM
WINDOW = 128
ROPE_THETA = 10000.0
B_HEADS = 8
B_HEAD_DIM = 128
B_WIDTH = B_HEADS * B_HEAD_DIM
EVEN_IN_WIDTH = A_Q_WIDTH + 2 * A_KV_WIDTH + 4 * B_WIDTH
C_HEADS = 4
C_KEY_DIM = D_MODEL // 2
C_VALUE_DIM = D_MODEL
C_DK = C_KEY_DIM // C_HEADS
C_DV = C_VALUE_DIM // C_HEADS
GATE_RANK = 16
GATE_NORMALIZER = 16.0
CHUNK = 64
D_FF = 5632
CONV_WIDTH = 3
LN_EPS = 1e-5
RMS_EPS = 1e-6
ALPHA = (2.0 * DEPTH) ** 0.25
NEG_INF = -1e30

V7X_LANES = 128
V7X_VMEM_BYTES = 64 * 1024 * 1024
V7X_VMEM_CEILING = 56 * 1024 * 1024
BF16_SUBLANE_PACK = 16

RANK_PAD = V7X_LANES


def _params(semantics, vmem_estimate):
    limit = min(V7X_VMEM_CEILING, max(16 * 1024 * 1024, int(vmem_estimate * 1.25)))
    return pltpu.CompilerParams(dimension_semantics=semantics, vmem_limit_bytes=limit)


def _resident(shape):
    return pl.BlockSpec(shape, lambda *_: (0,) * len(shape), pipeline_mode=pl.Buffered(1))


def _silu(x):
    return x * jax.nn.sigmoid(x)


class _SideCasts:
    def __init__(self, weights, grid):
        self.groups = [w if isinstance(w, (tuple, list)) else (w,) for w in weights]
        self.sources = [m for group in self.groups for m in group]
        self.group_sizes = tuple(len(group) for group in self.groups)
        steps = math.prod(grid)
        strides = [math.prod(grid[i + 1:]) for i in range(len(grid))]
        self.in_specs, self.out_specs, self.out_shapes = [], [], []
        self.vmem_bytes = 0
        for group in self.groups:
            rows = group[0].shape[0]
            cols = sum(m.shape[1] for m in group)
            share = 1
            while (rows * share) % (steps * BF16_SUBLANE_PACK):
                share *= 2
            block_rows = rows * share // steps

            def index(*ids, share=share):
                return (sum(i * s for i, s in zip(ids, strides)) // share, 0)

            self.in_specs += [pl.BlockSpec((block_rows, m.shape[1]), index) for m in group]
            self.out_specs.append(pl.BlockSpec((block_rows, cols), index))
            self.out_shapes.append(jax.ShapeDtypeStruct((rows, cols), BF16))
            self.vmem_bytes += 2 * block_rows * cols * (4 + 2)


def _copy_side_casts(src_refs, dst_refs, group_sizes):
    src_refs = list(src_refs)
    for dst, size in zip(dst_refs, group_sizes):
        col = 0
        for src in src_refs[:size]:
            dst[:, col:col + src.shape[1]] = src[...].astype(BF16)
            col += src.shape[1]
        src_refs = src_refs[size:]


def _layer_norm(z, g, b):
    mu = jnp.mean(z, axis=-1, keepdims=True)
    zc = z - mu
    var = jnp.mean(zc * zc, axis=-1, keepdims=True)
    return zc * lax.rsqrt(var + LN_EPS) * g + b


def _ada_kernel(c_ref, w_ref, b_ref, o_ref):
    a = _silu(c_ref[...]).astype(BF16)
    o_ref[...] = jnp.dot(a, w_ref[...].astype(BF16), preferred_element_type=F32) + b_ref[...]


def _ada_modulation(c_pad, ada_w, ada_b):
    rows, d = c_pad.shape
    n = ada_w.shape[1]
    tn = 1024
    est = 2 * d * tn * 4 + d * tn * 2 + 4 * rows * n
    return pl.pallas_call(
        _ada_kernel,
        out_shape=jax.ShapeDtypeStruct((rows, n), F32),
        grid=(n // tn,),
        in_specs=[pl.BlockSpec((rows, d), lambda j: (0, 0)),
                  pl.BlockSpec((d, tn), lambda j: (0, j)),
                  pl.BlockSpec((1, tn), lambda j: (0, j))],
        out_specs=pl.BlockSpec((rows, tn), lambda j: (0, j)),
        compiler_params=_params(("arbitrary",), est),
        name="ada_modulation",
    )(c_pad, ada_w, ada_b.reshape(1, n))


def _rope_table_kernel(*refs, cast_groups):
    pos_ref, invf_ref, sign_ref = refs[:3]
    n_src = sum(cast_groups)
    cos_ref, sin_ref = refs[3 + n_src:5 + n_src]
    _copy_side_casts(refs[3:3 + n_src], refs[5 + n_src:], cast_groups)
    ang = pos_ref[...].astype(F32) * invf_ref[...]
    cos_ref[...] = jnp.cos(ang)
    sin_ref[...] = jnp.sin(ang) * sign_ref[...]


def _rope_tables(pos_col, cast_weights):
    n = pos_col.shape[0]
    half = HEAD_DIM // 2
    lane = jnp.arange(V7X_LANES)
    inv_freq = ROPE_THETA ** (-jnp.arange(half, dtype=F32) / half)
    invf = inv_freq[lane % half].reshape(1, V7X_LANES)
    sign = jnp.where((lane % HEAD_DIM) < half, -1.0, 1.0).astype(F32).reshape(1, V7X_LANES)
    tm = 2048
    grid = (n // tm,)
    casts = _SideCasts(cast_weights, grid)
    est = 2 * (tm * V7X_LANES * 4) * 3 + casts.vmem_bytes
    outs = pl.pallas_call(
        functools.partial(_rope_table_kernel, cast_groups=casts.group_sizes),
        out_shape=[jax.ShapeDtypeStruct((n, V7X_LANES), F32)] * 2 + casts.out_shapes,
        grid=grid,
        in_specs=[pl.BlockSpec((tm, 1), lambda i: (i, 0)),
                  pl.BlockSpec((1, V7X_LANES), lambda i: (0, 0)),
                  pl.BlockSpec((1, V7X_LANES), lambda i: (0, 0))] + casts.in_specs,
        out_specs=[pl.BlockSpec((tm, V7X_LANES), lambda i: (i, 0))] * 2 + casts.out_specs,
        compiler_params=_params(("arbitrary",), est),
        name="rope_tables",
    )(pos_col, invf, sign, *casts.sources)
    return outs[0], outs[1], outs[2:]


def _rope(a, cos, sin_signed, first_half):
    outs = []
    for s in range(a.shape[1] // V7X_LANES):
        blk = a[:, s * V7X_LANES:(s + 1) * V7X_LANES]
        partner = jnp.where(first_half,
                            pltpu.roll(blk, V7X_LANES - HEAD_DIM // 2, 1),
                            pltpu.roll(blk, HEAD_DIM // 2, 1))
        outs.append(blk * cos + partner * sin_signed)
    return jnp.concatenate(outs, axis=1) if len(outs) > 1 else outs[0]


PROJ_COLS = 512


def _inproj_even_kernel(x_ref, shift_ref, scale_ref, w_ref, cos_ref, sin_ref, lbl_ref,
                        qa_ref, ka_ref, va_ref, qb_ref, kb_ref, lf_ref, ib_ref, gb_ref, h_scr, *, layer):
    tm = x_ref.shape[0]
    h_scr[...] = (x_ref[...] * (1.0 + scale_ref[...]) + shift_ref[...]).astype(BF16)

    def proj(c0, width=PROJ_COLS):
        return jnp.dot(h_scr[...], w_ref[:, c0:c0 + width], preferred_element_type=F32)

    cos = cos_ref[...]
    sin_signed = sin_ref[...]
    lane = lax.broadcasted_iota(jnp.int32, (tm, V7X_LANES), 1)
    first_half = (lane % HEAD_DIM) < (HEAD_DIM // 2)

    for c0 in range(0, A_Q_WIDTH, PROJ_COLS):
        qa_ref[:, c0:c0 + PROJ_COLS] = (_rope(proj(c0), cos, sin_signed, first_half)
                                        * (HEAD_DIM ** -0.5)).astype(BF16)
    kv = proj(A_Q_WIDTH, 2 * A_KV_WIDTH)
    ka_ref[...] = _rope(kv[:, :A_KV_WIDTH], cos, sin_signed, first_half).astype(BF16)
    va_ref[...] = kv[:, A_KV_WIDTH:].astype(BF16)

    base = A_Q_WIDTH + 2 * A_KV_WIDTH
    lg = lbl_ref[...]
    e = jnp.exp(lg - jnp.max(lg, axis=0, keepdims=True))
    sm = e / jnp.sum(e, axis=0, keepdims=True)
    lb = jnp.sum(sm[0:layer + 1], axis=0, keepdims=True)
    for c0 in range(0, B_WIDTH, PROJ_COLS):
        sl = slice(c0, c0 + PROJ_COLS)
        qb_ref[:, sl] = proj(base + c0).astype(BF16)
        lbc = lb[:, sl]
        fg = lbc + (1.0 - lbc) * jax.nn.sigmoid(proj(base + B_WIDTH + c0))
        kb_ref[:, sl] = (1.0 - fg).astype(BF16)
        lf_ref[:, sl] = jnp.log(fg)
        ib_ref[:, sl] = proj(base + 2 * B_WIDTH + c0).astype(BF16)
        gb_ref[:, sl] = proj(base + 3 * B_WIDTH + c0).astype(BF16)


def _inproj_even(x2, mod3, w_bf, cos, sin_signed, lb_logits, *, seq, layer):
    n, d = x2.shape
    tm = 512
    per_seq = seq // tm
    width = w_bf.shape[1]
    row = lambda i: (i, 0)
    out_shapes = (
        jax.ShapeDtypeStruct((n, A_Q_WIDTH), BF16), jax.ShapeDtypeStruct((n, A_KV_WIDTH), BF16),
        jax.ShapeDtypeStruct((n, A_KV_WIDTH), BF16), jax.ShapeDtypeStruct((n, B_WIDTH), BF16),
        jax.ShapeDtypeStruct((n, B_WIDTH), BF16), jax.ShapeDtypeStruct((n, B_WIDTH), F32),
        jax.ShapeDtypeStruct((n, B_WIDTH), BF16), jax.ShapeDtypeStruct((n, B_WIDTH), BF16))
    out_bytes = sum(tm * s.shape[1] * s.dtype.itemsize for s in out_shapes)
    est = d * width * 2 + 2 * tm * d * 4 + 2 * out_bytes + tm * d * 2 + 6 * tm * PROJ_COLS * 4
    return pl.pallas_call(
        functools.partial(_inproj_even_kernel, layer=layer),
        out_shape=out_shapes,
        grid=(n // tm,),
        in_specs=[pl.BlockSpec((tm, d), row),
                  pl.BlockSpec((None, 1, d), lambda i: ((i // per_seq) * 6 + 0, 0, 0)),
                  pl.BlockSpec((None, 1, d), lambda i: ((i // per_seq) * 6 + 1, 0, 0)),
                  _resident((d, width)),
                  pl.BlockSpec((tm, V7X_LANES), row),
                  pl.BlockSpec((tm, V7X_LANES), row),
                  pl.BlockSpec(lb_logits.shape, lambda i: (0, 0))],
        out_specs=tuple(pl.BlockSpec((tm, s.shape[1]), row) for s in out_shapes),
        scratch_shapes=[pltpu.VMEM((tm, d), BF16)],
        compiler_params=_params(("arbitrary",), est),
        name="inproj_even",
    )(x2, mod3, mod3, w_bf, cos, sin_signed, lb_logits)


def _swa_kernel(*refs, cast_groups):
    sink_ref, q_ref, kp_ref, kc_ref, vp_ref, vc_ref = refs[:6]
    n_src = sum(cast_groups)
    o_ref = refs[6 + n_src]
    _copy_side_casts(refs[6:6 + n_src], refs[7 + n_src:], cast_groups)
    first_step = pl.program_id(1) == 0
    grp = A_Q_HEADS // A_KV_HEADS
    assert HEAD_DIM * 2 == V7X_LANES and grp == 4
    r = lax.broadcasted_iota(jnp.int32, (WINDOW, 2 * WINDOW), 0)
    c = lax.broadcasted_iota(jnp.int32, (WINDOW, 2 * WINDOW), 1)
    rel = r + WINDOW - c
    band = (rel >= 0) & (rel < WINDOW)
    band_first = band & ((c >= WINDOW) | jnp.logical_not(first_step))
    lane = lax.broadcasted_iota(jnp.int32, (1, V7X_LANES), 1)
    half_mask = [jnp.where(lane < HEAD_DIM, 1.0, 0.0).astype(BF16),
                 jnp.where(lane >= HEAD_DIM, 1.0, 0.0).astype(BF16)]
    low_half = lax.broadcasted_iota(jnp.int32, (WINDOW, V7X_LANES), 1) < HEAD_DIM

    def frame(prev_ref, cur_ref, sub, kv_tile):
        own = cur_ref[sub * WINDOW:(sub + 1) * WINDOW, kv_tile]
        before = prev_ref[:, kv_tile] if sub == 0 else cur_ref[(sub - 1) * WINDOW:sub * WINDOW, kv_tile]
        return jnp.concatenate([before, own], axis=0)

    def scores(sub, g):
        kv_tile = slice((g // 2) * V7X_LANES, (g // 2 + 1) * V7X_LANES)
        kv_half = g % 2
        q_rows = []
        for e in range(grp):
            q_tile = (2 * g + e // 2) * V7X_LANES
            qt = q_ref[sub * WINDOW:(sub + 1) * WINDOW, q_tile:q_tile + V7X_LANES]
            if e % 2 != kv_half:
                qt = pltpu.roll(qt, HEAD_DIM, 1)
            q_rows.append(qt * half_mask[kv_half])
        qs = jnp.concatenate(q_rows, axis=0)
        return lax.dot_general(qs, frame(kp_ref, kc_ref, sub, kv_tile), (((1,), (1,)), ((), ())),
                               preferred_element_type=F32)

    units = [(sub, g) for sub in range(q_ref.shape[0] // WINDOW) for g in range(A_KV_HEADS)]
    s_next = scores(*units[0])
    for idx, (sub, g) in enumerate(units):
        kv_tile = slice((g // 2) * V7X_LANES, (g // 2 + 1) * V7X_LANES)
        kv_half = g % 2
        qrows = slice(sub * WINDOW, (sub + 1) * WINDOW)
        valid = band_first if sub == 0 else band
        s_all = s_next
        if idx + 1 < len(units):
            s_next = scores(*units[idx + 1])
        p_rows = []
        denoms = []
        for e in range(grp):
            s = jnp.where(valid, s_all[e * WINDOW:(e + 1) * WINDOW], NEG_INF)
            sink = sink_ref[grp * g + e]
            m = jnp.maximum(jnp.max(s, axis=-1, keepdims=True), sink)
            p = jnp.exp(s - m)
            denoms.append(jnp.sum(p, axis=-1, keepdims=True) + jnp.exp(sink - m))
            p_rows.append(p.astype(BF16))
        o_all = jnp.dot(jnp.concatenate(p_rows, axis=0), frame(vp_ref, vc_ref, sub, kv_tile),
                        preferred_element_type=F32)
        for u in range(grp // 2):
            o_even = o_all[(2 * u) * WINDOW:(2 * u + 1) * WINDOW] / denoms[2 * u]
            o_odd = o_all[(2 * u + 1) * WINDOW:(2 * u + 2) * WINDOW] / denoms[2 * u + 1]
            if kv_half == 0:
                tile = jnp.where(low_half, o_even, pltpu.roll(o_odd, HEAD_DIM, 1))
            else:
                tile = jnp.where(low_half, pltpu.roll(o_even, HEAD_DIM, 1), o_odd)
            out_tile = (2 * g + u) * V7X_LANES
            o_ref[qrows, out_tile:out_tile + V7X_LANES] = tile.astype(BF16)


SWA_BLOCKS_PER_STEP = 1


def _swa_attention(q_a, k_a, v_a, sinks, cast_weights, *, batch, seq):
    n = q_a.shape[0]
    tq = SWA_BLOCKS_PER_STEP * WINDOW
    nb = seq // tq
    grid = (batch, nb)
    casts = _SideCasts(cast_weights, grid)
    cur = lambda b, i: (b * nb + i, 0)
    prev = lambda b, i: (jnp.maximum((b * nb + i) * SWA_BLOCKS_PER_STEP - 1, 0), 0)
    est = (2 * (2 * tq * A_Q_WIDTH * 2 + 2 * (tq + WINDOW) * A_KV_WIDTH * 2) + 24 * WINDOW * 2 * WINDOW * 4
           + casts.vmem_bytes)
    outs = pl.pallas_call(
        functools.partial(_swa_kernel, cast_groups=casts.group_sizes),
        out_shape=[jax.ShapeDtypeStruct((n, A_Q_WIDTH), BF16)] + casts.out_shapes,
        grid=grid,
        in_specs=[pl.BlockSpec(memory_space=pltpu.SMEM),
                  pl.BlockSpec((tq, A_Q_WIDTH), cur),
                  pl.BlockSpec((WINDOW, A_KV_WIDTH), prev),
                  pl.BlockSpec((tq, A_KV_WIDTH), cur),
                  pl.BlockSpec((WINDOW, A_KV_WIDTH), prev),
                  pl.BlockSpec((tq, A_KV_WIDTH), cur)] + casts.in_specs,
        out_specs=[pl.BlockSpec((tq, A_Q_WIDTH), cur)] + casts.out_specs,
        compiler_params=_params(("arbitrary", "arbitrary"), est),
        name="swa_attention",
    )(sinks, q_a, k_a, k_a, v_a, v_a, *casts.sources)
    return outs[0], outs[1:]


def _split3(x):
    hi = x.astype(BF16)
    r1 = x - hi.astype(F32)
    mid = r1.astype(BF16)
    lo = (r1 - mid.astype(F32)).astype(BF16)
    return hi, mid, lo


GLA_GROUP = 256


def _gla_kernel(*refs, dk, dv, heads_per_step, cast_groups):
    q_ref, k_ref, v_ref, lf_ref, g_ref, nw_ref = refs[:6]
    n_src = sum(cast_groups)
    o_ref = refs[6 + n_src]
    s_scr = refs[-1]
    _copy_side_casts(refs[6:6 + n_src], refs[7 + n_src:-1], cast_groups)

    @pl.when(pl.program_id(2) == 0)
    def _():
        s_scr[...] = jnp.zeros_like(s_scr)

    tb = q_ref.shape[0]
    assert GLA_GROUP == 4 * CHUNK
    row = lax.broadcasted_iota(jnp.int32, (GLA_GROUP, GLA_GROUP), 0)
    col = lax.broadcasted_iota(jnp.int32, (GLA_GROUP, GLA_GROUP), 1)
    rc = row // CHUNK
    cc = col // CHUNK
    same_chunk = (rc == cc) & (row >= col)
    next_chunk = (rc == cc + 1) & (rc % 2 == 1)
    far_chunk = (rc >= 2) & (cc <= 1)
    tril = jnp.where(same_chunk, 1.0, 0.0).astype(BF16)
    nw = nw_ref[...]
    one = jnp.ones((1, dk), F32)
    nt = (((1,), (1,)), ((), ()))

    def by_chunk(vecs):
        return jnp.concatenate([jnp.broadcast_to(v, (CHUNK, dk)) for v in vecs], axis=0)

    def prepare(hh, gi):
        rows = slice(gi * GLA_GROUP, (gi + 1) * GLA_GROUP)
        kcols = slice(hh * dk, (hh + 1) * dk)
        vcols = slice(hh * dv, (hh + 1) * dv)
        hi, mid, lo = _split3(lf_ref[rows, kcols])
        bb = jnp.dot(tril, jnp.concatenate([hi, mid, lo], axis=1), preferred_element_type=F32)
        b = bb[:, :dk] + bb[:, dk:2 * dk] + bb[:, 2 * dk:]
        l0, l1, l2, l3 = [b[(c + 1) * CHUNK - 1:(c + 1) * CHUNK, :] for c in range(4)]
        b_mid = by_chunk([b[c * CHUNK + CHUNK // 2 - 1:c * CHUNK + CHUNK // 2, :] for c in range(4)])
        q = q_ref[rows, kcols].astype(F32)
        k = k_ref[rows, kcols].astype(F32)
        v = v_ref[rows, vcols]
        q_loc = q * jnp.exp(b)
        k_loc = k * jnp.exp(by_chunk([l0, l1, l2, l3]) - b)
        q_i = (q * jnp.exp(b - b_mid)).astype(BF16)
        k_i = (k * jnp.exp(b_mid - b)).astype(BF16)
        q_far = (q_loc * by_chunk([one, one, one, jnp.exp(l2)])).astype(BF16)
        k_far = (k_loc * by_chunk([jnp.exp(l1), one, one, one])).astype(BF16)
        q_grp = (q_loc * by_chunk([one, jnp.exp(l0), jnp.exp(l0 + l1), jnp.exp(l0 + l1 + l2)])).astype(BF16)
        k_grp = (k_loc * by_chunk([jnp.exp(l1 + l2 + l3), jnp.exp(l2 + l3), jnp.exp(l3), one])).astype(BF16)
        a_same = lax.dot_general(q_i, k_i, nt, preferred_element_type=F32)
        a_next = lax.dot_general(q_loc.astype(BF16), k_loc.astype(BF16), nt, preferred_element_type=F32)
        a_far = lax.dot_general(q_far, k_far, nt, preferred_element_type=F32)
        a = jnp.where(same_chunk, a_same, jnp.where(next_chunk, a_next, jnp.where(far_chunk, a_far, 0.0)))
        o_intra = jnp.dot(a.astype(BF16), v, preferred_element_type=F32)
        kv = lax.dot_general(k_grp, v, (((0,), (0,)), ((), ())), preferred_element_type=F32)
        decay = jnp.exp(jnp.transpose(jnp.broadcast_to(l0 + l1 + l2 + l3, (V7X_LANES, dk))))
        return rows, vcols, o_intra, q_grp, kv, decay

    def finish(state, prepared):
        rows, vcols, o_intra, q_grp, kv, decay = prepared
        o = o_intra + jnp.dot(q_grp, state.astype(BF16), preferred_element_type=F32)
        o = o * lax.rsqrt(jnp.mean(o * o, axis=-1, keepdims=True) + RMS_EPS)
        o_ref[rows, vcols] = ((o * nw) * _silu(g_ref[rows, vcols].astype(F32))).astype(BF16)
        return state * jnp.tile(decay, (1, dv // V7X_LANES)) + kv

    units = [(hh, gi) for gi in range(tb // GLA_GROUP) for hh in range(heads_per_step)]
    states = [s_scr[hh] for hh in range(heads_per_step)]
    pending = prepare(*units[0])
    for idx, (hh, gi) in enumerate(units):
        upcoming = prepare(*units[idx + 1]) if idx + 1 < len(units) else None
        states[hh] = finish(states[hh], pending)
        pending = upcoming
    for hh in range(heads_per_step):
        s_scr[hh] = states[hh]


def _gla(q, k, v, log_f, g, norm_w, cast_weights, *, batch, seq, heads, dk, dv, heads_per_step, name):
    n = q.shape[0]
    tb = 1024
    nt = seq // tb
    hp = heads_per_step
    grid = (batch, heads // hp, nt)
    casts = _SideCasts(cast_weights, grid)
    idx = lambda b, h, t: (b * nt + t, h)
    est = (2 * tb * hp * (3 * dk * 4 + dv * 2 + dv * 4 + dv * 2) + hp * dk * dv * 4 + 4 * dk * dv * 4
           + 16 * GLA_GROUP * max(3 * dk, dv) * 4 + casts.vmem_bytes)
    outs = pl.pallas_call(
        functools.partial(_gla_kernel, dk=dk, dv=dv, heads_per_step=hp, cast_groups=casts.group_sizes),
        out_shape=[jax.ShapeDtypeStruct((n, heads * dv), BF16)] + casts.out_shapes,
        grid=grid,
        in_specs=[pl.BlockSpec((tb, hp * dk), idx), pl.BlockSpec((tb, hp * dk), idx),
                  pl.BlockSpec((tb, hp * dv), idx), pl.BlockSpec((tb, hp * dk), idx),
                  pl.BlockSpec((tb, hp * dv), idx),
                  pl.BlockSpec((1, dv), lambda b, h, t: (0, 0))] + casts.in_specs,
        out_specs=[pl.BlockSpec((tb, hp * dv), idx)] + casts.out_specs,
        scratch_shapes=[pltpu.VMEM((hp, dk, dv), F32)],
        compiler_params=_params(("arbitrary", "arbitrary", "arbitrary"), est),
        name=name,
    )(q, k, v, log_f, g, norm_w.reshape(1, dv), *casts.sources)
    return outs[0], outs[1:]


OUTPROJ_SUB_ROWS = 128


def _outproj_ln_kernel(*refs, k_sizes):
    lhs_refs = refs[:len(k_sizes)]
    w_ref, x_ref, gate_ref, g_ref, b_ref, o_ref = refs[len(k_sizes):]

    def project(rows):
        y = None
        off = 0
        for r, ks in zip(lhs_refs, k_sizes):
            part = jnp.dot(r[rows, :], w_ref[off:off + ks, :], preferred_element_type=F32)
            y = part if y is None else y + part
            off += ks
        return y

    subs = [slice(r0, r0 + OUTPROJ_SUB_ROWS) for r0 in range(0, o_ref.shape[0], OUTPROJ_SUB_ROWS)]
    y_next = project(subs[0])
    for idx, rows in enumerate(subs):
        y = y_next
        if idx + 1 < len(subs):
            y_next = project(subs[idx + 1])
        o_ref[rows, :] = _layer_norm(ALPHA * x_ref[rows, :] + gate_ref[...] * y, g_ref[...], b_ref[...])


def _outproj_ln(lhs_list, w_bf, x2, mod3, gate_slot, ln_g, ln_b, *, seq):
    n, d = x2.shape
    tm = 512
    per_seq = seq // tm
    row = lambda i: (i, 0)
    k_sizes = tuple(a.shape[1] for a in lhs_list)
    k_total = sum(k_sizes)
    vec = pl.BlockSpec((1, d), lambda i: (0, 0))
    est = k_total * d * 2 + 2 * tm * k_total * 2 + 4 * tm * d * 4 + 4 * tm * d * 4
    return pl.pallas_call(
        functools.partial(_outproj_ln_kernel, k_sizes=k_sizes),
        out_shape=jax.ShapeDtypeStruct((n, d), F32),
        grid=(n // tm,),
        in_specs=[pl.BlockSpec((tm, ks), row) for ks in k_sizes] + [
            _resident((k_total, d)),
            pl.BlockSpec((tm, d), row),
            pl.BlockSpec((None, 1, d), lambda i: ((i // per_seq) * 6 + gate_slot, 0, 0)),
            vec, vec],
        out_specs=pl.BlockSpec((tm, d), row),
        compiler_params=_params(("arbitrary",), est),
        name="outproj_ln",
    )(*lhs_list, w_bf, x2, mod3, ln_g.reshape(1, d), ln_b.reshape(1, d))


FFN_PASSES = 2
FFN_PASS_TILE = 768
FFN_CARRY = 8


def _ffn_pass_kernel(*refs, per_seq, last):
    x_ref, shift_ref, scale_ref, wu_ref, wv_ref, cw_ref, cb_ref, wd_ref = refs[:8]
    if last:
        part_ref, gate_ref, lng_ref, lnb_ref, o_ref, h_scr, u_scr = refs[8:]
    else:
        o_ref, h_scr, u_scr = refs[8:]
    i = pl.program_id(0)
    tm = x_ref.shape[0]
    width = wd_ref.shape[0]
    h_scr[...] = (x_ref[...] * (1.0 + scale_ref[...]) + shift_ref[...]).astype(BF16)

    @pl.when(i == 0)
    def _():
        u_scr[...] = jnp.zeros_like(u_scr)

    inside_sequence = i % per_seq != 0
    u_scr[0:FFN_CARRY, :] = jnp.where(inside_sequence, u_scr[tm:tm + FFN_CARRY, :], 0.0)
    tiles = [(c0, min(FFN_PASS_TILE, width - c0)) for c0 in range(0, width, FFN_PASS_TILE)]

    def up(t):
        c0, w = tiles[t]
        cols = slice(c0, c0 + w)
        u_scr[FFN_CARRY:, cols] = jnp.dot(h_scr[...], wu_ref[:, cols], preferred_element_type=F32)
        return jnp.dot(h_scr[...], wv_ref[:, cols], preferred_element_type=F32)

    def down(t, v):
        c0, w = tiles[t]
        cols = slice(c0, c0 + w)
        u = (cw_ref[2:3, cols] * u_scr[FFN_CARRY:FFN_CARRY + tm, cols]
             + cw_ref[1:2, cols] * u_scr[FFN_CARRY - 1:FFN_CARRY - 1 + tm, cols]
             + cw_ref[0:1, cols] * u_scr[FFN_CARRY - 2:FFN_CARRY - 2 + tm, cols]) + cb_ref[:, cols]
        act = (_silu(u) * v).astype(BF16)
        return jnp.dot(act, wd_ref[cols, :], preferred_element_type=F32)

    acc = None
    v_next = up(0)
    for t in range(len(tiles)):
        v = v_next
        if t + 1 < len(tiles):
            v_next = up(t + 1)
        part = down(t, v)
        acc = part if acc is None else acc + part
    if last:
        y = acc + part_ref[...]
        o_ref[...] = _layer_norm(ALPHA * x_ref[...] + gate_ref[...] * y, lng_ref[...], lnb_ref[...])
    else:
        o_ref[...] = acc


def _ffn_two_pass(x2, mod3, w_up_bf, conv_w, conv_b, w_down_bf, ln_g, ln_b, *, seq):
    assert FFN_PASSES == 2
    n, d = x2.shape
    tm = 256
    per_seq = seq // tm
    width = D_FF // FFN_PASSES
    row = lambda i: (i, 0)
    slot = lambda s: (lambda i: ((i // per_seq) * 6 + s, 0, 0))
    vec = pl.BlockSpec((1, d), lambda i: (0, 0))
    conv_b2 = conv_b.reshape(1, D_FF)
    est = (3 * d * width * 2 + 6 * tm * d * 4 + tm * d * 2 + (tm + FFN_CARRY) * width * 4
           + 3 * tm * d * 4 + 8 * tm * FFN_PASS_TILE * 4)
    partial = None
    for p in range(FFN_PASSES):
        last = p == FFN_PASSES - 1
        in_specs = [pl.BlockSpec((tm, d), row),
                    pl.BlockSpec((None, 1, d), slot(3)),
                    pl.BlockSpec((None, 1, d), slot(4)),
                    pl.BlockSpec((d, width), lambda i, p=p: (0, p), pipeline_mode=pl.Buffered(1)),
                    pl.BlockSpec((d, width), lambda i, p=p: (0, FFN_PASSES + p), pipeline_mode=pl.Buffered(1)),
                    pl.BlockSpec((CONV_WIDTH, width), lambda i, p=p: (0, p)),
                    pl.BlockSpec((1, width), lambda i, p=p: (0, p)),
                    pl.BlockSpec((width, d), lambda i, p=p: (p, 0), pipeline_mode=pl.Buffered(1))]
        args = [x2, mod3, mod3, w_up_bf, w_up_bf, conv_w, conv_b2, w_down_bf]
        if partial is not None:
            in_specs.append(pl.BlockSpec((tm, d), row))
            args.append(partial)
        if last:
            in_specs += [pl.BlockSpec((None, 1, d), slot(5)), vec, vec]
            args += [mod3, ln_g.reshape(1, d), ln_b.reshape(1, d)]
        partial = pl.pallas_call(
            functools.partial(_ffn_pass_kernel, per_seq=per_seq, last=last),
            out_shape=jax.ShapeDtypeStruct((n, d), F32),
            grid=(n // tm,),
            in_specs=in_specs,
            out_specs=pl.BlockSpec((tm, d), row),
            scratch_shapes=[pltpu.VMEM((tm, d), BF16), pltpu.VMEM((tm + FFN_CARRY, width), F32)],
            compiler_params=_params(("arbitrary",), est),
            name="ffn_last" if last else "ffn_part",
        )(*args)
    return partial


def _inproj_odd_kernel(x_ref, shift_ref, scale_ref, w_ref, wgb_ref, bg_ref,
                       q_ref, k_ref, v_ref, g_ref, lf_ref, h_scr):
    h_scr[...] = (x_ref[...] * (1.0 + scale_ref[...]) + shift_ref[...]).astype(BF16)

    def proj(c0, width=PROJ_COLS):
        return jnp.dot(h_scr[...], w_ref[:, c0:c0 + width], preferred_element_type=F32)

    low = proj(2 * C_KEY_DIM + 2 * C_VALUE_DIM, RANK_PAD).astype(BF16)

    def log_decay(c0, width):
        sl = slice(c0, c0 + width)
        gk = jnp.dot(low, wgb_ref[:, sl], preferred_element_type=F32) + bg_ref[:, sl]
        lf_ref[:, sl] = jax.nn.log_sigmoid(gk) * (1.0 / GATE_NORMALIZER)

    piece = PROJ_COLS // 2
    for c0 in range(0, C_KEY_DIM, PROJ_COLS):
        sl = slice(c0, c0 + PROJ_COLS)
        q_ref[:, sl] = (proj(c0) * (C_DK ** -0.5)).astype(BF16)
        log_decay(c0, piece)
        k_ref[:, sl] = proj(C_KEY_DIM + c0).astype(BF16)
        log_decay(c0 + piece, piece)
    for c0 in range(0, C_VALUE_DIM, PROJ_COLS):
        sl = slice(c0, c0 + PROJ_COLS)
        v_ref[:, sl] = proj(2 * C_KEY_DIM + c0).astype(BF16)
        g_ref[:, sl] = proj(2 * C_KEY_DIM + C_VALUE_DIM + c0).astype(BF16)


def _inproj_odd(x2, mod3, w_bf, w_gk_b_bf, b_gk, *, seq):
    n, d = x2.shape
    tm = 512
    per_seq = seq // tm
    width = w_bf.shape[1]
    row = lambda i: (i, 0)
    out_shapes = (
        jax.ShapeDtypeStruct((n, C_KEY_DIM), BF16), jax.ShapeDtypeStruct((n, C_KEY_DIM), BF16),
        jax.ShapeDtypeStruct((n, C_VALUE_DIM), BF16), jax.ShapeDtypeStruct((n, C_VALUE_DIM), BF16),
        jax.ShapeDtypeStruct((n, C_KEY_DIM), F32))
    out_bytes = sum(tm * s.shape[1] * s.dtype.itemsize for s in out_shapes)
    est = d * width * 2 + 2 * tm * d * 4 + 2 * out_bytes + tm * d * 2 + 6 * tm * PROJ_COLS * 4
    return pl.pallas_call(
        _inproj_odd_kernel,
        out_shape=out_shapes,
        grid=(n // tm,),
        in_specs=[pl.BlockSpec((tm, d), row),
                  pl.BlockSpec((None, 1, d), lambda i: ((i // per_seq) * 6 + 0, 0, 0)),
                  pl.BlockSpec((None, 1, d), lambda i: ((i // per_seq) * 6 + 1, 0, 0)),
                  _resident((d, width)),
                  _resident((RANK_PAD, C_KEY_DIM)),
                  pl.BlockSpec((1, C_KEY_DIM), lambda i: (0, 0))],
        out_specs=tuple(pl.BlockSpec((tm, s.shape[1]), row) for s in out_shapes),
        scratch_shapes=[pltpu.VMEM((tm, d), BF16)],
        compiler_params=_params(("arbitrary",), est),
        name="inproj_odd",
    )(x2, mod3, mod3, w_bf, w_gk_b_bf, b_gk.reshape(1, C_KEY_DIM))


def kernel(x, c, positions,
           ada_w0, ada_b0, mix_w_in0, mix_w_out0, attn_sinks0, hgrn_lb_logits, hgrn_norm_w0,
           ln_mix_g0, ln_mix_b0, ffn_w_up0, ffn_conv_w0, ffn_conv_b0, ffn_w_down0, ln_ffn_g0, ln_ffn_b0,
           ada_w1, ada_b1, mix_w_in1, gla_w_gk_a1, gla_w_gk_b1, gla_b_gk1, gla_norm_w1, mix_w_out1,
           ln_mix_g1, ln_mix_b1, ffn_w_up1, ffn_conv_w1, ffn_conv_b1, ffn_w_down1, ln_ffn_g1, ln_ffn_b1):
    batch, seq, d = x.shape
    n = batch * seq
    x2 = x.reshape(n, d)
    c_pad = jnp.pad(c, ((0, 8 - batch), (0, 0)))

    def modulation(ada_w, ada_b):
        mod = _ada_modulation(c_pad, ada_w, ada_b)[:batch]
        return mod.reshape(batch * 6, 1, d)

    mod3 = modulation(ada_w0, ada_b0)
    cos, sin_signed, (w_in0,) = _rope_tables(positions.reshape(n, 1), [mix_w_in0])
    q_a, k_a, v_a, q_b, k_b, lf_b, i_b, g_b = _inproj_even(
        x2, mod3, w_in0, cos, sin_signed, hgrn_lb_logits, seq=seq, layer=0)
    w_gk_a = jnp.pad(gla_w_gk_a1, ((0, 0), (0, RANK_PAD - GATE_RANK)))
    o_a, (w_out0, w_in1, w_out1) = _swa_attention(
        q_a, k_a, v_a, attn_sinks0, [mix_w_out0, (mix_w_in1, w_gk_a), mix_w_out1], batch=batch, seq=seq)
    o_b, (w_up0, w_down0) = _gla(
        q_b, k_b, i_b, lf_b, g_b, hgrn_norm_w0, [ffn_w_up0, ffn_w_down0], batch=batch, seq=seq,
        heads=B_HEADS, dk=B_HEAD_DIM, dv=B_HEAD_DIM, heads_per_step=4, name="hgrn2")
    x2 = _outproj_ln([o_a, o_b], w_out0, x2, mod3, 2, ln_mix_g0, ln_mix_b0, seq=seq)
    x2 = _ffn_two_pass(x2, mod3, w_up0, ffn_conv_w0, ffn_conv_b0, w_down0, ln_ffn_g0, ln_ffn_b0, seq=seq)

    mod3 = modulation(ada_w1, ada_b1)
    w_gk_b = jnp.pad(gla_w_gk_b1, ((0, RANK_PAD - GATE_RANK), (0, 0))).astype(BF16)
    q_c, k_c, v_c, g_c, lf_c = _inproj_odd(x2, mod3, w_in1, w_gk_b, gla_b_gk1, seq=seq)
    o_c, (w_up1, w_down1) = _gla(
        q_c, k_c, v_c, lf_c, g_c, gla_norm_w1, [ffn_w_up1, ffn_w_down1], batch=batch, seq=seq,
        heads=C_HEADS, dk=C_DK, dv=C_DV, heads_per_step=1, name="gla")
    x2 = _outproj_ln([o_c], w_out1, x2, mod3, 2, ln_mix_g1, ln_mix_b1, seq=seq)
    x2 = _ffn_two_pass(x2, mod3, w_up1, ffn_conv_w1, ffn_conv_b1, w_down1, ln_ffn_g1, ln_ffn_b1, seq=seq)
    return x2.reshape(batch, seq, d)
```

```python
import functools
import math

import jax
import jax.numpy as jnp
from jax import lax
from jax.experimental import pallas as pl
from jax.experimental.pallas import tpu as pltpu

F32 = jnp.float32
BF16 = jnp.bfloat16

D_MODEL = 2048
DEPTH = 2
HEAD_DIM = 64
A_Q_HEADS = 16
A_KV_HEADS = 4
A_Q_WIDTH = A_Q_HEADS * HEAD_DIM
A_KV_WIDTH = A_KV_HEADS * HEAD_DIM
WINDOW = 128
ROPE_THETA = 10000.0
B_HEADS = 8
B_HEAD_DIM = 128
B_WIDTH = B_HEADS * B_HEAD_DIM
EVEN_IN_WIDTH = A_Q_WIDTH + 2 * A_KV_WIDTH + 4 * B_WIDTH
C_HEADS = 4
C_KEY_DIM = D_MODEL // 2
C_VALUE_DIM = D_MODEL
C_DK = C_KEY_DIM // C_HEADS
C_DV = C_VALUE_DIM // C_HEADS
GATE_RANK = 16
GATE_NORMALIZER = 16.0
CHUNK = 64
D_FF = 5632
CONV_WIDTH = 3
LN_EPS = 1e-5
RMS_EPS = 1e-6
ALPHA = (2.0 * DEPTH) ** 0.25
NEG_INF = -1e30

V7X_LANES = 128
V7X_VMEM_BYTES = 64 * 1024 * 1024
V7X_VMEM_CEILING = 56 * 1024 * 1024
BF16_SUBLANE_PACK = 16

RANK_PAD = V7X_LANES


def _params(semantics, vmem_estimate):
    limit = min(V7X_VMEM_CEILING, max(16 * 1024 * 1024, int(vmem_estimate * 1.25)))
    return pltpu.CompilerParams(dimension_semantics=semantics, vmem_limit_bytes=limit)


def _resident(shape):
    return pl.BlockSpec(shape, lambda *_: (0,) * len(shape), pipeline_mode=pl.Buffered(1))


def _silu(x):
    return x * jax.nn.sigmoid(x)


class _SideCasts:
    def __init__(self, weights, grid):
        self.groups = [w if isinstance(w, (tuple, list)) else (w,) for w in weights]
        self.sources = [m for group in self.groups for m in group]
        self.group_sizes = tuple(len(group) for group in self.groups)
        steps = math.prod(grid)
        strides = [math.prod(grid[i + 1:]) for i in range(len(grid))]
        self.in_specs, self.out_specs, self.out_shapes = [], [], []
        self.vmem_bytes = 0
        for group in self.groups:
            rows = group[0].shape[0]
            cols = sum(m.shape[1] for m in group)
            share = 1
            while (rows * share) % (steps * BF16_SUBLANE_PACK):
                share *= 2
            block_rows = rows * share // steps

            def index(*ids, share=share):
                return (sum(i * s for i, s in zip(ids, strides)) // share, 0)

            self.in_specs += [pl.BlockSpec((block_rows, m.shape[1]), index) for m in group]
            self.out_specs.append(pl.BlockSpec((block_rows, cols), index))
            self.out_shapes.append(jax.ShapeDtypeStruct((rows, cols), BF16))
            self.vmem_bytes += 2 * block_rows * cols * (4 + 2)


def _copy_side_casts(src_refs, dst_refs, group_sizes):
    src_refs = list(src_refs)
    for dst, size in zip(dst_refs, group_sizes):
        col = 0
        for src in src_refs[:size]:
            dst[:, col:col + src.shape[1]] = src[...].astype(BF16)
            col += src.shape[1]
        src_refs = src_refs[size:]


def _layer_norm(z, g, b):
    mu = jnp.mean(z, axis=-1, keepdims=True)
    zc = z - mu
    var = jnp.mean(zc * zc, axis=-1, keepdims=True)
    return zc * lax.rsqrt(var + LN_EPS) * g + b


def _ada_kernel(c_ref, w_ref, b_ref, o_ref):
    a = _silu(c_ref[...]).astype(BF16)
    o_ref[...] = jnp.dot(a, w_ref[...].astype(BF16), preferred_element_type=F32) + b_ref[...]


def _ada_modulation(c_pad, ada_w, ada_b):
    rows, d = c_pad.shape
    n = ada_w.shape[1]
    tn = 1024
    est = 2 * d * tn * 4 + d * tn * 2 + 4 * rows * n
    return pl.pallas_call(
        _ada_kernel,
        out_shape=jax.ShapeDtypeStruct((rows, n), F32),
        grid=(n // tn,),
        in_specs=[pl.BlockSpec((rows, d), lambda j: (0, 0)),
                  pl.BlockSpec((d, tn), lambda j: (0, j)),
                  pl.BlockSpec((1, tn), lambda j: (0, j))],
        out_specs=pl.BlockSpec((rows, tn), lambda j: (0, j)),
        compiler_params=_params(("arbitrary",), est),
        name="ada_modulation",
    )(c_pad, ada_w, ada_b.reshape(1, n))


def _rope_table_kernel(*refs, cast_groups):
    pos_ref, invf_ref, sign_ref = refs[:3]
    n_src = sum(cast_groups)
    cos_ref, sin_ref = refs[3 + n_src:5 + n_src]
    _copy_side_casts(refs[3:3 + n_src], refs[5 + n_src:], cast_groups)
    ang = pos_ref[...].astype(F32) * invf_ref[...]
    cos_ref[...] = jnp.cos(ang)
    sin_ref[...] = jnp.sin(ang) * sign_ref[...]


def _rope_tables(pos_col, cast_weights):
    n = pos_col.shape[0]
    half = HEAD_DIM // 2
    lane = jnp.arange(V7X_LANES)
    inv_freq = ROPE_THETA ** (-jnp.arange(half, dtype=F32) / half)
    invf = inv_freq[lane % half].reshape(1, V7X_LANES)
    sign = jnp.where((lane % HEAD_DIM) < half, -1.0, 1.0).astype(F32).reshape(1, V7X_LANES)
    tm = 2048
    grid = (n // tm,)
    casts = _SideCasts(cast_weights, grid)
    est = 2 * (tm * V7X_LANES * 4) * 3 + casts.vmem_bytes
    outs = pl.pallas_call(
        functools.partial(_rope_table_kernel, cast_groups=casts.group_sizes),
        out_shape=[jax.ShapeDtypeStruct((n, V7X_LANES), F32)] * 2 + casts.out_shapes,
        grid=grid,
        in_specs=[pl.BlockSpec((tm, 1), lambda i: (i, 0)),
                  pl.BlockSpec((1, V7X_LANES), lambda i: (0, 0)),
                  pl.BlockSpec((1, V7X_LANES), lambda i: (0, 0))] + casts.in_specs,
        out_specs=[pl.BlockSpec((tm, V7X_LANES), lambda i: (i, 0))] * 2 + casts.out_specs,
        compiler_params=_params(("arbitrary",), est),
        name="rope_tables",
    )(pos_col, invf, sign, *casts.sources)
    return outs[0], outs[1], outs[2:]


def _rope(a, cos, sin_signed, first_half):
    outs = []
    for s in range(a.shape[1] // V7X_LANES):
        blk = a[:, s * V7X_LANES:(s + 1) * V7X_LANES]
        partner = jnp.where(first_half,
                            pltpu.roll(blk, V7X_LANES - HEAD_DIM // 2, 1),
                            pltpu.roll(blk, HEAD_DIM // 2, 1))
        outs.append(blk * cos + partner * sin_signed)
    return jnp.concatenate(outs, axis=1) if len(outs) > 1 else outs[0]


PROJ_COLS = 1024


def _inproj_even_kernel(x_ref, shift_ref, scale_ref, w_ref, cos_ref, sin_ref, lbl_ref,
                        qa_ref, ka_ref, va_ref, qb_ref, kb_ref, lf_ref, ib_ref, gb_ref, h_scr, *, layer):
    tm = x_ref.shape[0]
    h_scr[...] = (x_ref[...] * (1.0 + scale_ref[...]) + shift_ref[...]).astype(BF16)

    def proj(c0, width=PROJ_COLS):
        return jnp.dot(h_scr[...], w_ref[:, c0:c0 + width], preferred_element_type=F32)

    cos = cos_ref[...]
    sin_signed = sin_ref[...]
    lane = lax.broadcasted_iota(jnp.int32, (tm, V7X_LANES), 1)
    first_half = (lane % HEAD_DIM) < (HEAD_DIM // 2)

    for c0 in range(0, A_Q_WIDTH, PROJ_COLS):
        qa_ref[:, c0:c0 + PROJ_COLS] = (_rope(proj(c0), cos, sin_signed, first_half)
                                        * (HEAD_DIM ** -0.5)).astype(BF16)
    kv = proj(A_Q_WIDTH, 2 * A_KV_WIDTH)
    ka_ref[...] = _rope(kv[:, :A_KV_WIDTH], cos, sin_signed, first_half).astype(BF16)
    va_ref[...] = kv[:, A_KV_WIDTH:].astype(BF16)

    base = A_Q_WIDTH + 2 * A_KV_WIDTH
    lg = lbl_ref[...]
    e = jnp.exp(lg - jnp.max(lg, axis=0, keepdims=True))
    sm = e / jnp.sum(e, axis=0, keepdims=True)
    lb = jnp.sum(sm[0:layer + 1], axis=0, keepdims=True)
    for c0 in range(0, B_WIDTH, PROJ_COLS):
        sl = slice(c0, c0 + PROJ_COLS)
        qb_ref[:, sl] = proj(base + c0).astype(BF16)
        lbc = lb[:, sl]
        fg = lbc + (1.0 - lbc) * jax.nn.sigmoid(proj(base + B_WIDTH + c0))
        kb_ref[:, sl] = (1.0 - fg).astype(BF16)
        lf_ref[:, sl] = jnp.log(fg)
        ib_ref[:, sl] = proj(base + 2 * B_WIDTH + c0).astype(BF16)
        gb_ref[:, sl] = proj(base + 3 * B_WIDTH + c0).astype(BF16)


def _inproj_even(x2, mod3, w_bf, cos, sin_signed, lb_logits, *, seq, layer):
    n, d = x2.shape
    tm = 512
    per_seq = seq // tm
    width = w_bf.shape[1]
    row = lambda i: (i, 0)
    out_shapes = (
        jax.ShapeDtypeStruct((n, A_Q_WIDTH), BF16), jax.ShapeDtypeStruct((n, A_KV_WIDTH), BF16),
        jax.ShapeDtypeStruct((n, A_KV_WIDTH), BF16), jax.ShapeDtypeStruct((n, B_WIDTH), BF16),
        jax.ShapeDtypeStruct((n, B_WIDTH), BF16), jax.ShapeDtypeStruct((n, B_WIDTH), F32),
        jax.ShapeDtypeStruct((n, B_WIDTH), BF16), jax.ShapeDtypeStruct((n, B_WIDTH), BF16))
    out_bytes = sum(tm * s.shape[1] * s.dtype.itemsize for s in out_shapes)
    est = d * width * 2 + 2 * tm * d * 4 + 2 * out_bytes + tm * d * 2 + 6 * tm * PROJ_COLS * 4
    return pl.pallas_call(
        functools.partial(_inproj_even_kernel, layer=layer),
        out_shape=out_shapes,
        grid=(n // tm,),
        in_specs=[pl.BlockSpec((tm, d), row),
                  pl.BlockSpec((None, 1, d), lambda i: ((i // per_seq) * 6 + 0, 0, 0)),
                  pl.BlockSpec((None, 1, d), lambda i: ((i // per_seq) * 6 + 1, 0, 0)),
                  _resident((d, width)),
                  pl.BlockSpec((tm, V7X_LANES), row),
                  pl.BlockSpec((tm, V7X_LANES), row),
                  pl.BlockSpec(lb_logits.shape, lambda i: (0, 0))],
        out_specs=tuple(pl.BlockSpec((tm, s.shape[1]), row) for s in out_shapes),
        scratch_shapes=[pltpu.VMEM((tm, d), BF16)],
        compiler_params=_params(("arbitrary",), est),
        name="inproj_even",
    )(x2, mod3, mod3, w_bf, cos, sin_signed, lb_logits)


def _swa_kernel(*refs, cast_groups):
    sink_ref, q_ref, kp_ref, kc_ref, vp_ref, vc_ref = refs[:6]
    n_src = sum(cast_groups)
    o_ref = refs[6 + n_src]
    _copy_side_casts(refs[6:6 + n_src], refs[7 + n_src:], cast_groups)
    first_step = pl.program_id(1) == 0
    grp = A_Q_HEADS // A_KV_HEADS
    assert HEAD_DIM * 2 == V7X_LANES and grp == 4
    r = lax.broadcasted_iota(jnp.int32, (WINDOW, 2 * WINDOW), 0)
    c = lax.broadcasted_iota(jnp.int32, (WINDOW, 2 * WINDOW), 1)
    rel = r + WINDOW - c
    band = (rel >= 0) & (rel < WINDOW)
    band_first = band & ((c >= WINDOW) | jnp.logical_not(first_step))
    lane = lax.broadcasted_iota(jnp.int32, (1, V7X_LANES), 1)
    half_mask = [jnp.where(lane < HEAD_DIM, 1.0, 0.0).astype(BF16),
                 jnp.where(lane >= HEAD_DIM, 1.0, 0.0).astype(BF16)]
    low_half = lax.broadcasted_iota(jnp.int32, (WINDOW, V7X_LANES), 1) < HEAD_DIM

    def frame(prev_ref, cur_ref, sub, kv_tile):
        own = cur_ref[sub * WINDOW:(sub + 1) * WINDOW, kv_tile]
        before = prev_ref[:, kv_tile] if sub == 0 else cur_ref[(sub - 1) * WINDOW:sub * WINDOW, kv_tile]
        return jnp.concatenate([before, own], axis=0)

    def scores(sub, g):
        kv_tile = slice((g // 2) * V7X_LANES, (g // 2 + 1) * V7X_LANES)
        kv_half = g % 2
        q_rows = []
        for e in range(grp):
            q_tile = (2 * g + e // 2) * V7X_LANES
            qt = q_ref[sub * WINDOW:(sub + 1) * WINDOW, q_tile:q_tile + V7X_LANES]
            if e % 2 != kv_half:
                qt = pltpu.roll(qt, HEAD_DIM, 1)
            q_rows.append(qt * half_mask[kv_half])
        qs = jnp.concatenate(q_rows, axis=0)
        return lax.dot_general(qs, frame(kp_ref, kc_ref, sub, kv_tile), (((1,), (1,)), ((), ())),
                               preferred_element_type=F32)

    units = [(sub, g) for sub in range(q_ref.shape[0] // WINDOW) for g in range(A_KV_HEADS)]
    s_next = scores(*units[0])
    for idx, (sub, g) in enumerate(units):
        kv_tile = slice((g // 2) * V7X_LANES, (g // 2 + 1) * V7X_LANES)
        kv_half = g % 2
        qrows = slice(sub * WINDOW, (sub + 1) * WINDOW)
        valid = band_first if sub == 0 else band
        s_all = s_next
        if idx + 1 < len(units):
            s_next = scores(*units[idx + 1])
        p_rows = []
        denoms = []
        for e in range(grp):
            s = jnp.where(valid, s_all[e * WINDOW:(e + 1) * WINDOW], NEG_INF)
            sink = sink_ref[grp * g + e]
            m = jnp.maximum(jnp.max(s, axis=-1, keepdims=True), sink)
            p = jnp.exp(s - m)
            denoms.append(jnp.sum(p, axis=-1, keepdims=True) + jnp.exp(sink - m))
            p_rows.append(p.astype(BF16))
        o_all = jnp.dot(jnp.concatenate(p_rows, axis=0), frame(vp_ref, vc_ref, sub, kv_tile),
                        preferred_element_type=F32)
        for u in range(grp // 2):
            o_even = o_all[(2 * u) * WINDOW:(2 * u + 1) * WINDOW] / denoms[2 * u]
            o_odd = o_all[(2 * u + 1) * WINDOW:(2 * u + 2) * WINDOW] / denoms[2 * u + 1]
            if kv_half == 0:
                tile = jnp.where(low_half, o_even, pltpu.roll(o_odd, HEAD_DIM, 1))
            else:
                tile = jnp.where(low_half, pltpu.roll(o_even, HEAD_DIM, 1), o_odd)
            out_tile = (2 * g + u) * V7X_LANES
            o_ref[qrows, out_tile:out_tile + V7X_LANES] = tile.astype(BF16)


SWA_BLOCKS_PER_STEP = 1


def _swa_attention(q_a, k_a, v_a, sinks, cast_weights, *, batch, seq):
    n = q_a.shape[0]
    tq = SWA_BLOCKS_PER_STEP * WINDOW
    nb = seq // tq
    grid = (batch, nb)
    casts = _SideCasts(cast_weights, grid)
    cur = lambda b, i: (b * nb + i, 0)
    prev = lambda b, i: (jnp.maximum((b * nb + i) * SWA_BLOCKS_PER_STEP - 1, 0), 0)
    est = (2 * (2 * tq * A_Q_WIDTH * 2 + 2 * (tq + WINDOW) * A_KV_WIDTH * 2) + 24 * WINDOW * 2 * WINDOW * 4
           + casts.vmem_bytes)
    outs = pl.pallas_call(
        functools.partial(_swa_kernel, cast_groups=casts.group_sizes),
        out_shape=[jax.ShapeDtypeStruct((n, A_Q_WIDTH), BF16)] + casts.out_shapes,
        grid=grid,
        in_specs=[pl.BlockSpec(memory_space=pltpu.SMEM),
                  pl.BlockSpec((tq, A_Q_WIDTH), cur),
                  pl.BlockSpec((WINDOW, A_KV_WIDTH), prev),
                  pl.BlockSpec((tq, A_KV_WIDTH), cur),
                  pl.BlockSpec((WINDOW, A_KV_WIDTH), prev),
                  pl.BlockSpec((tq, A_KV_WIDTH), cur)] + casts.in_specs,
        out_specs=[pl.BlockSpec((tq, A_Q_WIDTH), cur)] + casts.out_specs,
        compiler_params=_params(("arbitrary", "arbitrary"), est),
        name="swa_attention",
    )(sinks, q_a, k_a, k_a, v_a, v_a, *casts.sources)
    return outs[0], outs[1:]


def _split3(x):
    hi = x.astype(BF16)
    r1 = x - hi.astype(F32)
    mid = r1.astype(BF16)
    lo = (r1 - mid.astype(F32)).astype(BF16)
    return hi, mid, lo


GLA_GROUP = 256


def _gla_kernel(*refs, dk, dv, heads_per_step, cast_groups):
    q_ref, k_ref, v_ref, lf_ref, g_ref, nw_ref = refs[:6]
    n_src = sum(cast_groups)
    o_ref = refs[6 + n_src]
    s_scr = refs[-1]
    _copy_side_casts(refs[6:6 + n_src], refs[7 + n_src:-1], cast_groups)

    @pl.when(pl.program_id(2) == 0)
    def _():
        s_scr[...] = jnp.zeros_like(s_scr)

    tb = q_ref.shape[0]
    assert GLA_GROUP == 4 * CHUNK
    row = lax.broadcasted_iota(jnp.int32, (GLA_GROUP, GLA_GROUP), 0)
    col = lax.broadcasted_iota(jnp.int32, (GLA_GROUP, GLA_GROUP), 1)
    rc = row // CHUNK
    cc = col // CHUNK
    same_chunk = (rc == cc) & (row >= col)
    next_chunk = (rc == cc + 1) & (rc % 2 == 1)
    far_chunk = (rc >= 2) & (cc <= 1)
    tril = jnp.where(same_chunk, 1.0, 0.0).astype(BF16)
    nw = nw_ref[...]
    one = jnp.ones((1, dk), F32)
    nt = (((1,), (1,)), ((), ()))

    def by_chunk(vecs):
        return jnp.concatenate([jnp.broadcast_to(v, (CHUNK, dk)) for v in vecs], axis=0)

    def prepare(hh, gi):
        rows = slice(gi * GLA_GROUP, (gi + 1) * GLA_GROUP)
        kcols = slice(hh * dk, (hh + 1) * dk)
        vcols = slice(hh * dv, (hh + 1) * dv)
        hi, mid, lo = _split3(lf_ref[rows, kcols])
        bb = jnp.dot(tril, jnp.concatenate([hi, mid, lo], axis=1), preferred_element_type=F32)
        b = bb[:, :dk] + bb[:, dk:2 * dk] + bb[:, 2 * dk:]
        l0, l1, l2, l3 = [b[(c + 1) * CHUNK - 1:(c + 1) * CHUNK, :] for c in range(4)]
        b_mid = by_chunk([b[c * CHUNK + CHUNK // 2 - 1:c * CHUNK + CHUNK // 2, :] for c in range(4)])
        q = q_ref[rows, kcols].astype(F32)
        k = k_ref[rows, kcols].astype(F32)
        v = v_ref[rows, vcols]
        q_loc = q * jnp.exp(b)
        k_loc = k * jnp.exp(by_chunk([l0, l1, l2, l3]) - b)
        q_i = (q * jnp.exp(b - b_mid)).astype(BF16)
        k_i = (k * jnp.exp(b_mid - b)).astype(BF16)
        q_far = (q_loc * by_chunk([one, one, one, jnp.exp(l2)])).astype(BF16)
        k_far = (k_loc * by_chunk([jnp.exp(l1), one, one, one])).astype(BF16)
        q_grp = (q_loc * by_chunk([one, jnp.exp(l0), jnp.exp(l0 + l1), jnp.exp(l0 + l1 + l2)])).astype(BF16)
        k_grp = (k_loc * by_chunk([jnp.exp(l1 + l2 + l3), jnp.exp(l2 + l3), jnp.exp(l3), one])).astype(BF16)
        a_same = lax.dot_general(q_i, k_i, nt, preferred_element_type=F32)
        a_next = lax.dot_general(q_loc.astype(BF16), k_loc.astype(BF16), nt, preferred_element_type=F32)
        a_far = lax.dot_general(q_far, k_far, nt, preferred_element_type=F32)
        a = jnp.where(same_chunk, a_same, jnp.where(next_chunk, a_next, jnp.where(far_chunk, a_far, 0.0)))
        o_intra = jnp.dot(a.astype(BF16), v, preferred_element_type=F32)
        kv = lax.dot_general(k_grp, v, (((0,), (0,)), ((), ())), preferred_element_type=F32)
        decay = jnp.exp(jnp.transpose(jnp.broadcast_to(l0 + l1 + l2 + l3, (V7X_LANES, dk))))
        return rows, vcols, o_intra, q_grp, kv, decay

    def finish(state, prepared):
        rows, vcols, o_intra, q_grp, kv, decay = prepared
        o = o_intra + jnp.dot(q_grp, state.astype(BF16), preferred_element_type=F32)
        o = o * lax.rsqrt(jnp.mean(o * o, axis=-1, keepdims=True) + RMS_EPS)
        o_ref[rows, vcols] = ((o * nw) * _silu(g_ref[rows, vcols].astype(F32))).astype(BF16)
        return state * jnp.tile(decay, (1, dv // V7X_LANES)) + kv

    units = [(hh, gi) for gi in range(tb // GLA_GROUP) for hh in range(heads_per_step)]
    states = [s_scr[hh] for hh in range(heads_per_step)]
    pending = prepare(*units[0])
    for idx, (hh, gi) in enumerate(units):
        upcoming = prepare(*units[idx + 1]) if idx + 1 < len(units) else None
        states[hh] = finish(states[hh], pending)
        pending = upcoming
    for hh in range(heads_per_step):
        s_scr[hh] = states[hh]


def _gla(q, k, v, log_f, g, norm_w, cast_weights, *, batch, seq, heads, dk, dv, heads_per_step, name):
    n = q.shape[0]
    tb = 1024
    nt = seq // tb
    hp = heads_per_step
    grid = (batch, heads // hp, nt)
    casts = _SideCasts(cast_weights, grid)
    idx = lambda b, h, t: (b * nt + t, h)
    est = (2 * tb * hp * (3 * dk * 4 + dv * 2 + dv * 4 + dv * 2) + hp * dk * dv * 4 + 4 * dk * dv * 4
           + 16 * GLA_GROUP * max(3 * dk, dv) * 4 + casts.vmem_bytes)
    outs = pl.pallas_call(
        functools.partial(_gla_kernel, dk=dk, dv=dv, heads_per_step=hp, cast_groups=casts.group_sizes),
        out_shape=[jax.ShapeDtypeStruct((n, heads * dv), BF16)] + casts.out_shapes,
        grid=grid,
        in_specs=[pl.BlockSpec((tb, hp * dk), idx), pl.BlockSpec((tb, hp * dk), idx),
                  pl.BlockSpec((tb, hp * dv), idx), pl.BlockSpec((tb, hp * dk), idx),
                  pl.BlockSpec((tb, hp * dv), idx),
                  pl.BlockSpec((1, dv), lambda b, h, t: (0, 0))] + casts.in_specs,
        out_specs=[pl.BlockSpec((tb, hp * dv), idx)] + casts.out_specs,
        scratch_shapes=[pltpu.VMEM((hp, dk, dv), F32)],
        compiler_params=_params(("arbitrary", "arbitrary", "arbitrary"), est),
        name=name,
    )(q, k, v, log_f, g, norm_w.reshape(1, dv), *casts.sources)
    return outs[0], outs[1:]


OUTPROJ_SUB_ROWS = 256


def _outproj_ln_kernel(*refs, k_sizes):
    lhs_refs = refs[:len(k_sizes)]
    w_ref, x_ref, gate_ref, g_ref, b_ref, o_ref = refs[len(k_sizes):]

    def project(rows):
        y = None
        off = 0
        for r, ks in zip(lhs_refs, k_sizes):
            part = jnp.dot(r[rows, :], w_ref[off:off + ks, :], preferred_element_type=F32)
            y = part if y is None else y + part
            off += ks
        return y

    subs = [slice(r0, r0 + OUTPROJ_SUB_ROWS) for r0 in range(0, o_ref.shape[0], OUTPROJ_SUB_ROWS)]
    y_next = project(subs[0])
    for idx, rows in enumerate(subs):
        y = y_next
        if idx + 1 < len(subs):
            y_next = project(subs[idx + 1])
        o_ref[rows, :] = _layer_norm(ALPHA * x_ref[rows, :] + gate_ref[...] * y, g_ref[...], b_ref[...])


def _outproj_ln(lhs_list, w_bf, x2, mod3, gate_slot, ln_g, ln_b, *, seq):
    n, d = x2.shape
    tm = 512
    per_seq = seq // tm
    row = lambda i: (i, 0)
    k_sizes = tuple(a.shape[1] for a in lhs_list)
    k_total = sum(k_sizes)
    vec = pl.BlockSpec((1, d), lambda i: (0, 0))
    est = k_total * d * 2 + 2 * tm * k_total * 2 + 4 * tm * d * 4 + 4 * tm * d * 4
    return pl.pallas_call(
        functools.partial(_outproj_ln_kernel, k_sizes=k_sizes),
        out_shape=jax.ShapeDtypeStruct((n, d), F32),
        grid=(n // tm,),
        in_specs=[pl.BlockSpec((tm, ks), row) for ks in k_sizes] + [
            _resident((k_total, d)),
            pl.BlockSpec((tm, d), row),
            pl.BlockSpec((None, 1, d), lambda i: ((i // per_seq) * 6 + gate_slot, 0, 0)),
            vec, vec],
        out_specs=pl.BlockSpec((tm, d), row),
        compiler_params=_params(("arbitrary",), est),
        name="outproj_ln",
    )(*lhs_list, w_bf, x2, mod3, ln_g.reshape(1, d), ln_b.reshape(1, d))


FFN_PASSES = 2
FFN_PASS_TILE = 1024
FFN_CARRY = 8


def _ffn_pass_kernel(*refs, per_seq, last):
    x_ref, shift_ref, scale_ref, wu_ref, wv_ref, cw_ref, cb_ref, wd_ref = refs[:8]
    if last:
        part_ref, gate_ref, lng_ref, lnb_ref, o_ref, h_scr, u_scr = refs[8:]
    else:
        o_ref, h_scr, u_scr = refs[8:]
    i = pl.program_id(0)
    tm = x_ref.shape[0]
    width = wd_ref.shape[0]
    h_scr[...] = (x_ref[...] * (1.0 + scale_ref[...]) + shift_ref[...]).astype(BF16)

    @pl.when(i == 0)
    def _():
        u_scr[...] = jnp.zeros_like(u_scr)

    inside_sequence = i % per_seq != 0
    u_scr[0:FFN_CARRY, :] = jnp.where(inside_sequence, u_scr[tm:tm + FFN_CARRY, :], 0.0)
    tiles = [(c0, min(FFN_PASS_TILE, width - c0)) for c0 in range(0, width, FFN_PASS_TILE)]

    def up(t):
        c0, w = tiles[t]
        cols = slice(c0, c0 + w)
        u_scr[FFN_CARRY:, cols] = jnp.dot(h_scr[...], wu_ref[:, cols], preferred_element_type=F32)
        return jnp.dot(h_scr[...], wv_ref[:, cols], preferred_element_type=F32)

    def down(t, v):
        c0, w = tiles[t]
        cols = slice(c0, c0 + w)
        u = (cw_ref[2:3, cols] * u_scr[FFN_CARRY:FFN_CARRY + tm, cols]
             + cw_ref[1:2, cols] * u_scr[FFN_CARRY - 1:FFN_CARRY - 1 + tm, cols]
             + cw_ref[0:1, cols] * u_scr[FFN_CARRY - 2:FFN_CARRY - 2 + tm, cols]) + cb_ref[:, cols]
        act = (_silu(u) * v).astype(BF16)
        return jnp.dot(act, wd_ref[cols, :], preferred_element_type=F32)

    acc = None
    v_next = up(0)
    for t in range(len(tiles)):
        v = v_next
        if t + 1 < len(tiles):
            v_next = up(t + 1)
        part = down(t, v)
        acc = part if acc is None else acc + part
    if last:
        y = acc + part_ref[...]
        o_ref[...] = _layer_norm(ALPHA * x_ref[...] + gate_ref[...] * y, lng_ref[...], lnb_ref[...])
    else:
        o_ref[...] = acc


def _ffn_two_pass(x2, mod3, w_up_bf, conv_w, conv_b, w_down_bf, ln_g, ln_b, *, seq):
    assert FFN_PASSES == 2
    n, d = x2.shape
    tm = 256
    per_seq = seq // tm
    width = D_FF // FFN_PASSES
    row = lambda i: (i, 0)
    slot = lambda s: (lambda i: ((i // per_seq) * 6 + s, 0, 0))
    vec = pl.BlockSpec((1, d), lambda i: (0, 0))
    conv_b2 = conv_b.reshape(1, D_FF)
    est = (3 * d * width * 2 + 6 * tm * d * 4 + tm * d * 2 + (tm + FFN_CARRY) * width * 4
           + 3 * tm * d * 4 + 8 * tm * FFN_PASS_TILE * 4)
    partial = None
    for p in range(FFN_PASSES):
        last = p == FFN_PASSES - 1
        in_specs = [pl.BlockSpec((tm, d), row),
                    pl.BlockSpec((None, 1, d), slot(3)),
                    pl.BlockSpec((None, 1, d), slot(4)),
                    pl.BlockSpec((d, width), lambda i, p=p: (0, p), pipeline_mode=pl.Buffered(1)),
                    pl.BlockSpec((d, width), lambda i, p=p: (0, FFN_PASSES + p), pipeline_mode=pl.Buffered(1)),
                    pl.BlockSpec((CONV_WIDTH, width), lambda i, p=p: (0, p)),
                    pl.BlockSpec((1, width), lambda i, p=p: (0, p)),
                    pl.BlockSpec((width, d), lambda i, p=p: (p, 0), pipeline_mode=pl.Buffered(1))]
        args = [x2, mod3, mod3, w_up_bf, w_up_bf, conv_w, conv_b2, w_down_bf]
        if partial is not None:
            in_specs.append(pl.BlockSpec((tm, d), row))
            args.append(partial)
        if last:
            in_specs += [pl.BlockSpec((None, 1, d), slot(5)), vec, vec]
            args += [mod3, ln_g.reshape(1, d), ln_b.reshape(1, d)]
        partial = pl.pallas_call(
            functools.partial(_ffn_pass_kernel, per_seq=per_seq, last=last),
            out_shape=jax.ShapeDtypeStruct((n, d), F32),
            grid=(n // tm,),
            in_specs=in_specs,
            out_specs=pl.BlockSpec((tm, d), row),
            scratch_shapes=[pltpu.VMEM((tm, d), BF16), pltpu.VMEM((tm + FFN_CARRY, width), F32)],
            compiler_params=_params(("arbitrary",), est),
            name="ffn_last" if last else "ffn_part",
        )(*args)
    return partial


def _inproj_odd_kernel(x_ref, shift_ref, scale_ref, w_ref, wgb_ref, bg_ref,
                       q_ref, k_ref, v_ref, g_ref, lf_ref, h_scr):
    h_scr[...] = (x_ref[...] * (1.0 + scale_ref[...]) + shift_ref[...]).astype(BF16)

    def proj(c0, width=PROJ_COLS):
        return jnp.dot(h_scr[...], w_ref[:, c0:c0 + width], preferred_element_type=F32)

    low = proj(2 * C_KEY_DIM + 2 * C_VALUE_DIM, RANK_PAD).astype(BF16)

    def log_decay(c0, width):
        sl = slice(c0, c0 + width)
        gk = jnp.dot(low, wgb_ref[:, sl], preferred_element_type=F32) + bg_ref[:, sl]
        lf_ref[:, sl] = jax.nn.log_sigmoid(gk) * (1.0 / GATE_NORMALIZER)

    piece = PROJ_COLS // 2
    for c0 in range(0, C_KEY_DIM, PROJ_COLS):
        sl = slice(c0, c0 + PROJ_COLS)
        q_ref[:, sl] = (proj(c0) * (C_DK ** -0.5)).astype(BF16)
        log_decay(c0, piece)
        k_ref[:, sl] = proj(C_KEY_DIM + c0).astype(BF16)
        log_decay(c0 + piece, piece)
    for c0 in range(0, C_VALUE_DIM, PROJ_COLS):
        sl = slice(c0, c0 + PROJ_COLS)
        v_ref[:, sl] = proj(2 * C_KEY_DIM + c0).astype(BF16)
        g_ref[:, sl] = proj(2 * C_KEY_DIM + C_VALUE_DIM + c0).astype(BF16)


def _inproj_odd(x2, mod3, w_bf, w_gk_b_bf, b_gk, *, seq):
    n, d = x2.shape
    tm = 512
    per_seq = seq // tm
    width = w_bf.shape[1]
    row = lambda i: (i, 0)
    out_shapes = (
        jax.ShapeDtypeStruct((n, C_KEY_DIM), BF16), jax.ShapeDtypeStruct((n, C_KEY_DIM), BF16),
        jax.ShapeDtypeStruct((n, C_VALUE_DIM), BF16), jax.ShapeDtypeStruct((n, C_VALUE_DIM), BF16),
        jax.ShapeDtypeStruct((n, C_KEY_DIM), F32))
    out_bytes = sum(tm * s.shape[1] * s.dtype.itemsize for s in out_shapes)
    est = d * width * 2 + 2 * tm * d * 4 + 2 * out_bytes + tm * d * 2 + 6 * tm * PROJ_COLS * 4
    return pl.pallas_call(
        _inproj_odd_kernel,
        out_shape=out_shapes,
        grid=(n // tm,),
        in_specs=[pl.BlockSpec((tm, d), row),
                  pl.BlockSpec((None, 1, d), lambda i: ((i // per_seq) * 6 + 0, 0, 0)),
                  pl.BlockSpec((None, 1, d), lambda i: ((i // per_seq) * 6 + 1, 0, 0)),
                  _resident((d, width)),
                  _resident((RANK_PAD, C_KEY_DIM)),
                  pl.BlockSpec((1, C_KEY_DIM), lambda i: (0, 0))],
        out_specs=tuple(pl.BlockSpec((tm, s.shape[1]), row) for s in out_shapes),
        scratch_shapes=[pltpu.VMEM((tm, d), BF16)],
        compiler_params=_params(("arbitrary",), est),
        name="inproj_odd",
    )(x2, mod3, mod3, w_bf, w_gk_b_bf, b_gk.reshape(1, C_KEY_DIM))


def kernel(x, c, positions,
           ada_w0, ada_b0, mix_w_in0, mix_w_out0, attn_sinks0, hgrn_lb_logits, hgrn_norm_w0,
           ln_mix_g0, ln_mix_b0, ffn_w_up0, ffn_conv_w0, ffn_conv_b0, ffn_w_down0, ln_ffn_g0, ln_ffn_b0,
           ada_w1, ada_b1, mix_w_in1, gla_w_gk_a1, gla_w_gk_b1, gla_b_gk1, gla_norm_w1, mix_w_out1,
           ln_mix_g1, ln_mix_b1, ffn_w_up1, ffn_conv_w1, ffn_conv_b1, ffn_w_down1, ln_ffn_g1, ln_ffn_b1):
    batch, seq, d = x.shape
    n = batch * seq
    x2 = x.reshape(n, d)
    c_pad = jnp.pad(c, ((0, 8 - batch), (0, 0)))

    def modulation(ada_w, ada_b):
        mod = _ada_modulation(c_pad, ada_w, ada_b)[:batch]
        return mod.reshape(batch * 6, 1, d)

    mod3 = modulation(ada_w0, ada_b0)
    cos, sin_signed, (w_in0,) = _rope_tables(positions.reshape(n, 1), [mix_w_in0])
    q_a, k_a, v_a, q_b, k_b, lf_b, i_b, g_b = _inproj_even(
        x2, mod3, w_in0, cos, sin_signed, hgrn_lb_logits, seq=seq, layer=0)
    w_gk_a = jnp.pad(gla_w_gk_a1, ((0, 0), (0, RANK_PAD - GATE_RANK)))
    o_a, (w_out0, w_in1, w_out1) = _swa_attention(
        q_a, k_a, v_a, attn_sinks0, [mix_w_out0, (mix_w_in1, w_gk_a), mix_w_out1], batch=batch, seq=seq)
    o_b, (w_up0, w_down0) = _gla(
        q_b, k_b, i_b, lf_b, g_b, hgrn_norm_w0, [ffn_w_up0, ffn_w_down0], batch=batch, seq=seq,
        heads=B_HEADS, dk=B_HEAD_DIM, dv=B_HEAD_DIM, heads_per_step=4, name="hgrn2")
    x2 = _outproj_ln([o_a, o_b], w_out0, x2, mod3, 2, ln_mix_g0, ln_mix_b0, seq=seq)
    x2 = _ffn_two_pass(x2, mod3, w_up0, ffn_conv_w0, ffn_conv_b0, w_down0, ln_ffn_g0, ln_ffn_b0, seq=seq)

    mod3 = modulation(ada_w1, ada_b1)
    w_gk_b = jnp.pad(gla_w_gk_b1, ((0, RANK_PAD - GATE_RANK), (0, 0))).astype(BF16)
    q_c, k_c, v_c, g_c, lf_c = _inproj_odd(x2, mod3, w_in1, w_gk_b, gla_b_gk1, seq=seq)
    o_c, (w_up1, w_down1) = _gla(
        q_c, k_c, v_c, lf_c, g_c, gla_norm_w1, [ffn_w_up1, ffn_w_down1], batch=batch, seq=seq,
        heads=C_HEADS, dk=C_DK, dv=C_DV, heads_per_step=1, name="gla")
    x2 = _outproj_ln([o_c], w_out1, x2, mod3, 2, ln_mix_g1, ln_mix_b1, seq=seq)
    x2 = _ffn_two_pass(x2, mod3, w_up1, ffn_conv_w1, ffn_conv_b1, w_down1, ln_ffn_g1, ln_ffn_b1, seq=seq)
    return x2.reshape(batch, seq, d)
```

```python
import functools
import math

import jax
import jax.numpy as jnp
from jax import lax
from jax.experimental import pallas as pl
from jax.experimental.pallas import tpu as pltpu

F32 = jnp.float32
BF16 = jnp.bfloat16

D_MODEL = 2048
DEPTH = 2
HEAD_DIM = 64
A_Q_HEADS = 16
A_KV_HEADS = 4
A_Q_WIDTH = A_Q_HEADS * HEAD_DIM
A_KV_WIDTH = A_KV_HEADS * HEAD_DIM
WINDOW = 128
ROPE_THETA = 10000.0
B_HEADS = 8
B_HEAD_DIM = 128
B_WIDTH = B_HEADS * B_HEAD_DIM
EVEN_IN_WIDTH = A_Q_WIDTH + 2 * A_KV_WIDTH + 4 * B_WIDTH
C_HEADS = 4
C_KEY_DIM = D_MODEL // 2
C_VALUE_DIM = D_MODEL
C_DK = C_KEY_DIM // C_HEADS
C_DV = C_VALUE_DIM // C_HEADS
GATE_RANK = 16
GATE_NORMALIZER = 16.0
CHUNK = 64
D_FF = 5632
CONV_WIDTH = 3
LN_EPS = 1e-5
RMS_EPS = 1e-6
ALPHA = (2.0 * DEPTH) ** 0.25
NEG_INF = -1e30

V7X_LANES = 128
V7X_VMEM_BYTES = 64 * 1024 * 1024
V7X_VMEM_CEILING = 56 * 1024 * 1024
BF16_SUBLANE_PACK = 16

RANK_PAD = V7X_LANES


def _params(semantics, vmem_estimate):
    limit = min(V7X_VMEM_CEILING, max(16 * 1024 * 1024, int(vmem_estimate * 1.25)))
    return pltpu.CompilerParams(dimension_semantics=semantics, vmem_limit_bytes=limit)


def _resident(shape):
    return pl.BlockSpec(shape, lambda *_: (0,) * len(shape), pipeline_mode=pl.Buffered(1))


def _silu(x):
    return x * jax.nn.sigmoid(x)


class _SideCasts:
    def __init__(self, weights, grid):
        self.groups = [w if isinstance(w, (tuple, list)) else (w,) for w in weights]
        self.sources = [m for group in self.groups for m in group]
        self.group_sizes = tuple(len(group) for group in self.groups)
        steps = math.prod(grid)
        strides = [math.prod(grid[i + 1:]) for i in range(len(grid))]
        self.in_specs, self.out_specs, self.out_shapes = [], [], []
        self.vmem_bytes = 0
        for group in self.groups:
            rows = group[0].shape[0]
            cols = sum(m.shape[1] for m in group)
            share = 1
            while (rows * share) % (steps * BF16_SUBLANE_PACK):
                share *= 2
            block_rows = rows * share // steps

            def index(*ids, share=share):
                return (sum(i * s for i, s in zip(ids, strides)) // share, 0)

            self.in_specs += [pl.BlockSpec((block_rows, m.shape[1]), index) for m in group]
            self.out_specs.append(pl.BlockSpec((block_rows, cols), index))
            self.out_shapes.append(jax.ShapeDtypeStruct((rows, cols), BF16))
            self.vmem_bytes += 2 * block_rows * cols * (4 + 2)


def _copy_side_casts(src_refs, dst_refs, group_sizes):
    src_refs = list(src_refs)
    for dst, size in zip(dst_refs, group_sizes):
        col = 0
        for src in src_refs[:size]:
            dst[:, col:col + src.shape[1]] = src[...].astype(BF16)
            col += src.shape[1]
        src_refs = src_refs[size:]


def _layer_norm(z, g, b):
    mu = jnp.mean(z, axis=-1, keepdims=True)
    zc = z - mu
    var = jnp.mean(zc * zc, axis=-1, keepdims=True)
    return zc * lax.rsqrt(var + LN_EPS) * g + b


def _ada_kernel(c_ref, w_ref, b_ref, o_ref):
    a = _silu(c_ref[...]).astype(BF16)
    o_ref[...] = jnp.dot(a, w_ref[...].astype(BF16), preferred_element_type=F32) + b_ref[...]


def _ada_modulation(c_pad, ada_w, ada_b):
    rows, d = c_pad.shape
    n = ada_w.shape[1]
    tn = 1024
    est = 2 * d * tn * 4 + d * tn * 2 + 4 * rows * n
    return pl.pallas_call(
        _ada_kernel,
        out_shape=jax.ShapeDtypeStruct((rows, n), F32),
        grid=(n // tn,),
        in_specs=[pl.BlockSpec((rows, d), lambda j: (0, 0)),
                  pl.BlockSpec((d, tn), lambda j: (0, j)),
                  pl.BlockSpec((1, tn), lambda j: (0, j))],
        out_specs=pl.BlockSpec((rows, tn), lambda j: (0, j)),
        compiler_params=_params(("arbitrary",), est),
        name="ada_modulation",
    )(c_pad, ada_w, ada_b.reshape(1, n))


def _rope_table_kernel(*refs, cast_groups):
    pos_ref, invf_ref, sign_ref = refs[:3]
    n_src = sum(cast_groups)
    cos_ref, sin_ref = refs[3 + n_src:5 + n_src]
    _copy_side_casts(refs[3:3 + n_src], refs[5 + n_src:], cast_groups)
    ang = pos_ref[...].astype(F32) * invf_ref[...]
    cos_ref[...] = jnp.cos(ang)
    sin_ref[...] = jnp.sin(ang) * sign_ref[...]


def _rope_tables(pos_col, cast_weights):
    n = pos_col.shape[0]
    half = HEAD_DIM // 2
    lane = jnp.arange(V7X_LANES)
    inv_freq = ROPE_THETA ** (-jnp.arange(half, dtype=F32) / half)
    invf = inv_freq[lane % half].reshape(1, V7X_LANES)
    sign = jnp.where((lane % HEAD_DIM) < half, -1.0, 1.0).astype(F32).reshape(1, V7X_LANES)
    tm = 2048
    grid = (n // tm,)
    casts = _SideCasts(cast_weights, grid)
    est = 2 * (tm * V7X_LANES * 4) * 3 + casts.vmem_bytes
    outs = pl.pallas_call(
        functools.partial(_rope_table_kernel, cast_groups=casts.group_sizes),
        out_shape=[jax.ShapeDtypeStruct((n, V7X_LANES), F32)] * 2 + casts.out_shapes,
        grid=grid,
        in_specs=[pl.BlockSpec((tm, 1), lambda i: (i, 0)),
                  pl.BlockSpec((1, V7X_LANES), lambda i: (0, 0)),
                  pl.BlockSpec((1, V7X_LANES), lambda i: (0, 0))] + casts.in_specs,
        out_specs=[pl.BlockSpec((tm, V7X_LANES), lambda i: (i, 0))] * 2 + casts.out_specs,
        compiler_params=_params(("arbitrary",), est),
        name="rope_tables",
    )(pos_col, invf, sign, *casts.sources)
    return outs[0], outs[1], outs[2:]


def _rope(a, cos, sin_signed, first_half):
    outs = []
    for s in range(a.shape[1] // V7X_LANES):
        blk = a[:, s * V7X_LANES:(s + 1) * V7X_LANES]
        partner = jnp.where(first_half,
                            pltpu.roll(blk, V7X_LANES - HEAD_DIM // 2, 1),
                            pltpu.roll(blk, HEAD_DIM // 2, 1))
        outs.append(blk * cos + partner * sin_signed)
    return jnp.concatenate(outs, axis=1) if len(outs) > 1 else outs[0]


PROJ_COLS = 1024


def _inproj_even_kernel(x_ref, shift_ref, scale_ref, w_ref, cos_ref, sin_ref, lbl_ref,
                        qa_ref, ka_ref, va_ref, qb_ref, kb_ref, lf_ref, ib_ref, gb_ref, h_scr, *, layer):
    tm = x_ref.shape[0]
    h_scr[...] = (x_ref[...] * (1.0 + scale_ref[...]) + shift_ref[...]).astype(BF16)

    def proj(c0, width=PROJ_COLS):
        return jnp.dot(h_scr[...], w_ref[:, c0:c0 + width], preferred_element_type=F32)

    cos = cos_ref[...]
    sin_signed = sin_ref[...]
    lane = lax.broadcasted_iota(jnp.int32, (tm, V7X_LANES), 1)
    first_half = (lane % HEAD_DIM) < (HEAD_DIM // 2)

    for c0 in range(0, A_Q_WIDTH, PROJ_COLS):
        qa_ref[:, c0:c0 + PROJ_COLS] = (_rope(proj(c0), cos, sin_signed, first_half)
                                        * (HEAD_DIM ** -0.5)).astype(BF16)
    kv = proj(A_Q_WIDTH, 2 * A_KV_WIDTH)
    ka_ref[...] = _rope(kv[:, :A_KV_WIDTH], cos, sin_signed, first_half).astype(BF16)
    va_ref[...] = kv[:, A_KV_WIDTH:].astype(BF16)

    base = A_Q_WIDTH + 2 * A_KV_WIDTH
    lg = lbl_ref[...]
    e = jnp.exp(lg - jnp.max(lg, axis=0, keepdims=True))
    sm = e / jnp.sum(e, axis=0, keepdims=True)
    lb = jnp.sum(sm[0:layer + 1], axis=0, keepdims=True)
    for c0 in range(0, B_WIDTH, PROJ_COLS):
        sl = slice(c0, c0 + PROJ_COLS)
        qb_ref[:, sl] = proj(base + c0).astype(BF16)
        lbc = lb[:, sl]
        fg = lbc + (1.0 - lbc) * jax.nn.sigmoid(proj(base + B_WIDTH + c0))
        kb_ref[:, sl] = (1.0 - fg).astype(BF16)
        lf_ref[:, sl] = jnp.log(fg)
        ib_ref[:, sl] = proj(base + 2 * B_WIDTH + c0).astype(BF16)
        gb_ref[:, sl] = proj(base + 3 * B_WIDTH + c0).astype(BF16)


def _inproj_even(x2, mod3, w_bf, cos, sin_signed, lb_logits, *, seq, layer):
    n, d = x2.shape
    tm = 512
    per_seq = seq // tm
    width = w_bf.shape[1]
    row = lambda i: (i, 0)
    out_shapes = (
        jax.ShapeDtypeStruct((n, A_Q_WIDTH), BF16), jax.ShapeDtypeStruct((n, A_KV_WIDTH), BF16),
        jax.ShapeDtypeStruct((n, A_KV_WIDTH), BF16), jax.ShapeDtypeStruct((n, B_WIDTH), BF16),
        jax.ShapeDtypeStruct((n, B_WIDTH), BF16), jax.ShapeDtypeStruct((n, B_WIDTH), F32),
        jax.ShapeDtypeStruct((n, B_WIDTH), BF16), jax.ShapeDtypeStruct((n, B_WIDTH), BF16))
    out_bytes = sum(tm * s.shape[1] * s.dtype.itemsize for s in out_shapes)
    est = d * width * 2 + 2 * tm * d * 4 + 2 * out_bytes + tm * d * 2 + 6 * tm * PROJ_COLS * 4
    return pl.pallas_call(
        functools.partial(_inproj_even_kernel, layer=layer),
        out_shape=out_shapes,
        grid=(n // tm,),
        in_specs=[pl.BlockSpec((tm, d), row),
                  pl.BlockSpec((None, 1, d), lambda i: ((i // per_seq) * 6 + 0, 0, 0)),
                  pl.BlockSpec((None, 1, d), lambda i: ((i // per_seq) * 6 + 1, 0, 0)),
                  _resident((d, width)),
                  pl.BlockSpec((tm, V7X_LANES), row),
                  pl.BlockSpec((tm, V7X_LANES), row),
                  pl.BlockSpec(lb_logits.shape, lambda i: (0, 0))],
        out_specs=tuple(pl.BlockSpec((tm, s.shape[1]), row) for s in out_shapes),
        scratch_shapes=[pltpu.VMEM((tm, d), BF16)],
        compiler_params=_params(("arbitrary",), est),
        name="inproj_even",
    )(x2, mod3, mod3, w_bf, cos, sin_signed, lb_logits)


def _swa_kernel(*refs, cast_groups):
    sink_ref, q_ref, kp_ref, kc_ref, vp_ref, vc_ref = refs[:6]
    n_src = sum(cast_groups)
    o_ref = refs[6 + n_src]
    _copy_side_casts(refs[6:6 + n_src], refs[7 + n_src:], cast_groups)
    first_step = pl.program_id(1) == 0
    grp = A_Q_HEADS // A_KV_HEADS
    assert HEAD_DIM * 2 == V7X_LANES and grp == 4
    r = lax.broadcasted_iota(jnp.int32, (WINDOW, 2 * WINDOW), 0)
    c = lax.broadcasted_iota(jnp.int32, (WINDOW, 2 * WINDOW), 1)
    rel = r + WINDOW - c
    band = (rel >= 0) & (rel < WINDOW)
    band_first = band & ((c >= WINDOW) | jnp.logical_not(first_step))
    lane = lax.broadcasted_iota(jnp.int32, (1, V7X_LANES), 1)
    half_mask = [jnp.where(lane < HEAD_DIM, 1.0, 0.0).astype(BF16),
                 jnp.where(lane >= HEAD_DIM, 1.0, 0.0).astype(BF16)]
    low_half = lax.broadcasted_iota(jnp.int32, (WINDOW, V7X_LANES), 1) < HEAD_DIM

    def frame(prev_ref, cur_ref, sub, kv_tile):
        own = cur_ref[sub * WINDOW:(sub + 1) * WINDOW, kv_tile]
        before = prev_ref[:, kv_tile] if sub == 0 else cur_ref[(sub - 1) * WINDOW:sub * WINDOW, kv_tile]
        return jnp.concatenate([before, own], axis=0)

    def scores(sub, g):
        kv_tile = slice((g // 2) * V7X_LANES, (g // 2 + 1) * V7X_LANES)
        kv_half = g % 2
        q_rows = []
        for e in range(grp):
            q_tile = (2 * g + e // 2) * V7X_LANES
            qt = q_ref[sub * WINDOW:(sub + 1) * WINDOW, q_tile:q_tile + V7X_LANES]
            if e % 2 != kv_half:
                qt = pltpu.roll(qt, HEAD_DIM, 1)
            q_rows.append(qt * half_mask[kv_half])
        qs = jnp.concatenate(q_rows, axis=0)
        return lax.dot_general(qs, frame(kp_ref, kc_ref, sub, kv_tile), (((1,), (1,)), ((), ())),
                               preferred_element_type=F32)

    units = [(sub, g) for sub in range(q_ref.shape[0] // WINDOW) for g in range(A_KV_HEADS)]
    s_next = scores(*units[0])
    for idx, (sub, g) in enumerate(units):
        kv_tile = slice((g // 2) * V7X_LANES, (g // 2 + 1) * V7X_LANES)
        kv_half = g % 2
        qrows = slice(sub * WINDOW, (sub + 1) * WINDOW)
        valid = band_first if sub == 0 else band
        s_all = s_next
        if idx + 1 < len(units):
            s_next = scores(*units[idx + 1])
        p_rows = []
        denoms = []
        for e in range(grp):
            s = jnp.where(valid, s_all[e * WINDOW:(e + 1) * WINDOW], NEG_INF)
            sink = sink_ref[grp * g + e]
            m = jnp.maximum(jnp.max(s, axis=-1, keepdims=True), sink)
            p = jnp.exp(s - m)
            denoms.append(jnp.sum(p, axis=-1, keepdims=True) + jnp.exp(sink - m))
            p_rows.append(p.astype(BF16))
        o_all = jnp.dot(jnp.concatenate(p_rows, axis=0), frame(vp_ref, vc_ref, sub, kv_tile),
                        preferred_element_type=F32)
        for u in range(grp // 2):
            o_even = o_all[(2 * u) * WINDOW:(2 * u + 1) * WINDOW] / denoms[2 * u]
            o_odd = o_all[(2 * u + 1) * WINDOW:(2 * u + 2) * WINDOW] / denoms[2 * u + 1]
            if kv_half == 0:
                tile = jnp.where(low_half, o_even, pltpu.roll(o_odd, HEAD_DIM, 1))
            else:
                tile = jnp.where(low_half, pltpu.roll(o_even, HEAD_DIM, 1), o_odd)
            out_tile = (2 * g + u) * V7X_LANES
            o_ref[qrows, out_tile:out_tile + V7X_LANES] = tile.astype(BF16)


SWA_BLOCKS_PER_STEP = 1


def _swa_attention(q_a, k_a, v_a, sinks, cast_weights, *, batch, seq):
    n = q_a.shape[0]
    tq = SWA_BLOCKS_PER_STEP * WINDOW
    nb = seq // tq
    grid = (batch, nb)
    casts = _SideCasts(cast_weights, grid)
    cur = lambda b, i: (b * nb + i, 0)
    prev = lambda b, i: (jnp.maximum((b * nb + i) * SWA_BLOCKS_PER_STEP - 1, 0), 0)
    est = (2 * (2 * tq * A_Q_WIDTH * 2 + 2 * (tq + WINDOW) * A_KV_WIDTH * 2) + 24 * WINDOW * 2 * WINDOW * 4
           + casts.vmem_bytes)
    outs = pl.pallas_call(
        functools.partial(_swa_kernel, cast_groups=casts.group_sizes),
        out_shape=[jax.ShapeDtypeStruct((n, A_Q_WIDTH), BF16)] + casts.out_shapes,
        grid=grid,
        in_specs=[pl.BlockSpec(memory_space=pltpu.SMEM),
                  pl.BlockSpec((tq, A_Q_WIDTH), cur),
                  pl.BlockSpec((WINDOW, A_KV_WIDTH), prev),
                  pl.BlockSpec((tq, A_KV_WIDTH), cur),
                  pl.BlockSpec((WINDOW, A_KV_WIDTH), prev),
                  pl.BlockSpec((tq, A_KV_WIDTH), cur)] + casts.in_specs,
        out_specs=[pl.BlockSpec((tq, A_Q_WIDTH), cur)] + casts.out_specs,
        compiler_params=_params(("arbitrary", "arbitrary"), est),
        name="swa_attention",
    )(sinks, q_a, k_a, k_a, v_a, v_a, *casts.sources)
    return outs[0], outs[1:]


def _split3(x):
    hi = x.astype(BF16)
    r1 = x - hi.astype(F32)
    mid = r1.astype(BF16)
    lo = (r1 - mid.astype(F32)).astype(BF16)
    return hi, mid, lo


GLA_GROUP = 256


def _gla_kernel(*refs, dk, dv, heads_per_step, cast_groups):
    q_ref, k_ref, v_ref, lf_ref, g_ref, nw_ref = refs[:6]
    n_src = sum(cast_groups)
    o_ref = refs[6 + n_src]
    s_scr = refs[-1]
    _copy_side_casts(refs[6:6 + n_src], refs[7 + n_src:-1], cast_groups)

    @pl.when(pl.program_id(2) == 0)
    def _():
        s_scr[...] = jnp.zeros_like(s_scr)

    tb = q_ref.shape[0]
    assert GLA_GROUP == 4 * CHUNK
    row = lax.broadcasted_iota(jnp.int32, (GLA_GROUP, GLA_GROUP), 0)
    col = lax.broadcasted_iota(jnp.int32, (GLA_GROUP, GLA_GROUP), 1)
    rc = row // CHUNK
    cc = col // CHUNK
    same_chunk = (rc == cc) & (row >= col)
    next_chunk = (rc == cc + 1) & (rc % 2 == 1)
    far_chunk = (rc >= 2) & (cc <= 1)
    tril = jnp.where(same_chunk, 1.0, 0.0).astype(BF16)
    nw = nw_ref[...]
    one = jnp.ones((1, dk), F32)
    nt = (((1,), (1,)), ((), ()))

    def by_chunk(vecs):
        return jnp.concatenate([jnp.broadcast_to(v, (CHUNK, dk)) for v in vecs], axis=0)

    def prepare(hh, gi):
        rows = slice(gi * GLA_GROUP, (gi + 1) * GLA_GROUP)
        kcols = slice(hh * dk, (hh + 1) * dk)
        vcols = slice(hh * dv, (hh + 1) * dv)
        hi, mid, lo = _split3(lf_ref[rows, kcols])
        bb = jnp.dot(tril, jnp.concatenate([hi, mid, lo], axis=1), preferred_element_type=F32)
        b = bb[:, :dk] + bb[:, dk:2 * dk] + bb[:, 2 * dk:]
        l0, l1, l2, l3 = [b[(c + 1) * CHUNK - 1:(c + 1) * CHUNK, :] for c in range(4)]
        b_mid = by_chunk([b[c * CHUNK + CHUNK // 2 - 1:c * CHUNK + CHUNK // 2, :] for c in range(4)])
        q = q_ref[rows, kcols].astype(F32)
        k = k_ref[rows, kcols].astype(F32)
        v = v_ref[rows, vcols]
        q_loc = q * jnp.exp(b)
        k_loc = k * jnp.exp(by_chunk([l0, l1, l2, l3]) - b)
        q_i = (q * jnp.exp(b - b_mid)).astype(BF16)
        k_i = (k * jnp.exp(b_mid - b)).astype(BF16)
        q_far = (q_loc * by_chunk([one, one, one, jnp.exp(l2)])).astype(BF16)
        k_far = (k_loc * by_chunk([jnp.exp(l1), one, one, one])).astype(BF16)
        q_grp = (q_loc * by_chunk([one, jnp.exp(l0), jnp.exp(l0 + l1), jnp.exp(l0 + l1 + l2)])).astype(BF16)
        k_grp = (k_loc * by_chunk([jnp.exp(l1 + l2 + l3), jnp.exp(l2 + l3), jnp.exp(l3), one])).astype(BF16)
        a_same = lax.dot_general(q_i, k_i, nt, preferred_element_type=F32)
        a_next = lax.dot_general(q_loc.astype(BF16), k_loc.astype(BF16), nt, preferred_element_type=F32)
        a_far = lax.dot_general(q_far, k_far, nt, preferred_element_type=F32)
        a = jnp.where(same_chunk, a_same, jnp.where(next_chunk, a_next, jnp.where(far_chunk, a_far, 0.0)))
        o_intra = jnp.dot(a.astype(BF16), v, preferred_element_type=F32)
        kv = lax.dot_general(k_grp, v, (((0,), (0,)), ((), ())), preferred_element_type=F32)
        decay = jnp.exp(jnp.transpose(jnp.broadcast_to(l0 + l1 + l2 + l3, (V7X_LANES, dk))))
        return rows, vcols, o_intra, q_grp, kv, decay

    def finish(state, prepared):
        rows, vcols, o_intra, q_grp, kv, decay = prepared
        o = o_intra + jnp.dot(q_grp, state.astype(BF16), preferred_element_type=F32)
        o = o * lax.rsqrt(jnp.mean(o * o, axis=-1, keepdims=True) + RMS_EPS)
        o_ref[rows, vcols] = ((o * nw) * _silu(g_ref[rows, vcols].astype(F32))).astype(BF16)
        return state * jnp.tile(decay, (1, dv // V7X_LANES)) + kv

    units = [(hh, gi) for gi in range(tb // GLA_GROUP) for hh in range(heads_per_step)]
    states = [s_scr[hh] for hh in range(heads_per_step)]
    pending = prepare(*units[0])
    for idx, (hh, gi) in enumerate(units):
        upcoming = prepare(*units[idx + 1]) if idx + 1 < len(units) else None
        states[hh] = finish(states[hh], pending)
        pending = upcoming
    for hh in range(heads_per_step):
        s_scr[hh] = states[hh]


def _gla(q, k, v, log_f, g, norm_w, cast_weights, *, batch, seq, heads, dk, dv, heads_per_step, name):
    n = q.shape[0]
    tb = 1024
    nt = seq // tb
    hp = heads_per_step
    grid = (batch, heads // hp, nt)
    casts = _SideCasts(cast_weights, grid)
    idx = lambda b, h, t: (b * nt + t, h)
    est = (2 * tb * hp * (3 * dk * 4 + dv * 2 + dv * 4 + dv * 2) + hp * dk * dv * 4 + 4 * dk * dv * 4
           + 16 * GLA_GROUP * max(3 * dk, dv) * 4 + casts.vmem_bytes)
    outs = pl.pallas_call(
        functools.partial(_gla_kernel, dk=dk, dv=dv, heads_per_step=hp, cast_groups=casts.group_sizes),
        out_shape=[jax.ShapeDtypeStruct((n, heads * dv), BF16)] + casts.out_shapes,
        grid=grid,
        in_specs=[pl.BlockSpec((tb, hp * dk), idx), pl.BlockSpec((tb, hp * dk), idx),
                  pl.BlockSpec((tb, hp * dv), idx), pl.BlockSpec((tb, hp * dk), idx),
                  pl.BlockSpec((tb, hp * dv), idx),
                  pl.BlockSpec((1, dv), lambda b, h, t: (0, 0))] + casts.in_specs,
        out_specs=[pl.BlockSpec((tb, hp * dv), idx)] + casts.out_specs,
        scratch_shapes=[pltpu.VMEM((hp, dk, dv), F32)],
        compiler_params=_params(("arbitrary", "arbitrary", "arbitrary"), est),
        name=name,
    )(q, k, v, log_f, g, norm_w.reshape(1, dv), *casts.sources)
    return outs[0], outs[1:]


OUTPROJ_SUB_ROWS = 256


def _outproj_ln_kernel(*refs, k_sizes):
    lhs_refs = refs[:len(k_sizes)]
    w_ref, x_ref, gate_ref, g_ref, b_ref, o_ref = refs[len(k_sizes):]

    def project(rows):
        y = None
        off = 0
        for r, ks in zip(lhs_refs, k_sizes):
            part = jnp.dot(r[rows, :], w_ref[off:off + ks, :], preferred_element_type=F32)
            y = part if y is None else y + part
            off += ks
        return y

    subs = [slice(r0, r0 + OUTPROJ_SUB_ROWS) for r0 in range(0, o_ref.shape[0], OUTPROJ_SUB_ROWS)]
    y_next = project(subs[0])
    for idx, rows in enumerate(subs):
        y = y_next
        if idx + 1 < len(subs):
            y_next = project(subs[idx + 1])
        o_ref[rows, :] = _layer_norm(ALPHA * x_ref[rows, :] + gate_ref[...] * y, g_ref[...], b_ref[...])


def _outproj_ln(lhs_list, w_bf, x2, mod3, gate_slot, ln_g, ln_b, *, seq):
    n, d = x2.shape
    tm = 512
    per_seq = seq // tm
    row = lambda i: (i, 0)
    k_sizes = tuple(a.shape[1] for a in lhs_list)
    k_total = sum(k_sizes)
    vec = pl.BlockSpec((1, d), lambda i: (0, 0))
    est = k_total * d * 2 + 2 * tm * k_total * 2 + 4 * tm * d * 4 + 4 * tm * d * 4
    return pl.pallas_call(
        functools.partial(_outproj_ln_kernel, k_sizes=k_sizes),
        out_shape=jax.ShapeDtypeStruct((n, d), F32),
        grid=(n // tm,),
        in_specs=[pl.BlockSpec((tm, ks), row) for ks in k_sizes] + [
            _resident((k_total, d)),
            pl.BlockSpec((tm, d), row),
            pl.BlockSpec((None, 1, d), lambda i: ((i // per_seq) * 6 + gate_slot, 0, 0)),
            vec, vec],
        out_specs=pl.BlockSpec((tm, d), row),
        compiler_params=_params(("arbitrary",), est),
        name="outproj_ln",
    )(*lhs_list, w_bf, x2, mod3, ln_g.reshape(1, d), ln_b.reshape(1, d))


FFN_PASSES = 2
FFN_PASS_TILE = 1024
FFN_CARRY = 8


def _ffn_pass_kernel(*refs, per_seq, last):
    x_ref, shift_ref, scale_ref, wu_ref, wv_ref, cw_ref, cb_ref, wd_ref = refs[:8]
    if last:
        part_ref, gate_ref, lng_ref, lnb_ref, o_ref, h_scr, u_scr = refs[8:]
    else:
        o_ref, h_scr, u_scr = refs[8:]
    i = pl.program_id(0)
    tm = x_ref.shape[0]
    width = wd_ref.shape[0]
    h_scr[...] = (x_ref[...] * (1.0 + scale_ref[...]) + shift_ref[...]).astype(BF16)

    @pl.when(i == 0)
    def _():
        u_scr[...] = jnp.zeros_like(u_scr)

    inside_sequence = i % per_seq != 0
    u_scr[0:FFN_CARRY, :] = jnp.where(inside_sequence, u_scr[tm:tm + FFN_CARRY, :], 0.0)
    tiles = [(c0, min(FFN_PASS_TILE, width - c0)) for c0 in range(0, width, FFN_PASS_TILE)]

    def up(t):
        c0, w = tiles[t]
        cols = slice(c0, c0 + w)
        u_scr[FFN_CARRY:, cols] = jnp.dot(h_scr[...], wu_ref[:, cols], preferred_element_type=F32)
        return jnp.dot(h_scr[...], wv_ref[:, cols], preferred_element_type=F32)

    def down(t, v):
        c0, w = tiles[t]
        cols = slice(c0, c0 + w)
        u = (cw_ref[2:3, cols] * u_scr[FFN_CARRY:FFN_CARRY + tm, cols]
             + cw_ref[1:2, cols] * u_scr[FFN_CARRY - 1:FFN_CARRY - 1 + tm, cols]
             + cw_ref[0:1, cols] * u_scr[FFN_CARRY - 2:FFN_CARRY - 2 + tm, cols]) + cb_ref[:, cols]
        act = (_silu(u) * v).astype(BF16)
        return jnp.dot(act, wd_ref[cols, :], preferred_element_type=F32)

    acc = None
    v_next = up(0)
    for t in range(len(tiles)):
        v = v_next
        if t + 1 < len(tiles):
            v_next = up(t + 1)
        part = down(t, v)
        acc = part if acc is None else acc + part
    if last:
        y = acc + part_ref[...]
        o_ref[...] = _layer_norm(ALPHA * x_ref[...] + gate_ref[...] * y, lng_ref[...], lnb_ref[...])
    else:
        o_ref[...] = acc


def _ffn_two_pass(x2, mod3, w_up_bf, conv_w, conv_b, w_down_bf, ln_g, ln_b, *, seq):
    assert FFN_PASSES == 2
    n, d = x2.shape
    tm = 256
    per_seq = seq // tm
    width = D_FF // FFN_PASSES
    row = lambda i: (i, 0)
    slot = lambda s: (lambda i: ((i // per_seq) * 6 + s, 0, 0))
    vec = pl.BlockSpec((1, d), lambda i: (0, 0))
    conv_b2 = conv_b.reshape(1, D_FF)
    est = (3 * d * width * 2 + 6 * tm * d * 4 + tm * d * 2 + (tm + FFN_CARRY) * width * 4
           + 3 * tm * d * 4 + 8 * tm * FFN_PASS_TILE * 4)
    partial = None
    for p in range(FFN_PASSES):
        last = p == FFN_PASSES - 1
        in_specs = [pl.BlockSpec((tm, d), row),
                    pl.BlockSpec((None, 1, d), slot(3)),
                    pl.BlockSpec((None, 1, d), slot(4)),
                    pl.BlockSpec((d, width), lambda i, p=p: (0, p), pipeline_mode=pl.Buffered(1)),
                    pl.BlockSpec((d, width), lambda i, p=p: (0, FFN_PASSES + p), pipeline_mode=pl.Buffered(1)),
                    pl.BlockSpec((CONV_WIDTH, width), lambda i, p=p: (0, p)),
                    pl.BlockSpec((1, width), lambda i, p=p: (0, p)),
                    pl.BlockSpec((width, d), lambda i, p=p: (p, 0), pipeline_mode=pl.Buffered(1))]
        args = [x2, mod3, mod3, w_up_bf, w_up_bf, conv_w, conv_b2, w_down_bf]
        if partial is not None:
            in_specs.append(pl.BlockSpec((tm, d), row))
            args.append(partial)
        if last:
            in_specs += [pl.BlockSpec((None, 1, d), slot(5)), vec, vec]
            args += [mod3, ln_g.reshape(1, d), ln_b.reshape(1, d)]
        partial = pl.pallas_call(
            functools.partial(_ffn_pass_kernel, per_seq=per_seq, last=last),
            out_shape=jax.ShapeDtypeStruct((n, d), F32),
            grid=(n // tm,),
            in_specs=in_specs,
            out_specs=pl.BlockSpec((tm, d), row),
            scratch_shapes=[pltpu.VMEM((tm, d), BF16), pltpu.VMEM((tm + FFN_CARRY, width), F32)],
            compiler_params=_params(("arbitrary",), est),
            name="ffn_last" if last else "ffn_part",
        )(*args)
    return partial


def _inproj_odd_kernel(x_ref, shift_ref, scale_ref, w_ref, wgb_ref, bg_ref,
                       q_ref, k_ref, v_ref, g_ref, lf_ref, h_scr):
    h_scr[...] = (x_ref[...] * (1.0 + scale_ref[...]) + shift_ref[...]).astype(BF16)

    def proj(c0, width=PROJ_COLS):
        return jnp.dot(h_scr[...], w_ref[:, c0:c0 + width], preferred_element_type=F32)

    low = proj(2 * C_KEY_DIM + 2 * C_VALUE_DIM, RANK_PAD).astype(BF16)

    def log_decay(c0, width):
        sl = slice(c0, c0 + width)
        gk = jnp.dot(low, wgb_ref[:, sl], preferred_element_type=F32) + bg_ref[:, sl]
        lf_ref[:, sl] = jax.nn.log_sigmoid(gk) * (1.0 / GATE_NORMALIZER)

    piece = PROJ_COLS // 2
    for c0 in range(0, C_KEY_DIM, PROJ_COLS):
        sl = slice(c0, c0 + PROJ_COLS)
        q_ref[:, sl] = (proj(c0) * (C_DK ** -0.5)).astype(BF16)
        log_decay(c0, piece)
        k_ref[:, sl] = proj(C_KEY_DIM + c0).astype(BF16)
        log_decay(c0 + piece, piece)
    for c0 in range(0, C_VALUE_DIM, PROJ_COLS):
        sl = slice(c0, c0 + PROJ_COLS)
        v_ref[:, sl] = proj(2 * C_KEY_DIM + c0).astype(BF16)
        g_ref[:, sl] = proj(2 * C_KEY_DIM + C_VALUE_DIM + c0).astype(BF16)


def _inproj_odd(x2, mod3, w_bf, w_gk_b_bf, b_gk, *, seq):
    n, d = x2.shape
    tm = 512
    per_seq = seq // tm
    width = w_bf.shape[1]
    row = lambda i: (i, 0)
    out_shapes = (
        jax.ShapeDtypeStruct((n, C_KEY_DIM), BF16), jax.ShapeDtypeStruct((n, C_KEY_DIM), BF16),
        jax.ShapeDtypeStruct((n, C_VALUE_DIM), BF16), jax.ShapeDtypeStruct((n, C_VALUE_DIM), BF16),
        jax.ShapeDtypeStruct((n, C_KEY_DIM), F32))
    out_bytes = sum(tm * s.shape[1] * s.dtype.itemsize for s in out_shapes)
    est = d * width * 2 + 2 * tm * d * 4 + 2 * out_bytes + tm * d * 2 + 6 * tm * PROJ_COLS * 4
    return pl.pallas_call(
        _inproj_odd_kernel,
        out_shape=out_shapes,
        grid=(n // tm,),
        in_specs=[pl.BlockSpec((tm, d), row),
                  pl.BlockSpec((None, 1, d), lambda i: ((i // per_seq) * 6 + 0, 0, 0)),
                  pl.BlockSpec((None, 1, d), lambda i: ((i // per_seq) * 6 + 1, 0, 0)),
                  _resident((d, width)),
                  _resident((RANK_PAD, C_KEY_DIM)),
                  pl.BlockSpec((1, C_KEY_DIM), lambda i: (0, 0))],
        out_specs=tuple(pl.BlockSpec((tm, s.shape[1]), row) for s in out_shapes),
        scratch_shapes=[pltpu.VMEM((tm, d), BF16)],
        compiler_params=_params(("arbitrary",), est),
        name="inproj_odd",
    )(x2, mod3, mod3, w_bf, w_gk_b_bf, b_gk.reshape(1, C_KEY_DIM))


def kernel(x, c, positions,
           ada_w0, ada_b0, mix_w_in0, mix_w_out0, attn_sinks0, hgrn_lb_logits, hgrn_norm_w0,
           ln_mix_g0, ln_mix_b0, ffn_w_up0, ffn_conv_w0, ffn_conv_b0, ffn_w_down0, ln_ffn_g0, ln_ffn_b0,
           ada_w1, ada_b1, mix_w_in1, gla_w_gk_a1, gla_w_gk_b1, gla_b_gk1, gla_norm_w1, mix_w_out1,
           ln_mix_g1, ln_mix_b1, ffn_w_up1, ffn_conv_w1, ffn_conv_b1, ffn_w_down1, ln_ffn_g1, ln_ffn_b1):
    batch, seq, d = x.shape
    n = batch * seq
    x2 = x.reshape(n, d)
    c_pad = jnp.pad(c, ((0, 8 - batch), (0, 0)))

    def modulation(ada_w, ada_b):
        mod = _ada_modulation(c_pad, ada_w, ada_b)[:batch]
        return mod.reshape(batch * 6, 1, d)

    mod3 = modulation(ada_w0, ada_b0)
    cos, sin_signed, (w_in0,) = _rope_tables(positions.reshape(n, 1), [mix_w_in0])
    q_a, k_a, v_a, q_b, k_b, lf_b, i_b, g_b = _inproj_even(
        x2, mod3, w_in0, cos, sin_signed, hgrn_lb_logits, seq=seq, layer=0)
    w_gk_a = jnp.pad(gla_w_gk_a1, ((0, 0), (0, RANK_PAD - GATE_RANK)))
    o_a, (w_out0, w_in1, w_out1) = _swa_attention(
        q_a, k_a, v_a, attn_sinks0, [mix_w_out0, (mix_w_in1, w_gk_a), mix_w_out1], batch=batch, seq=seq)
    o_b, (w_up0, w_down0) = _gla(
        q_b, k_b, i_b, lf_b, g_b, hgrn_norm_w0, [ffn_w_up0, ffn_w_down0], batch=batch, seq=seq,
        heads=B_HEADS, dk=B_HEAD_DIM, dv=B_HEAD_DIM, heads_per_step=4, name="hgrn2")
    x2 = _outproj_ln([o_a, o_b], w_out0, x2, mod3, 2, ln_mix_g0, ln_mix_b0, seq=seq)
    x2 = _ffn_two_pass(x2, mod3, w_up0, ffn_conv_w0, ffn_conv_b0, w_down0, ln_ffn_g0, ln_ffn_b0, seq=seq)

    mod3 = modulation(ada_w1, ada_b1)
    w_gk_b = jnp.pad(gla_w_gk_b1, ((0, RANK_PAD - GATE_RANK), (0, 0))).astype(BF16)
    q_c, k_c, v_c, g_c, lf_c = _inproj_odd(x2, mod3, w_in1, w_gk_b, gla_b_gk1, seq=seq)
    o_c, (w_up1, w_down1) = _gla(
        q_c, k_c, v_c, lf_c, g_c, gla_norm_w1, [ffn_w_up1, ffn_w_down1], batch=batch, seq=seq,
        heads=C_HEADS, dk=C_DK, dv=C_DV, heads_per_step=2, name="gla")
    x2 = _outproj_ln([o_c], w_out1, x2, mod3, 2, ln_mix_g1, ln_mix_b1, seq=seq)
    x2 = _ffn_two_pass(x2, mod3, w_up1, ffn_conv_w1, ffn_conv_b1, w_down1, ln_ffn_g1, ln_ffn_b1, seq=seq)
    return x2.reshape(batch, seq, d)
```

```python
import functools
import math

import jax
import jax.numpy as jnp
from jax import lax
from jax.experimental import pallas as pl
from jax.experimental.pallas import tpu as pltpu

F32 = jnp.float32
BF16 = jnp.bfloat16

D_MODEL = 2048
DEPTH = 2
HEAD_DIM = 64
A_Q_HEADS = 16
A_KV_HEADS = 4
A_Q_WIDTH = A_Q_HEADS * HEAD_DIM
A_KV_WIDTH = A_KV_HEADS * HEAD_DIM
WINDOW = 128
ROPE_THETA = 10000.0
B_HEADS = 8
B_HEAD_DIM = 128
B_WIDTH = B_HEADS * B_HEAD_DIM
EVEN_IN_WIDTH = A_Q_WIDTH + 2 * A_KV_WIDTH + 4 * B_WIDTH
C_HEADS = 4
C_KEY_DIM = D_MODEL // 2
C_VALUE_DIM = D_MODEL
C_DK = C_KEY_DIM // C_HEADS
C_DV = C_VALUE_DIM // C_HEADS
GATE_RANK = 16
GATE_NORMALIZER = 16.0
CHUNK = 64
D_FF = 5632
CONV_WIDTH = 3
LN_EPS = 1e-5
RMS_EPS = 1e-6
ALPHA = (2.0 * DEPTH) ** 0.25
NEG_INF = -1e30

V7X_LANES = 128
V7X_VMEM_BYTES = 64 * 1024 * 1024
V7X_VMEM_CEILING = 56 * 1024 * 1024
BF16_SUBLANE_PACK = 16

RANK_PAD = V7X_LANES


def _params(semantics, vmem_estimate):
    limit = min(V7X_VMEM_CEILING, max(16 * 1024 * 1024, int(vmem_estimate * 1.25)))
    return pltpu.CompilerParams(dimension_semantics=semantics, vmem_limit_bytes=limit)


def _resident(shape):
    return pl.BlockSpec(shape, lambda *_: (0,) * len(shape), pipeline_mode=pl.Buffered(1))


def _silu(x):
    return x * jax.nn.sigmoid(x)


class _SideCasts:
    def __init__(self, weights, grid):
        self.groups = [w if isinstance(w, (tuple, list)) else (w,) for w in weights]
        self.sources = [m for group in self.groups for m in group]
        self.group_sizes = tuple(len(group) for group in self.groups)
        steps = math.prod(grid)
        strides = [math.prod(grid[i + 1:]) for i in range(len(grid))]
        self.in_specs, self.out_specs, self.out_shapes = [], [], []
        self.vmem_bytes = 0
        for group in self.groups:
            rows = group[0].shape[0]
            cols = sum(m.shape[1] for m in group)
            share = 1
            while (rows * share) % (steps * BF16_SUBLANE_PACK):
                share *= 2
            block_rows = rows * share // steps

            def index(*ids, share=share):
                return (sum(i * s for i, s in zip(ids, strides)) // share, 0)

            self.in_specs += [pl.BlockSpec((block_rows, m.shape[1]), index) for m in group]
            self.out_specs.append(pl.BlockSpec((block_rows, cols), index))
            self.out_shapes.append(jax.ShapeDtypeStruct((rows, cols), BF16))
            self.vmem_bytes += 2 * block_rows * cols * (4 + 2)


def _copy_side_casts(src_refs, dst_refs, group_sizes):
    src_refs = list(src_refs)
    for dst, size in zip(dst_refs, group_sizes):
        col = 0
        for src in src_refs[:size]:
            dst[:, col:col + src.shape[1]] = src[...].astype(BF16)
            col += src.shape[1]
        src_refs = src_refs[size:]


def _layer_norm(z, g, b):
    mu = jnp.mean(z, axis=-1, keepdims=True)
    zc = z - mu
    var = jnp.mean(zc * zc, axis=-1, keepdims=True)
    return zc * lax.rsqrt(var + LN_EPS) * g + b


def _ada_kernel(c_ref, w_ref, b_ref, o_ref):
    a = _silu(c_ref[...]).astype(BF16)
    o_ref[...] = jnp.dot(a, w_ref[...].astype(BF16), preferred_element_type=F32) + b_ref[...]


def _ada_modulation(c_pad, ada_w, ada_b):
    rows, d = c_pad.shape
    n = ada_w.shape[1]
    tn = 1024
    est = 2 * d * tn * 4 + d * tn * 2 + 4 * rows * n
    return pl.pallas_call(
        _ada_kernel,
        out_shape=jax.ShapeDtypeStruct((rows, n), F32),
        grid=(n // tn,),
        in_specs=[pl.BlockSpec((rows, d), lambda j: (0, 0)),
                  pl.BlockSpec((d, tn), lambda j: (0, j)),
                  pl.BlockSpec((1, tn), lambda j: (0, j))],
        out_specs=pl.BlockSpec((rows, tn), lambda j: (0, j)),
        compiler_params=_params(("arbitrary",), est),
        name="ada_modulation",
    )(c_pad, ada_w, ada_b.reshape(1, n))


def _rope_table_kernel(*refs, cast_groups):
    pos_ref, invf_ref, sign_ref = refs[:3]
    n_src = sum(cast_groups)
    cos_ref, sin_ref = refs[3 + n_src:5 + n_src]
    _copy_side_casts(refs[3:3 + n_src], refs[5 + n_src:], cast_groups)
    ang = pos_ref[...].astype(F32) * invf_ref[...]
    cos_ref[...] = jnp.cos(ang)
    sin_ref[...] = jnp.sin(ang) * sign_ref[...]


def _rope_tables(pos_col, cast_weights):
    n = pos_col.shape[0]
    half = HEAD_DIM // 2
    lane = jnp.arange(V7X_LANES)
    inv_freq = ROPE_THETA ** (-jnp.arange(half, dtype=F32) / half)
    invf = inv_freq[lane % half].reshape(1, V7X_LANES)
    sign = jnp.where((lane % HEAD_DIM) < half, -1.0, 1.0).astype(F32).reshape(1, V7X_LANES)
    tm = 2048
    grid = (n // tm,)
    casts = _SideCasts(cast_weights, grid)
    est = 2 * (tm * V7X_LANES * 4) * 3 + casts.vmem_bytes
    outs = pl.pallas_call(
        functools.partial(_rope_table_kernel, cast_groups=casts.group_sizes),
        out_shape=[jax.ShapeDtypeStruct((n, V7X_LANES), F32)] * 2 + casts.out_shapes,
        grid=grid,
        in_specs=[pl.BlockSpec((tm, 1), lambda i: (i, 0)),
                  pl.BlockSpec((1, V7X_LANES), lambda i: (0, 0)),
                  pl.BlockSpec((1, V7X_LANES), lambda i: (0, 0))] + casts.in_specs,
        out_specs=[pl.BlockSpec((tm, V7X_LANES), lambda i: (i, 0))] * 2 + casts.out_specs,
        compiler_params=_params(("arbitrary",), est),
        name="rope_tables",
    )(pos_col, invf, sign, *casts.sources)
    return outs[0], outs[1], outs[2:]


def _rope(a, cos, sin_signed, first_half):
    outs = []
    for s in range(a.shape[1] // V7X_LANES):
        blk = a[:, s * V7X_LANES:(s + 1) * V7X_LANES]
        partner = jnp.where(first_half,
                            pltpu.roll(blk, V7X_LANES - HEAD_DIM // 2, 1),
                            pltpu.roll(blk, HEAD_DIM // 2, 1))
        outs.append(blk * cos + partner * sin_signed)
    return jnp.concatenate(outs, axis=1) if len(outs) > 1 else outs[0]


PROJ_COLS = 1024


def _inproj_even_kernel(x_ref, shift_ref, scale_ref, w_ref, cos_ref, sin_ref, lbl_ref,
                        qa_ref, ka_ref, va_ref, qb_ref, kb_ref, lf_ref, ib_ref, gb_ref, h_scr, *, layer):
    tm = x_ref.shape[0]
    h_scr[...] = (x_ref[...] * (1.0 + scale_ref[...]) + shift_ref[...]).astype(BF16)

    def proj(c0, width=PROJ_COLS):
        return jnp.dot(h_scr[...], w_ref[:, c0:c0 + width], preferred_element_type=F32)

    cos = cos_ref[...]
    sin_signed = sin_ref[...]
    lane = lax.broadcasted_iota(jnp.int32, (tm, V7X_LANES), 1)
    first_half = (lane % HEAD_DIM) < (HEAD_DIM // 2)

    for c0 in range(0, A_Q_WIDTH, PROJ_COLS):
        qa_ref[:, c0:c0 + PROJ_COLS] = (_rope(proj(c0), cos, sin_signed, first_half)
                                        * (HEAD_DIM ** -0.5)).astype(BF16)
    kv = proj(A_Q_WIDTH, 2 * A_KV_WIDTH)
    ka_ref[...] = _rope(kv[:, :A_KV_WIDTH], cos, sin_signed, first_half).astype(BF16)
    va_ref[...] = kv[:, A_KV_WIDTH:].astype(BF16)

    base = A_Q_WIDTH + 2 * A_KV_WIDTH
    lg = lbl_ref[...]
    e = jnp.exp(lg - jnp.max(lg, axis=0, keepdims=True))
    sm = e / jnp.sum(e, axis=0, keepdims=True)
    lb = jnp.sum(sm[0:layer + 1], axis=0, keepdims=True)
    for c0 in range(0, B_WIDTH, PROJ_COLS):
        sl = slice(c0, c0 + PROJ_COLS)
        qb_ref[:, sl] = proj(base + c0).astype(BF16)
        lbc = lb[:, sl]
        fg = lbc + (1.0 - lbc) * jax.nn.sigmoid(proj(base + B_WIDTH + c0))
        kb_ref[:, sl] = (1.0 - fg).astype(BF16)
        lf_ref[:, sl] = jnp.log(fg)
        ib_ref[:, sl] = proj(base + 2 * B_WIDTH + c0).astype(BF16)
        gb_ref[:, sl] = proj(base + 3 * B_WIDTH + c0).astype(BF16)


def _inproj_even(x2, mod3, w_bf, cos, sin_signed, lb_logits, *, seq, layer):
    n, d = x2.shape
    tm = 512
    per_seq = seq // tm
    width = w_bf.shape[1]
    row = lambda i: (i, 0)
    out_shapes = (
        jax.ShapeDtypeStruct((n, A_Q_WIDTH), BF16), jax.ShapeDtypeStruct((n, A_KV_WIDTH), BF16),
        jax.ShapeDtypeStruct((n, A_KV_WIDTH), BF16), jax.ShapeDtypeStruct((n, B_WIDTH), BF16),
        jax.ShapeDtypeStruct((n, B_WIDTH), BF16), jax.ShapeDtypeStruct((n, B_WIDTH), F32),
        jax.ShapeDtypeStruct((n, B_WIDTH), BF16), jax.ShapeDtypeStruct((n, B_WIDTH), BF16))
    out_bytes = sum(tm * s.shape[1] * s.dtype.itemsize for s in out_shapes)
    est = d * width * 2 + 2 * tm * d * 4 + 2 * out_bytes + tm * d * 2 + 6 * tm * PROJ_COLS * 4
    return pl.pallas_call(
        functools.partial(_inproj_even_kernel, layer=layer),
        out_shape=out_shapes,
        grid=(n // tm,),
        in_specs=[pl.BlockSpec((tm, d), row),
                  pl.BlockSpec((None, 1, d), lambda i: ((i // per_seq) * 6 + 0, 0, 0)),
                  pl.BlockSpec((None, 1, d), lambda i: ((i // per_seq) * 6 + 1, 0, 0)),
                  _resident((d, width)),
                  pl.BlockSpec((tm, V7X_LANES), row),
                  pl.BlockSpec((tm, V7X_LANES), row),
                  pl.BlockSpec(lb_logits.shape, lambda i: (0, 0))],
        out_specs=tuple(pl.BlockSpec((tm, s.shape[1]), row) for s in out_shapes),
        scratch_shapes=[pltpu.VMEM((tm, d), BF16)],
        compiler_params=_params(("arbitrary",), est),
        name="inproj_even",
    )(x2, mod3, mod3, w_bf, cos, sin_signed, lb_logits)


def _swa_kernel(*refs, cast_groups):
    sink_ref, q_ref, kp_ref, kc_ref, vp_ref, vc_ref = refs[:6]
    n_src = sum(cast_groups)
    o_ref = refs[6 + n_src]
    _copy_side_casts(refs[6:6 + n_src], refs[7 + n_src:], cast_groups)
    first_step = pl.program_id(1) == 0
    grp = A_Q_HEADS // A_KV_HEADS
    assert HEAD_DIM * 2 == V7X_LANES and grp == 4
    r = lax.broadcasted_iota(jnp.int32, (WINDOW, 2 * WINDOW), 0)
    c = lax.broadcasted_iota(jnp.int32, (WINDOW, 2 * WINDOW), 1)
    rel = r + WINDOW - c
    band = (rel >= 0) & (rel < WINDOW)
    band_first = band & ((c >= WINDOW) | jnp.logical_not(first_step))
    lane = lax.broadcasted_iota(jnp.int32, (1, V7X_LANES), 1)
    half_mask = [jnp.where(lane < HEAD_DIM, 1.0, 0.0).astype(BF16),
                 jnp.where(lane >= HEAD_DIM, 1.0, 0.0).astype(BF16)]
    low_half = lax.broadcasted_iota(jnp.int32, (WINDOW, V7X_LANES), 1) < HEAD_DIM

    def frame(prev_ref, cur_ref, sub, kv_tile):
        own = cur_ref[sub * WINDOW:(sub + 1) * WINDOW, kv_tile]
        before = prev_ref[:, kv_tile] if sub == 0 else cur_ref[(sub - 1) * WINDOW:sub * WINDOW, kv_tile]
        return jnp.concatenate([before, own], axis=0)

    def scores(sub, g):
        kv_tile = slice((g // 2) * V7X_LANES, (g // 2 + 1) * V7X_LANES)
        kv_half = g % 2
        q_rows = []
        for e in range(grp):
            q_tile = (2 * g + e // 2) * V7X_LANES
            qt = q_ref[sub * WINDOW:(sub + 1) * WINDOW, q_tile:q_tile + V7X_LANES]
            if e % 2 != kv_half:
                qt = pltpu.roll(qt, HEAD_DIM, 1)
            q_rows.append(qt * half_mask[kv_half])
        qs = jnp.concatenate(q_rows, axis=0)
        return lax.dot_general(qs, frame(kp_ref, kc_ref, sub, kv_tile), (((1,), (1,)), ((), ())),
                               preferred_element_type=F32)

    units = [(sub, g) for sub in range(q_ref.shape[0] // WINDOW) for g in range(A_KV_HEADS)]
    s_next = scores(*units[0])
    for idx, (sub, g) in enumerate(units):
        kv_tile = slice((g // 2) * V7X_LANES, (g // 2 + 1) * V7X_LANES)
        kv_half = g % 2
        qrows = slice(sub * WINDOW, (sub + 1) * WINDOW)
        valid = band_first if sub == 0 else band
        s_all = s_next
        if idx + 1 < len(units):
            s_next = scores(*units[idx + 1])
        p_rows = []
        denoms = []
        for e in range(grp):
            s = jnp.where(valid, s_all[e * WINDOW:(e + 1) * WINDOW], NEG_INF)
            sink = sink_ref[grp * g + e]
            m = jnp.maximum(jnp.max(s, axis=-1, keepdims=True), sink)
            p = jnp.exp(s - m)
            denoms.append(jnp.sum(p, axis=-1, keepdims=True) + jnp.exp(sink - m))
            p_rows.append(p.astype(BF16))
        o_all = jnp.dot(jnp.concatenate(p_rows, axis=0), frame(vp_ref, vc_ref, sub, kv_tile),
                        preferred_element_type=F32)
        for u in range(grp // 2):
            o_even = o_all[(2 * u) * WINDOW:(2 * u + 1) * WINDOW] / denoms[2 * u]
            o_odd = o_all[(2 * u + 1) * WINDOW:(2 * u + 2) * WINDOW] / denoms[2 * u + 1]
            if kv_half == 0:
                tile = jnp.where(low_half, o_even, pltpu.roll(o_odd, HEAD_DIM, 1))
            else:
                tile = jnp.where(low_half, pltpu.roll(o_even, HEAD_DIM, 1), o_odd)
            out_tile = (2 * g + u) * V7X_LANES
            o_ref[qrows, out_tile:out_tile + V7X_LANES] = tile.astype(BF16)


SWA_BLOCKS_PER_STEP = 1


def _swa_attention(q_a, k_a, v_a, sinks, cast_weights, *, batch, seq):
    n = q_a.shape[0]
    tq = SWA_BLOCKS_PER_STEP * WINDOW
    nb = seq // tq
    grid = (batch, nb)
    casts = _SideCasts(cast_weights, grid)
    cur = lambda b, i: (b * nb + i, 0)
    prev = lambda b, i: (jnp.maximum((b * nb + i) * SWA_BLOCKS_PER_STEP - 1, 0), 0)
    est = (2 * (2 * tq * A_Q_WIDTH * 2 + 2 * (tq + WINDOW) * A_KV_WIDTH * 2) + 24 * WINDOW * 2 * WINDOW * 4
           + casts.vmem_bytes)
    outs = pl.pallas_call(
        functools.partial(_swa_kernel, cast_groups=casts.group_sizes),
        out_shape=[jax.ShapeDtypeStruct((n, A_Q_WIDTH), BF16)] + casts.out_shapes,
        grid=grid,
        in_specs=[pl.BlockSpec(memory_space=pltpu.SMEM),
                  pl.BlockSpec((tq, A_Q_WIDTH), cur),
                  pl.BlockSpec((WINDOW, A_KV_WIDTH), prev),
                  pl.BlockSpec((tq, A_KV_WIDTH), cur),
                  pl.BlockSpec((WINDOW, A_KV_WIDTH), prev),
                  pl.BlockSpec((tq, A_KV_WIDTH), cur)] + casts.in_specs,
        out_specs=[pl.BlockSpec((tq, A_Q_WIDTH), cur)] + casts.out_specs,
        compiler_params=_params(("arbitrary", "arbitrary"), est),
        name="swa_attention",
    )(sinks, q_a, k_a, k_a, v_a, v_a, *casts.sources)
    return outs[0], outs[1:]


def _split3(x):
    hi = x.astype(BF16)
    r1 = x - hi.astype(F32)
    mid = r1.astype(BF16)
    lo = (r1 - mid.astype(F32)).astype(BF16)
    return hi, mid, lo


GLA_GROUP = 256


def _gla_kernel(*refs, dk, dv, heads_per_step, cast_groups):
    q_ref, k_ref, v_ref, lf_ref, g_ref, nw_ref = refs[:6]
    n_src = sum(cast_groups)
    o_ref = refs[6 + n_src]
    s_scr = refs[-1]
    _copy_side_casts(refs[6:6 + n_src], refs[7 + n_src:-1], cast_groups)

    @pl.when(pl.program_id(2) == 0)
    def _():
        s_scr[...] = jnp.zeros_like(s_scr)

    tb = q_ref.shape[0]
    assert GLA_GROUP == 4 * CHUNK
    row = lax.broadcasted_iota(jnp.int32, (GLA_GROUP, GLA_GROUP), 0)
    col = lax.broadcasted_iota(jnp.int32, (GLA_GROUP, GLA_GROUP), 1)
    rc = row // CHUNK
    cc = col // CHUNK
    same_chunk = (rc == cc) & (row >= col)
    next_chunk = (rc == cc + 1) & (rc % 2 == 1)
    far_chunk = (rc >= 2) & (cc <= 1)
    tril = jnp.where(same_chunk, 1.0, 0.0).astype(BF16)
    nw = nw_ref[...]
    one = jnp.ones((1, dk), F32)
    nt = (((1,), (1,)), ((), ()))

    def by_chunk(vecs):
        return jnp.concatenate([jnp.broadcast_to(v, (CHUNK, dk)) for v in vecs], axis=0)

    def prepare(hh, gi):
        rows = slice(gi * GLA_GROUP, (gi + 1) * GLA_GROUP)
        kcols = slice(hh * dk, (hh + 1) * dk)
        vcols = slice(hh * dv, (hh + 1) * dv)
        hi, mid, lo = _split3(lf_ref[rows, kcols])
        bb = jnp.dot(tril, jnp.concatenate([hi, mid, lo], axis=1), preferred_element_type=F32)
        b = bb[:, :dk] + bb[:, dk:2 * dk] + bb[:, 2 * dk:]
        l0, l1, l2, l3 = [b[(c + 1) * CHUNK - 1:(c + 1) * CHUNK, :] for c in range(4)]
        b_mid = by_chunk([b[c * CHUNK + CHUNK // 2 - 1:c * CHUNK + CHUNK // 2, :] for c in range(4)])
        q = q_ref[rows, kcols].astype(F32)
        k = k_ref[rows, kcols].astype(F32)
        v = v_ref[rows, vcols]
        q_loc = q * jnp.exp(b)
        k_loc = k * jnp.exp(by_chunk([l0, l1, l2, l3]) - b)
        q_i = (q * jnp.exp(b - b_mid)).astype(BF16)
        k_i = (k * jnp.exp(b_mid - b)).astype(BF16)
        q_far = (q_loc * by_chunk([one, one, one, jnp.exp(l2)])).astype(BF16)
        k_far = (k_loc * by_chunk([jnp.exp(l1), one, one, one])).astype(BF16)
        q_grp = (q_loc * by_chunk([one, jnp.exp(l0), jnp.exp(l0 + l1), jnp.exp(l0 + l1 + l2)])).astype(BF16)
        k_grp = (k_loc * by_chunk([jnp.exp(l1 + l2 + l3), jnp.exp(l2 + l3), jnp.exp(l3), one])).astype(BF16)
        a_same = lax.dot_general(q_i, k_i, nt, preferred_element_type=F32)
        a_next = lax.dot_general(q_loc.astype(BF16), k_loc.astype(BF16), nt, preferred_element_type=F32)
        a_far = lax.dot_general(q_far, k_far, nt, preferred_element_type=F32)
        a = jnp.where(same_chunk, a_same, jnp.where(next_chunk, a_next, jnp.where(far_chunk, a_far, 0.0)))
        o_intra = jnp.dot(a.astype(BF16), v, preferred_element_type=F32)
        kv = lax.dot_general(k_grp, v, (((0,), (0,)), ((), ())), preferred_element_type=F32)
        decay = jnp.exp(jnp.transpose(jnp.broadcast_to(l0 + l1 + l2 + l3, (V7X_LANES, dk))))
        return rows, vcols, o_intra, q_grp, kv, decay

    def finish(state, prepared):
        rows, vcols, o_intra, q_grp, kv, decay = prepared
        o = o_intra + jnp.dot(q_grp, state.astype(BF16), preferred_element_type=F32)
        o = o * lax.rsqrt(jnp.mean(o * o, axis=-1, keepdims=True) + RMS_EPS)
        o_ref[rows, vcols] = ((o * nw) * _silu(g_ref[rows, vcols].astype(F32))).astype(BF16)
        return state * jnp.tile(decay, (1, dv // V7X_LANES)) + kv

    units = [(hh, gi) for gi in range(tb // GLA_GROUP) for hh in range(heads_per_step)]
    states = [s_scr[hh] for hh in range(heads_per_step)]
    pending = prepare(*units[0])
    for idx, (hh, gi) in enumerate(units):
        upcoming = prepare(*units[idx + 1]) if idx + 1 < len(units) else None
        states[hh] = finish(states[hh], pending)
        pending = upcoming
    for hh in range(heads_per_step):
        s_scr[hh] = states[hh]


def _gla(q, k, v, log_f, g, norm_w, cast_weights, *, batch, seq, heads, dk, dv, heads_per_step, time_block, name):
    n = q.shape[0]
    tb = time_block
    nt = seq // tb
    hp = heads_per_step
    grid = (batch, heads // hp, nt)
    casts = _SideCasts(cast_weights, grid)
    idx = lambda b, h, t: (b * nt + t, h)
    est = (2 * tb * hp * (3 * dk * 4 + dv * 2 + dv * 4 + dv * 2) + hp * dk * dv * 4 + 4 * dk * dv * 4
           + 16 * GLA_GROUP * max(3 * dk, dv) * 4 + casts.vmem_bytes)
    outs = pl.pallas_call(
        functools.partial(_gla_kernel, dk=dk, dv=dv, heads_per_step=hp, cast_groups=casts.group_sizes),
        out_shape=[jax.ShapeDtypeStruct((n, heads * dv), BF16)] + casts.out_shapes,
        grid=grid,
        in_specs=[pl.BlockSpec((tb, hp * dk), idx), pl.BlockSpec((tb, hp * dk), idx),
                  pl.BlockSpec((tb, hp * dv), idx), pl.BlockSpec((tb, hp * dk), idx),
                  pl.BlockSpec((tb, hp * dv), idx),
                  pl.BlockSpec((1, dv), lambda b, h, t: (0, 0))] + casts.in_specs,
        out_specs=[pl.BlockSpec((tb, hp * dv), idx)] + casts.out_specs,
        scratch_shapes=[pltpu.VMEM((hp, dk, dv), F32)],
        compiler_params=_params(("arbitrary", "arbitrary", "arbitrary"), est),
        name=name,
    )(q, k, v, log_f, g, norm_w.reshape(1, dv), *casts.sources)
    return outs[0], outs[1:]


OUTPROJ_SUB_ROWS = 256


def _outproj_ln_kernel(*refs, k_sizes):
    lhs_refs = refs[:len(k_sizes)]
    w_ref, x_ref, gate_ref, g_ref, b_ref, o_ref = refs[len(k_sizes):]

    def project(rows):
        y = None
        off = 0
        for r, ks in zip(lhs_refs, k_sizes):
            part = jnp.dot(r[rows, :], w_ref[off:off + ks, :], preferred_element_type=F32)
            y = part if y is None else y + part
            off += ks
        return y

    subs = [slice(r0, r0 + OUTPROJ_SUB_ROWS) for r0 in range(0, o_ref.shape[0], OUTPROJ_SUB_ROWS)]
    y_next = project(subs[0])
    for idx, rows in enumerate(subs):
        y = y_next
        if idx + 1 < len(subs):
            y_next = project(subs[idx + 1])
        o_ref[rows, :] = _layer_norm(ALPHA * x_ref[rows, :] + gate_ref[...] * y, g_ref[...], b_ref[...])


def _outproj_ln(lhs_list, w_bf, x2, mod3, gate_slot, ln_g, ln_b, *, seq):
    n, d = x2.shape
    tm = 512
    per_seq = seq // tm
    row = lambda i: (i, 0)
    k_sizes = tuple(a.shape[1] for a in lhs_list)
    k_total = sum(k_sizes)
    vec = pl.BlockSpec((1, d), lambda i: (0, 0))
    est = k_total * d * 2 + 2 * tm * k_total * 2 + 4 * tm * d * 4 + 4 * tm * d * 4
    return pl.pallas_call(
        functools.partial(_outproj_ln_kernel, k_sizes=k_sizes),
        out_shape=jax.ShapeDtypeStruct((n, d), F32),
        grid=(n // tm,),
        in_specs=[pl.BlockSpec((tm, ks), row) for ks in k_sizes] + [
            _resident((k_total, d)),
            pl.BlockSpec((tm, d), row),
            pl.BlockSpec((None, 1, d), lambda i: ((i // per_seq) * 6 + gate_slot, 0, 0)),
            vec, vec],
        out_specs=pl.BlockSpec((tm, d), row),
        compiler_params=_params(("arbitrary",), est),
        name="outproj_ln",
    )(*lhs_list, w_bf, x2, mod3, ln_g.reshape(1, d), ln_b.reshape(1, d))


FFN_PASSES = 2
FFN_PASS_TILE = 1024
FFN_CARRY = 8


def _ffn_pass_kernel(*refs, per_seq, last):
    x_ref, shift_ref, scale_ref, wu_ref, wv_ref, cw_ref, cb_ref, wd_ref = refs[:8]
    if last:
        part_ref, gate_ref, lng_ref, lnb_ref, o_ref, h_scr, u_scr = refs[8:]
    else:
        o_ref, h_scr, u_scr = refs[8:]
    i = pl.program_id(0)
    tm = x_ref.shape[0]
    width = wd_ref.shape[0]
    h_scr[...] = (x_ref[...] * (1.0 + scale_ref[...]) + shift_ref[...]).astype(BF16)

    @pl.when(i == 0)
    def _():
        u_scr[...] = jnp.zeros_like(u_scr)

    inside_sequence = i % per_seq != 0
    u_scr[0:FFN_CARRY, :] = jnp.where(inside_sequence, u_scr[tm:tm + FFN_CARRY, :], 0.0)
    tiles = [(c0, min(FFN_PASS_TILE, width - c0)) for c0 in range(0, width, FFN_PASS_TILE)]

    def up(t):
        c0, w = tiles[t]
        cols = slice(c0, c0 + w)
        u_scr[FFN_CARRY:, cols] = jnp.dot(h_scr[...], wu_ref[:, cols], preferred_element_type=F32)
        return jnp.dot(h_scr[...], wv_ref[:, cols], preferred_element_type=F32)

    def down(t, v):
        c0, w = tiles[t]
        cols = slice(c0, c0 + w)
        u = (cw_ref[2:3, cols] * u_scr[FFN_CARRY:FFN_CARRY + tm, cols]
             + cw_ref[1:2, cols] * u_scr[FFN_CARRY - 1:FFN_CARRY - 1 + tm, cols]
             + cw_ref[0:1, cols] * u_scr[FFN_CARRY - 2:FFN_CARRY - 2 + tm, cols]) + cb_ref[:, cols]
        act = (_silu(u) * v).astype(BF16)
        return jnp.dot(act, wd_ref[cols, :], preferred_element_type=F32)

    acc = None
    v_next = up(0)
    for t in range(len(tiles)):
        v = v_next
        if t + 1 < len(tiles):
            v_next = up(t + 1)
        part = down(t, v)
        acc = part if acc is None else acc + part
    if last:
        y = acc + part_ref[...]
        o_ref[...] = _layer_norm(ALPHA * x_ref[...] + gate_ref[...] * y, lng_ref[...], lnb_ref[...])
    else:
        o_ref[...] = acc


def _ffn_two_pass(x2, mod3, w_up_bf, conv_w, conv_b, w_down_bf, ln_g, ln_b, *, seq):
    assert FFN_PASSES == 2
    n, d = x2.shape
    tm = 256
    per_seq = seq // tm
    width = D_FF // FFN_PASSES
    row = lambda i: (i, 0)
    slot = lambda s: (lambda i: ((i // per_seq) * 6 + s, 0, 0))
    vec = pl.BlockSpec((1, d), lambda i: (0, 0))
    conv_b2 = conv_b.reshape(1, D_FF)
    est = (3 * d * width * 2 + 6 * tm * d * 4 + tm * d * 2 + (tm + FFN_CARRY) * width * 4
           + 3 * tm * d * 4 + 8 * tm * FFN_PASS_TILE * 4)
    partial = None
    for p in range(FFN_PASSES):
        last = p == FFN_PASSES - 1
        in_specs = [pl.BlockSpec((tm, d), row),
                    pl.BlockSpec((None, 1, d), slot(3)),
                    pl.BlockSpec((None, 1, d), slot(4)),
                    pl.BlockSpec((d, width), lambda i, p=p: (0, p), pipeline_mode=pl.Buffered(1)),
                    pl.BlockSpec((d, width), lambda i, p=p: (0, FFN_PASSES + p), pipeline_mode=pl.Buffered(1)),
                    pl.BlockSpec((CONV_WIDTH, width), lambda i, p=p: (0, p)),
                    pl.BlockSpec((1, width), lambda i, p=p: (0, p)),
                    pl.BlockSpec((width, d), lambda i, p=p: (p, 0), pipeline_mode=pl.Buffered(1))]
        args = [x2, mod3, mod3, w_up_bf, w_up_bf, conv_w, conv_b2, w_down_bf]
        if partial is not None:
            in_specs.append(pl.BlockSpec((tm, d), row))
            args.append(partial)
        if last:
            in_specs += [pl.BlockSpec((None, 1, d), slot(5)), vec, vec]
            args += [mod3, ln_g.reshape(1, d), ln_b.reshape(1, d)]
        partial = pl.pallas_call(
            functools.partial(_ffn_pass_kernel, per_seq=per_seq, last=last),
            out_shape=jax.ShapeDtypeStruct((n, d), F32),
            grid=(n // tm,),
            in_specs=in_specs,
            out_specs=pl.BlockSpec((tm, d), row),
            scratch_shapes=[pltpu.VMEM((tm, d), BF16), pltpu.VMEM((tm + FFN_CARRY, width), F32)],
            compiler_params=_params(("arbitrary",), est),
            name="ffn_last" if last else "ffn_part",
        )(*args)
    return partial


def _inproj_odd_kernel(x_ref, shift_ref, scale_ref, w_ref, wgb_ref, bg_ref,
                       q_ref, k_ref, v_ref, g_ref, lf_ref, h_scr):
    h_scr[...] = (x_ref[...] * (1.0 + scale_ref[...]) + shift_ref[...]).astype(BF16)

    def proj(c0, width=PROJ_COLS):
        return jnp.dot(h_scr[...], w_ref[:, c0:c0 + width], preferred_element_type=F32)

    low = proj(2 * C_KEY_DIM + 2 * C_VALUE_DIM, RANK_PAD).astype(BF16)

    def log_decay(c0, width):
        sl = slice(c0, c0 + width)
        gk = jnp.dot(low, wgb_ref[:, sl], preferred_element_type=F32) + bg_ref[:, sl]
        lf_ref[:, sl] = jax.nn.log_sigmoid(gk) * (1.0 / GATE_NORMALIZER)

    piece = PROJ_COLS // 2
    for c0 in range(0, C_KEY_DIM, PROJ_COLS):
        sl = slice(c0, c0 + PROJ_COLS)
        q_ref[:, sl] = (proj(c0) * (C_DK ** -0.5)).astype(BF16)
        log_decay(c0, piece)
        k_ref[:, sl] = proj(C_KEY_DIM + c0).astype(BF16)
        log_decay(c0 + piece, piece)
    for c0 in range(0, C_VALUE_DIM, PROJ_COLS):
        sl = slice(c0, c0 + PROJ_COLS)
        v_ref[:, sl] = proj(2 * C_KEY_DIM + c0).astype(BF16)
        g_ref[:, sl] = proj(2 * C_KEY_DIM + C_VALUE_DIM + c0).astype(BF16)


def _inproj_odd(x2, mod3, w_bf, w_gk_b_bf, b_gk, *, seq):
    n, d = x2.shape
    tm = 512
    per_seq = seq // tm
    width = w_bf.shape[1]
    row = lambda i: (i, 0)
    out_shapes = (
        jax.ShapeDtypeStruct((n, C_KEY_DIM), BF16), jax.ShapeDtypeStruct((n, C_KEY_DIM), BF16),
        jax.ShapeDtypeStruct((n, C_VALUE_DIM), BF16), jax.ShapeDtypeStruct((n, C_VALUE_DIM), BF16),
        jax.ShapeDtypeStruct((n, C_KEY_DIM), F32))
    out_bytes = sum(tm * s.shape[1] * s.dtype.itemsize for s in out_shapes)
    est = d * width * 2 + 2 * tm * d * 4 + 2 * out_bytes + tm * d * 2 + 6 * tm * PROJ_COLS * 4
    return pl.pallas_call(
        _inproj_odd_kernel,
        out_shape=out_shapes,
        grid=(n // tm,),
        in_specs=[pl.BlockSpec((tm, d), row),
                  pl.BlockSpec((None, 1, d), lambda i: ((i // per_seq) * 6 + 0, 0, 0)),
                  pl.BlockSpec((None, 1, d), lambda i: ((i // per_seq) * 6 + 1, 0, 0)),
                  _resident((d, width)),
                  _resident((RANK_PAD, C_KEY_DIM)),
                  pl.BlockSpec((1, C_KEY_DIM), lambda i: (0, 0))],
        out_specs=tuple(pl.BlockSpec((tm, s.shape[1]), row) for s in out_shapes),
        scratch_shapes=[pltpu.VMEM((tm, d), BF16)],
        compiler_params=_params(("arbitrary",), est),
        name="inproj_odd",
    )(x2, mod3, mod3, w_bf, w_gk_b_bf, b_gk.reshape(1, C_KEY_DIM))


def kernel(x, c, positions,
           ada_w0, ada_b0, mix_w_in0, mix_w_out0, attn_sinks0, hgrn_lb_logits, hgrn_norm_w0,
           ln_mix_g0, ln_mix_b0, ffn_w_up0, ffn_conv_w0, ffn_conv_b0, ffn_w_down0, ln_ffn_g0, ln_ffn_b0,
           ada_w1, ada_b1, mix_w_in1, gla_w_gk_a1, gla_w_gk_b1, gla_b_gk1, gla_norm_w1, mix_w_out1,
           ln_mix_g1, ln_mix_b1, ffn_w_up1, ffn_conv_w1, ffn_conv_b1, ffn_w_down1, ln_ffn_g1, ln_ffn_b1):
    batch, seq, d = x.shape
    n = batch * seq
    x2 = x.reshape(n, d)
    c_pad = jnp.pad(c, ((0, 8 - batch), (0, 0)))

    def modulation(ada_w, ada_b):
        mod = _ada_modulation(c_pad, ada_w, ada_b)[:batch]
        return mod.reshape(batch * 6, 1, d)

    mod3 = modulation(ada_w0, ada_b0)
    cos, sin_signed, (w_in0,) = _rope_tables(positions.reshape(n, 1), [mix_w_in0])
    q_a, k_a, v_a, q_b, k_b, lf_b, i_b, g_b = _inproj_even(
        x2, mod3, w_in0, cos, sin_signed, hgrn_lb_logits, seq=seq, layer=0)
    w_gk_a = jnp.pad(gla_w_gk_a1, ((0, 0), (0, RANK_PAD - GATE_RANK)))
    o_a, (w_out0, w_in1, w_out1) = _swa_attention(
        q_a, k_a, v_a, attn_sinks0, [mix_w_out0, (mix_w_in1, w_gk_a), mix_w_out1], batch=batch, seq=seq)
    o_b, (w_up0, w_down0) = _gla(
        q_b, k_b, i_b, lf_b, g_b, hgrn_norm_w0, [ffn_w_up0, ffn_w_down0], batch=batch, seq=seq,
        heads=B_HEADS, dk=B_HEAD_DIM, dv=B_HEAD_DIM, heads_per_step=8, time_block=512, name="hgrn2")
    x2 = _outproj_ln([o_a, o_b], w_out0, x2, mod3, 2, ln_mix_g0, ln_mix_b0, seq=seq)
    x2 = _ffn_two_pass(x2, mod3, w_up0, ffn_conv_w0, ffn_conv_b0, w_down0, ln_ffn_g0, ln_ffn_b0, seq=seq)

    mod3 = modulation(ada_w1, ada_b1)
    w_gk_b = jnp.pad(gla_w_gk_b1, ((0, RANK_PAD - GATE_RANK), (0, 0))).astype(BF16)
    q_c, k_c, v_c, g_c, lf_c = _inproj_odd(x2, mod3, w_in1, w_gk_b, gla_b_gk1, seq=seq)
    o_c, (w_up1, w_down1) = _gla(
        q_c, k_c, v_c, lf_c, g_c, gla_norm_w1, [ffn_w_up1, ffn_w_down1], batch=batch, seq=seq,
        heads=C_HEADS, dk=C_DK, dv=C_DV, heads_per_step=1, time_block=1024, name="gla")
    x2 = _outproj_ln([o_c], w_out1, x2, mod3, 2, ln_mix_g1, ln_mix_b1, seq=seq)
    x2 = _ffn_two_pass(x2, mod3, w_up1, ffn_conv_w1, ffn_conv_b1, w_down1, ln_ffn_g1, ln_ffn_b1, seq=seq)
    return x2.reshape(batch, seq, d)
```
